```python
import math
import jax, jax.numpy as jnp
from jax import lax
import numpy as np

D_MODEL = 1024
BATCH = 1
SEQ = 16384
DEPTH = 1

CHUNK = 64
CONV_WIDTH = 1024
CONV_K = 3
SSM_WIDTH = 512
SSM_GROUP = 16
SSM_GROUPS = SSM_WIDTH // SSM_GROUP
SSM_STATE = 64
DT_MIN = 1e-3
DT_MAX = 1e-1
N_BRANCHES = 2
IN_COLS = 3 * CONV_WIDTH + SSM_WIDTH + N_BRANCHES * D_MODEL
N_EXPERT_GROUPS = 4
EXPERTS_PER_GROUP = 8
N_EXPERTS = N_EXPERT_GROUPS * EXPERTS_PER_GROUP
EXPERT_FF = 512
TOP_K_INNER = 2
EPS = 1e-6

kernel_name = "hybrid_conv_s5_hiermoe_block"


def rms_norm(x, g):
    xf = x.astype(jnp.float32)
    y = xf * lax.rsqrt(jnp.mean(xf * xf, axis=-1, keepdims=True) + EPS)
    return (y * g.astype(jnp.float32)).astype(x.dtype)


def short_conv_branch(xc, bg, cg, conv_w, conv_b, w_out):
    v = cg * xc
    y = lax.conv_general_dilated(
        v, conv_w.astype(v.dtype), window_strides=(1,), padding=[(CONV_K - 1, 0)],
        dimension_numbers=("NWC", "WIO", "NWC"), feature_group_count=CONV_WIDTH)
    y = y + conv_b
    return (bg * y) @ w_out


def _ssm_combine(e1, e2):
    a1, s1 = e1
    a2, s2 = e2
    return a1 * a2, a2 * s1 + s2


def s5_branch(u, a_re, a_im, log_dt, b_re, b_im, c_re, c_im, d_skip, w_glu_a, w_glu_b):
    bsz, length, _ = u.shape
    f32 = jnp.float32
    uf = u.astype(f32).reshape(bsz, length, SSM_GROUPS, SSM_GROUP)
    lam = lax.complex(a_re.astype(f32), a_im.astype(f32))
    dt = jnp.exp(log_dt.astype(f32))[:, None]
    a_bar = jnp.exp(lam * dt)
    b = lax.complex(b_re.astype(f32), b_im.astype(f32))
    b_bar = ((a_bar - 1.0) / lam)[..., None] * b
    bu = lax.complex(jnp.einsum("gnh,blgh->blgn", b_bar.real, uf),
                     jnp.einsum("gnh,blgh->blgn", b_bar.imag, uf))
    a_seq = jnp.broadcast_to(a_bar[None, None], bu.shape)
    _, states = lax.associative_scan(_ssm_combine, (a_seq, bu), axis=1)
    y = (jnp.einsum("ghn,blgn->blgh", c_re.astype(f32), states.real)
         - jnp.einsum("ghn,blgn->blgh", c_im.astype(f32), states.imag))
    y = y.reshape(bsz, length, SSM_WIDTH) + d_skip.astype(f32) * u.astype(f32)
    y = jax.nn.gelu(y).astype(u.dtype)
    return (y @ w_glu_a) * jax.nn.sigmoid(y @ w_glu_b)


def hier_moe(h, w_route_group, b_route_group, w_route_expert, b_route_expert,
             w_gate, w_up, w_down):
    bsz, length, d = h.shape
    t = bsz * length
    hf = h.reshape(t, d)
    f32 = jnp.float32
    pg = jax.nn.softmax((hf @ w_route_group + b_route_group).astype(f32), axis=-1)
    pg_top, g_idx = lax.top_k(pg, 1)
    le = (hf @ w_route_expert + b_route_expert).astype(f32).reshape(t, N_EXPERT_GROUPS, EXPERTS_PER_GROUP)
    le_sel = jnp.take_along_axis(le, g_idx[:, :, None], axis=1)[:, 0]
    pe = jax.nn.softmax(le_sel, axis=-1)
    pe_top, e_idx = lax.top_k(pe, TOP_K_INNER)
    pe_top = pe_top / jnp.sum(pe_top, axis=-1, keepdims=True)
    weights = pg_top * pe_top
    expert_ids = g_idx * EXPERTS_PER_GROUP + e_idx
    combine = jnp.sum(jax.nn.one_hot(expert_ids, N_EXPERTS, dtype=f32) * weights[..., None], axis=1)
    out = jnp.zeros((t, d), f32)
    for e in range(N_EXPERTS):
        a = jax.nn.silu(hf @ w_gate[e]) * (hf @ w_up[e])
        out = out + combine[:, e:e + 1] * (a @ w_down[e]).astype(f32)
    return out.astype(h.dtype).reshape(bsz, length, d)


def setup_inputs(seed: int = 0) -> dict:
    key = jax.random.key(seed)
    ks = jax.random.split(key, 24)
    f32 = jnp.float32
    nrm = lambda k, shape, scale: jax.random.normal(k, shape, f32) * scale
    L_, D = DEPTH, D_MODEL
    G, N, H = SSM_GROUPS, SSM_STATE, SSM_GROUP
    return {
        "x": nrm(ks[0], (BATCH, SEQ, D), 1.0),
        "norm_mix": 1.0 + nrm(ks[1], (L_, D), 0.01),
        "w_in": nrm(ks[2], (L_, D, IN_COLS), D ** -0.5),
        "conv_w": nrm(ks[3], (L_, CONV_K, 1, CONV_WIDTH), CONV_K ** -0.5),
        "conv_b": nrm(ks[4], (L_, CONV_WIDTH), 0.01),
        "w_conv_out": nrm(ks[5], (L_, CONV_WIDTH, D), CONV_WIDTH ** -0.5),
        "ssm_a_re": -0.5 + nrm(ks[6], (L_, G, N), 0.01),
        "ssm_a_im": jnp.pi * jnp.arange(N, dtype=f32)[None, None, :] + nrm(ks[7], (L_, G, N), 0.01),
        "ssm_log_dt": jax.random.uniform(ks[8], (L_, G), f32, math.log(DT_MIN), math.log(DT_MAX)),
        "ssm_b_re": nrm(ks[9], (L_, G, N, H), (2 * H) ** -0.5),
        "ssm_b_im": nrm(ks[10], (L_, G, N, H), (2 * H) ** -0.5),
        "ssm_c_re": nrm(ks[11], (L_, G, H, N), (2 * N) ** -0.5),
        "ssm_c_im": nrm(ks[12], (L_, G, H, N), (2 * N) ** -0.5),
        "ssm_d": nrm(ks[13], (L_, SSM_WIDTH), 1.0),
        "w_glu_a": nrm(ks[14], (L_, SSM_WIDTH, D), SSM_WIDTH ** -0.5),
        "w_glu_b": nrm(ks[15], (L_, SSM_WIDTH, D), SSM_WIDTH ** -0.5),
        "gate_bias": nrm(ks[16], (L_, N_BRANCHES, D), 0.01),
        "w_o": nrm(ks[17], (L_, D, D), D ** -0.5),
        "norm_ffn": 1.0 + nrm(ks[18], (L_, D), 0.01),
        "w_route_group": nrm(ks[19], (L_, D, N_EXPERT_GROUPS), D ** -0.5),
        "b_route_group": nrm(ks[20], (L_, N_EXPERT_GROUPS), 0.01),
        "w_route_expert": nrm(ks[21], (L_, D, N_EXPERTS), D ** -0.5),
        "b_route_expert": nrm(ks[22], (L_, N_EXPERTS), 0.01),
        "w_gate": nrm(jax.random.fold_in(ks[23], 0), (L_, N_EXPERTS, D, EXPERT_FF), D ** -0.5),
        "w_up": nrm(jax.random.fold_in(ks[23], 1), (L_, N_EXPERTS, D, EXPERT_FF), D ** -0.5),
        "w_down": nrm(jax.random.fold_in(ks[23], 2), (L_, N_EXPERTS, EXPERT_FF, D), EXPERT_FF ** -0.5),
        "norm_final": 1.0 + nrm(jax.random.fold_in(ks[23], 3), (D,), 0.01),
    }


def reference(x, norm_mix, w_in, conv_w, conv_b, w_conv_out, ssm_a_re, ssm_a_im, ssm_log_dt,
              ssm_b_re, ssm_b_im, ssm_c_re, ssm_c_im, ssm_d, w_glu_a, w_glu_b, gate_bias, w_o,
              norm_ffn, w_route_group, b_route_group, w_route_expert, b_route_expert,
              w_gate, w_up, w_down, norm_final):
    bsz, length, d = x.shape
    c0, c1, c2, c3 = CONV_WIDTH, 2 * CONV_WIDTH, 3 * CONV_WIDTH, 3 * CONV_WIDTH + SSM_WIDTH
    for l in range(DEPTH):
        h = rms_norm(x, norm_mix[l])
        p = h @ w_in[l]
        xc, bg, cg, u, gl = p[..., :c0], p[..., c0:c1], p[..., c1:c2], p[..., c2:c3], p[..., c3:]
        gates = jax.nn.sigmoid(gl.reshape(bsz, length, N_BRANCHES, d) + gate_bias[l])
        ya = short_conv_branch(xc, bg, cg, conv_w[l], conv_b[l], w_conv_out[l])
        yb = s5_branch(u, ssm_a_re[l], ssm_a_im[l], ssm_log_dt[l], ssm_b_re[l], ssm_b_im[l],
                       ssm_c_re[l], ssm_c_im[l], ssm_d[l], w_glu_a[l], w_glu_b[l])
        mixed = gates[..., 0, :] * ya + gates[..., 1, :] * yb
        x = x + mixed @ w_o[l]
        h2 = rms_norm(x, norm_ffn[l])
        x = x + hier_moe(h2, w_route_group[l], b_route_group[l], w_route_expert[l],
                         b_route_expert[l], w_gate[l], w_up[l], w_down[l])
    return rms_norm(x, norm_final)
```

```python
import functools

import jax
import jax.numpy as jnp
from jax import lax
from jax.experimental import pallas as pl
from jax.experimental.pallas import tpu as pltpu

F32 = jnp.float32
BF16 = jnp.bfloat16

D_MODEL = 1024
CONV_WIDTH = 1024
SSM_WIDTH = 512
SSM_GROUP = 16
SSM_GROUPS = 32
SSM_STATE = 64
N_EXPERT_GROUPS = 4
EXPERTS_PER_GROUP = 8
N_EXPERTS = 32
EXPERT_FF = 512
EPS = 1e-6

LANES = 128
MXU_DIM = 256
TOKEN_TILE = 256
MOE_TILE = 256
SCAN_PITCH = 36
STATE_ROWS = 2 * SSM_GROUPS * SSM_STATE // LANES
HALF_ROWS = STATE_ROWS // 2
ROUTE_GROUP_LANE = N_EXPERTS
VMEM_LIMIT = 56 * 1024 * 1024

_ARB = pltpu.CompilerParams(dimension_semantics=("arbitrary",), vmem_limit_bytes=VMEM_LIMIT)


def _const_spec(shape):
    nd = len(shape)
    return pl.BlockSpec(shape, lambda i, *_: (0,) * nd, pipeline_mode=pl.Buffered(1))


def _row_spec(tm, width):
    return pl.BlockSpec((tm, width), lambda i, *_: (i, 0))


def _rms(x, g):
    ms = jnp.mean(x * x, axis=-1, keepdims=True)
    return x * lax.rsqrt(ms + EPS) * g


def _front_kernel(x_ref, g_ref, win_ref, cw_ref, cb_ref, wco_ref, gbias_ref,
                  ma_ref, gb_ref, u_ref, carry_ref):
    tm = x_ref.shape[0]
    c0, c1, c2 = CONV_WIDTH, 2 * CONV_WIDTH, 3 * CONV_WIDTH
    c3 = c2 + SSM_WIDTH

    @pl.when(pl.program_id(0) == 0)
    def _():
        carry_ref[...] = jnp.zeros_like(carry_ref)

    h = _rms(x_ref[...], g_ref[...]).astype(BF16)

    def proj(lo, hi):
        return jnp.dot(h, win_ref[:, lo:hi], preferred_element_type=F32)

    v = proj(c1, c2) * proj(0, c0)
    row = lax.broadcasted_iota(jnp.int32, v.shape, 0)
    prev1 = carry_ref[7:8, :]
    prev2 = carry_ref[6:7, :]
    v1 = jnp.where(row == 0, prev1, pltpu.roll(v, 1, 0))
    v2 = jnp.where(row == 0, prev2, jnp.where(row == 1, prev1, pltpu.roll(v, 2, 0)))
    carry_ref[...] = v[tm - 8:, :]
    y = cw_ref[0:1, :] * v2 + cw_ref[1:2, :] * v1 + cw_ref[2:3, :] * v + cb_ref[...]
    z = (proj(c0, c1) * y).astype(BF16)
    ya = jnp.dot(z, wco_ref[...], preferred_element_type=F32)
    ga = jax.nn.sigmoid(proj(c3, c3 + D_MODEL) + gbias_ref[0:1, :])
    ma_ref[...] = (ga * ya).astype(BF16)
    gb = jax.nn.sigmoid(proj(c3 + D_MODEL, c3 + 2 * D_MODEL) + gbias_ref[1:2, :])
    gb_ref[...] = gb.astype(BF16)
    u_ref[...] = proj(c2, c3).astype(BF16)


def _mixer_front(x, norm_g, w_in, conv_w, conv_b, w_conv_out, gate_bias, tm):
    t = x.shape[0]
    in_cols = w_in.shape[1]
    return pl.pallas_call(
        _front_kernel,
        grid=(t // tm,),
        in_specs=[
            _row_spec(tm, D_MODEL),
            _const_spec((1, D_MODEL)),
            _const_spec((D_MODEL, in_cols)),
            _const_spec((3, CONV_WIDTH)),
            _const_spec((1, CONV_WIDTH)),
            _const_spec((CONV_WIDTH, D_MODEL)),
            _const_spec((2, D_MODEL)),
        ],
        out_specs=[_row_spec(tm, D_MODEL), _row_spec(tm, D_MODEL), _row_spec(tm, SSM_WIDTH)],
        out_shape=[
            jax.ShapeDtypeStruct((t, D_MODEL), BF16),
            jax.ShapeDtypeStruct((t, D_MODEL), BF16),
            jax.ShapeDtypeStruct((t, SSM_WIDTH), BF16),
        ],
        scratch_shapes=[pltpu.VMEM((8, CONV_WIDTH), F32)],
        compiler_params=_ARB,
        name="mixer_front",
    )(x, norm_g, w_in, conv_w, conv_b, w_conv_out, gate_bias)


def _ssm_kernel(u_ref, bre_ref, bim_ref, cw_ref, are_ref, aim_ref, d_ref,
                y_ref, r_ref, state_ref):
    tm = u_ref.shape[0]
    tiles_per_half = MXU_DIM * 4 // LANES

    @pl.when(pl.program_id(0) == 0)
    def _():
        state_ref[...] = jnp.zeros_like(state_ref)

    u = u_ref[...]
    for k in range(2):
        uk = u[:, k * MXU_DIM:(k + 1) * MXU_DIM]
        re = jnp.dot(uk, bre_ref[k], preferred_element_type=F32)
        im = jnp.dot(uk, bim_ref[k], preferred_element_type=F32)
        for jj in range(tiles_per_half):
            j = k * tiles_per_half + jj
            sl = slice(jj * LANES, (jj + 1) * LANES)
            r_ref[pl.ds(j, tm, stride=SCAN_PITCH), :] = re[:, sl]
            r_ref[pl.ds(HALF_ROWS + j, tm, stride=SCAN_PITCH), :] = im[:, sl]

    a_re = are_ref[...]
    a_im = aim_ref[...]

    def step(t, carry):
        s_re, s_im = carry
        base = t * SCAN_PITCH
        b_re = r_ref[pl.ds(base, HALF_ROWS), :]
        b_im = r_ref[pl.ds(base + HALF_ROWS, HALF_ROWS), :]
        n_re = a_re * s_re - a_im * s_im + b_re
        n_im = a_re * s_im + a_im * s_re + b_im
        r_ref[pl.ds(base, HALF_ROWS), :] = n_re
        r_ref[pl.ds(base + HALF_ROWS, HALF_ROWS), :] = n_im
        return n_re, n_im

    s_re, s_im = lax.fori_loop(
        0, tm, step, (state_ref[0:HALF_ROWS, :], state_ref[HALF_ROWS:, :]), unroll=8)
    state_ref[0:HALF_ROWS, :] = s_re
    state_ref[HALF_ROWS:, :] = s_im

    ys = []
    for k in range(2):
        cols = []
        for half in range(2):
            for jj in range(tiles_per_half):
                j = half * HALF_ROWS + k * tiles_per_half + jj
                cols.append(r_ref[pl.ds(j, tm, stride=SCAN_PITCH), :])
        s = jnp.concatenate(cols, axis=1).astype(BF16)
        ys.append(jnp.dot(s, cw_ref[k], preferred_element_type=F32))
    y = jnp.concatenate(ys, axis=1) + d_ref[...] * u.astype(F32)
    y_ref[...] = jax.nn.gelu(y).astype(BF16)


def _s5_scan(u, bre, bim, cw, a_re, a_im, d_skip, tm):
    t = u.shape[0]
    return pl.pallas_call(
        _ssm_kernel,
        grid=(t // tm,),
        in_specs=[
            _row_spec(tm, SSM_WIDTH),
            _const_spec(bre.shape),
            _const_spec(bim.shape),
            _const_spec(cw.shape),
            _const_spec(a_re.shape),
            _const_spec(a_im.shape),
            _const_spec((1, SSM_WIDTH)),
        ],
        out_specs=_row_spec(tm, SSM_WIDTH),
        out_shape=jax.ShapeDtypeStruct((t, SSM_WIDTH), BF16),
        scratch_shapes=[
            pltpu.VMEM((tm * SCAN_PITCH, LANES), F32),
            pltpu.VMEM((STATE_ROWS, LANES), F32),
        ],
        compiler_params=_ARB,
        name="s5_scan",
    )(u, bre, bim, cw, a_re, a_im, d_skip)


def _s5_weights(a_re, a_im, log_dt, b_re, b_im, c_re, c_im):
    g, n, h = SSM_GROUPS, SSM_STATE, SSM_GROUP
    gpt = MXU_DIM // h
    lam = lax.complex(a_re.astype(F32), a_im.astype(F32))
    dt = jnp.exp(log_dt.astype(F32))[:, None]
    a_bar = jnp.exp(lam * dt)
    b_bar = ((a_bar - 1.0) / lam)[..., None] * lax.complex(b_re.astype(F32), b_im.astype(F32))
    eye = jnp.eye(gpt, dtype=F32)

    def blk_in(w):
        w = w.reshape(g // gpt, gpt, n, h).transpose(0, 1, 3, 2)
        return jnp.einsum("kghn,gf->kghfn", w, eye).reshape(g // gpt, gpt * h, gpt * n)

    def blk_out(c):
        c = c.astype(F32).reshape(g // gpt, gpt, h, n).transpose(0, 1, 3, 2)
        return jnp.einsum("kgnh,gf->kgnfh", c, eye).reshape(g // gpt, gpt * n, gpt * h)

    bre = blk_in(jnp.real(b_bar)).astype(BF16)
    bim = blk_in(jnp.imag(b_bar)).astype(BF16)
    cw = jnp.concatenate([blk_out(c_re), -blk_out(c_im)], axis=1).astype(BF16)
    tile = (HALF_ROWS, LANES)
    return bre, bim, cw, jnp.real(a_bar).reshape(tile), jnp.imag(a_bar).reshape(tile)


def _back_kernel(yg_ref, ma_ref, gb_ref, x_ref, wa_ref, wb_ref, wo_ref, g2_ref, wr_ref, br_ref,
                 x1_ref, h2_ref, route_ref):
    yg = yg_ref[...]
    yb = (jnp.dot(yg, wa_ref[...], preferred_element_type=F32)
          * jax.nn.sigmoid(jnp.dot(yg, wb_ref[...], preferred_element_type=F32)))
    mixed = (ma_ref[...].astype(F32) + gb_ref[...].astype(F32) * yb).astype(BF16)
    x1 = x_ref[...] + jnp.dot(mixed, wo_ref[...], preferred_element_type=F32)
    x1_ref[...] = x1
    h2 = _rms(x1, g2_ref[...])
    h2_ref[...] = h2.astype(BF16)

    logits = jnp.dot(h2, wr_ref[...], preferred_element_type=F32,
                     precision=lax.Precision.HIGHEST) + br_ref[...]
    lane = lax.broadcasted_iota(jnp.int32, logits.shape, 1)
    lane_f = lane.astype(F32)
    neg = jnp.float32(-jnp.inf)
    big = jnp.float32(LANES)
    is_grp = (lane >= ROUTE_GROUP_LANE) & (lane < ROUTE_GROUP_LANE + N_EXPERT_GROUPS)
    gl = jnp.where(is_grp, logits, neg)
    gmax = jnp.max(gl, axis=1, keepdims=True)
    gidx = jnp.min(jnp.where(gl == gmax, lane_f - ROUTE_GROUP_LANE, big), axis=1, keepdims=True)
    pg_top = 1.0 / jnp.sum(jnp.exp(gl - gmax), axis=1, keepdims=True)
    lane_grp = (lane // EXPERTS_PER_GROUP).astype(F32)
    el = jnp.where((lane < N_EXPERTS) & (lane_grp == gidx), logits, neg)
    l1 = jnp.max(el, axis=1, keepdims=True)
    i1 = jnp.min(jnp.where(el == l1, lane_f, big), axis=1, keepdims=True)
    el2 = jnp.where(lane_f == i1, neg, el)
    l2 = jnp.max(el2, axis=1, keepdims=True)
    i2 = jnp.min(jnp.where(el2 == l2, lane_f, big), axis=1, keepdims=True)
    r = jnp.exp(l2 - l1)
    w1 = pg_top / (1.0 + r)
    w2 = pg_top * r / (1.0 + r)
    route_ref[...] = jnp.where(lane == 0, i1, jnp.where(lane == 1, i2,
                               jnp.where(lane == 2, w1, jnp.where(lane == 3, w2, 0.0))))


def _mixer_back(yg, ma, gb, x, w_glu_a, w_glu_b, w_o, norm_g, w_route, b_route, tm):
    t = x.shape[0]
    return pl.pallas_call(
        _back_kernel,
        grid=(t // tm,),
        in_specs=[
            _row_spec(tm, SSM_WIDTH),
            _row_spec(tm, D_MODEL),
            _row_spec(tm, D_MODEL),
            _row_spec(tm, D_MODEL),
            _const_spec((SSM_WIDTH, D_MODEL)),
            _const_spec((SSM_WIDTH, D_MODEL)),
            _const_spec((D_MODEL, D_MODEL)),
            _const_spec((1, D_MODEL)),
            _const_spec((D_MODEL, LANES)),
            _const_spec((1, LANES)),
        ],
        out_specs=[_row_spec(tm, D_MODEL), _row_spec(tm, D_MODEL), _row_spec(tm, LANES)],
        out_shape=[
            jax.ShapeDtypeStruct((t, D_MODEL), F32),
            jax.ShapeDtypeStruct((t, D_MODEL), BF16),
            jax.ShapeDtypeStruct((t, LANES), F32),
        ],
        compiler_params=_ARB,
        name="mixer_back",
    )(yg, ma, gb, x, w_glu_a, w_glu_b, w_o, norm_g, w_route, b_route)


def _moe_kernel(tile_expert_ref, n_used_ref, xs_ref, wg_ref, wu_ref, wd_ref, out_ref):
    del tile_expert_ref
    i = pl.program_id(0)

    @pl.when(i < n_used_ref[0])
    def _():
        xs = xs_ref[...]
        gate = jnp.dot(xs, wg_ref[0].astype(BF16), preferred_element_type=F32)
        up = jnp.dot(xs, wu_ref[0].astype(BF16), preferred_element_type=F32)
        act = (gate * jax.nn.sigmoid(gate) * up).astype(BF16)
        out_ref[...] = jnp.dot(act, wd_ref[0].astype(BF16), preferred_element_type=F32)

    @pl.when(i >= n_used_ref[0])
    def _():
        out_ref[...] = jnp.zeros_like(out_ref)


def _moe_experts(tile_expert, n_used, xs, w_gate, w_up, w_down, tm):
    nslot = xs.shape[0]
    grid_spec = pltpu.PrefetchScalarGridSpec(
        num_scalar_prefetch=2,
        grid=(nslot // tm,),
        in_specs=[
            _row_spec(tm, D_MODEL),
            pl.BlockSpec((1, D_MODEL, EXPERT_FF), lambda i, te, nu: (te[i], 0, 0)),
            pl.BlockSpec((1, D_MODEL, EXPERT_FF), lambda i, te, nu: (te[i], 0, 0)),
            pl.BlockSpec((1, EXPERT_FF, D_MODEL), lambda i, te, nu: (te[i], 0, 0)),
        ],
        out_specs=_row_spec(tm, D_MODEL),
    )
    return pl.pallas_call(
        _moe_kernel,
        grid_spec=grid_spec,
        out_shape=jax.ShapeDtypeStruct((nslot, D_MODEL), F32),
        compiler_params=_ARB,
        name="moe_experts",
    )(tile_expert, n_used, xs, w_gate, w_up, w_down)


def _dispatch(e_ids, tm):
    t = e_ids.shape[0]
    flat = e_ids.reshape(-1)
    onehot = (flat[:, None] == jnp.arange(N_EXPERTS, dtype=jnp.int32)[None, :]).astype(jnp.int32)
    csum = jnp.cumsum(onehot, axis=0)
    rank = jnp.sum(onehot * csum, axis=1) - 1
    counts = csum[-1]
    padded = ((counts + tm - 1) // tm) * tm
    ends = jnp.cumsum(padded)
    starts = ends - padded
    slot = starts[flat] + rank
    nslot = 2 * t + N_EXPERTS * tm
    tok_of_slot = jnp.zeros((nslot,), jnp.int32).at[slot].set(
        jnp.arange(2 * t, dtype=jnp.int32) // 2)
    n_used = (ends[-1] // tm).astype(jnp.int32)
    tile_start = jnp.arange(nslot // tm, dtype=jnp.int32) * tm
    tile_expert = jnp.searchsorted(ends, tile_start, side="right").astype(jnp.int32)
    last = jnp.minimum(tile_expert[jnp.maximum(n_used - 1, 0)], N_EXPERTS - 1)
    tile_expert = jnp.where(tile_start < ends[-1], tile_expert, last)
    return slot.reshape(t, 2), tok_of_slot, tile_expert, n_used.reshape(1)


def _final_kernel(x1_ref, y0_ref, y1_ref, route_ref, g_ref, out_ref):
    route = route_ref[...]
    x2 = x1_ref[...] + route[:, 2:3] * y0_ref[...] + route[:, 3:4] * y1_ref[...]
    out_ref[...] = _rms(x2, g_ref[...])


def _final(x1, y0, y1, route, norm_g, tm):
    t = x1.shape[0]
    return pl.pallas_call(
        _final_kernel,
        grid=(t // tm,),
        in_specs=[
            _row_spec(tm, D_MODEL),
            _row_spec(tm, D_MODEL),
            _row_spec(tm, D_MODEL),
            _row_spec(tm, LANES),
            _const_spec((1, D_MODEL)),
        ],
        out_specs=_row_spec(tm, D_MODEL),
        out_shape=jax.ShapeDtypeStruct((t, D_MODEL), F32),
        compiler_params=_ARB,
        name="final_norm",
    )(x1, y0, y1, route, norm_g)


def kernel(x, norm_mix, w_in, conv_w, conv_b, w_conv_out, ssm_a_re, ssm_a_im, ssm_log_dt,
           ssm_b_re, ssm_b_im, ssm_c_re, ssm_c_im, ssm_d, w_glu_a, w_glu_b, gate_bias, w_o,
           norm_ffn, w_route_group, b_route_group, w_route_expert, b_route_expert,
           w_gate, w_up, w_down, norm_final):
    bsz, length, d = x.shape
    assert d == D_MODEL and norm_mix.shape[0] == 1
    t = bsz * length
    tm = TOKEN_TILE
    assert bsz == 1 and t % tm == 0
    xt = x.reshape(t, d)
    row = lambda a: a.reshape(1, -1).astype(F32)

    ma, gb, u = _mixer_front(
        xt, row(norm_mix[0]), w_in[0].astype(BF16), conv_w[0].reshape(3, CONV_WIDTH),
        row(conv_b[0]), w_conv_out[0].astype(BF16), gate_bias[0], tm)

    bre, bim, cw, a_re, a_im = _s5_weights(
        ssm_a_re[0], ssm_a_im[0], ssm_log_dt[0], ssm_b_re[0], ssm_b_im[0],
        ssm_c_re[0], ssm_c_im[0])
    yg = _s5_scan(u, bre, bim, cw, a_re, a_im, row(ssm_d[0]), tm)

    w_route = jnp.zeros((d, LANES), F32)
    w_route = w_route.at[:, :N_EXPERTS].set(w_route_expert[0])
    w_route = w_route.at[:, ROUTE_GROUP_LANE:ROUTE_GROUP_LANE + N_EXPERT_GROUPS].set(w_route_group[0])
    b_route = jnp.zeros((1, LANES), F32)
    b_route = b_route.at[0, :N_EXPERTS].set(b_route_expert[0])
    b_route = b_route.at[0, ROUTE_GROUP_LANE:ROUTE_GROUP_LANE + N_EXPERT_GROUPS].set(b_route_group[0])
    x1, h2, route = _mixer_back(
        yg, ma, gb, xt, w_glu_a[0].astype(BF16), w_glu_b[0].astype(BF16), w_o[0].astype(BF16),
        row(norm_ffn[0]), w_route, b_route, tm)

    e_ids = route[:, 0:2].astype(jnp.int32)
    slot, tok_of_slot, tile_expert, n_used = _dispatch(e_ids, MOE_TILE)
    xs = jnp.take(h2, tok_of_slot, axis=0)
    ys = _moe_experts(tile_expert, n_used, xs, w_gate[0], w_up[0], w_down[0], MOE_TILE)
    y0 = jnp.take(ys, slot[:, 0], axis=0)
    y1 = jnp.take(ys, slot[:, 1], axis=0)
    out = _final(x1, y0, y1, route, row(norm_final), tm)
    return out.reshape(bsz, length, d)
```

```python
import functools

import jax
import jax.numpy as jnp
from jax import lax
from jax.experimental import pallas as pl
from jax.experimental.pallas import tpu as pltpu
from jax.experimental.pallas import tpu_sc as plsc

F32 = jnp.float32
BF16 = jnp.bfloat16
I32 = jnp.int32

D_MODEL = 1024
CONV_WIDTH = 1024
SSM_WIDTH = 512
SSM_GROUP = 16
SSM_GROUPS = 32
SSM_STATE = 64
N_EXPERT_GROUPS = 4
EXPERTS_PER_GROUP = 8
N_EXPERTS = 32
EXPERT_FF = 512
EPS = 1e-6

LANES = 128
MXU_DIM = 256
TOKEN_TILE = 256
MOE_TILE = 256
SCAN_PITCH = 36
STATE_ROWS = 2 * SSM_GROUPS * SSM_STATE // LANES
HALF_ROWS = STATE_ROWS // 2
ROUTE_GROUP_LANE = N_EXPERTS
PACKED = D_MODEL // 2
HI_MASK = -65536
SC_CORES = 2
SC_SUBCORES = 16
SC_CHUNK = 128
VMEM_LIMIT = 56 * 1024 * 1024

_ARB = pltpu.CompilerParams(dimension_semantics=("arbitrary",), vmem_limit_bytes=VMEM_LIMIT)


def _const_spec(shape):
    nd = len(shape)
    return pl.BlockSpec(shape, lambda i, *_: (0,) * nd, pipeline_mode=pl.Buffered(1))


def _row_spec(tm, width):
    return pl.BlockSpec((tm, width), lambda i, *_: (i, 0))


def _rms(x, g):
    ms = jnp.mean(x * x, axis=-1, keepdims=True)
    return x * lax.rsqrt(ms + EPS) * g


def _pack_rows(v):
    bits = lax.bitcast_convert_type(v.astype(BF16).astype(F32), I32)
    lo = lax.shift_right_logical(bits[:, :PACKED], 16)
    hi = bits[:, PACKED:] & HI_MASK
    return lo | hi


def _unpack_rows(w):
    lo = lax.bitcast_convert_type(lax.shift_left(w, 16), F32)
    hi = lax.bitcast_convert_type(w & HI_MASK, F32)
    return lo, hi


def _front_kernel(x_ref, g_ref, win_ref, cw_ref, cb_ref, wco_ref, gbias_ref,
                  ma_ref, gb_ref, u_ref, carry_ref):
    tm = x_ref.shape[0]
    c0, c1, c2 = CONV_WIDTH, 2 * CONV_WIDTH, 3 * CONV_WIDTH
    c3 = c2 + SSM_WIDTH

    @pl.when(pl.program_id(0) == 0)
    def _():
        carry_ref[...] = jnp.zeros_like(carry_ref)

    h = _rms(x_ref[...], g_ref[...]).astype(BF16)

    def proj(lo, hi):
        return jnp.dot(h, win_ref[:, lo:hi], preferred_element_type=F32)

    v = proj(c1, c2) * proj(0, c0)
    row = lax.broadcasted_iota(jnp.int32, v.shape, 0)
    prev1 = carry_ref[7:8, :]
    prev2 = carry_ref[6:7, :]
    v1 = jnp.where(row == 0, prev1, pltpu.roll(v, 1, 0))
    v2 = jnp.where(row == 0, prev2, jnp.where(row == 1, prev1, pltpu.roll(v, 2, 0)))
    carry_ref[...] = v[tm - 8:, :]
    y = cw_ref[0:1, :] * v2 + cw_ref[1:2, :] * v1 + cw_ref[2:3, :] * v + cb_ref[...]
    z = (proj(c0, c1) * y).astype(BF16)
    ya = jnp.dot(z, wco_ref[...], preferred_element_type=F32)
    ga = jax.nn.sigmoid(proj(c3, c3 + D_MODEL) + gbias_ref[0:1, :])
    ma_ref[...] = (ga * ya).astype(BF16)
    gb = jax.nn.sigmoid(proj(c3 + D_MODEL, c3 + 2 * D_MODEL) + gbias_ref[1:2, :])
    gb_ref[...] = gb.astype(BF16)
    u_ref[...] = proj(c2, c3).astype(BF16)


def _mixer_front(x, norm_g, w_in, conv_w, conv_b, w_conv_out, gate_bias, tm):
    t = x.shape[0]
    in_cols = w_in.shape[1]
    return pl.pallas_call(
        _front_kernel,
        grid=(t // tm,),
        in_specs=[
            _row_spec(tm, D_MODEL),
            _const_spec((1, D_MODEL)),
            _const_spec((D_MODEL, in_cols)),
            _const_spec((3, CONV_WIDTH)),
            _const_spec((1, CONV_WIDTH)),
            _const_spec((CONV_WIDTH, D_MODEL)),
            _const_spec((2, D_MODEL)),
        ],
        out_specs=[_row_spec(tm, D_MODEL), _row_spec(tm, D_MODEL), _row_spec(tm, SSM_WIDTH)],
        out_shape=[
            jax.ShapeDtypeStruct((t, D_MODEL), BF16),
            jax.ShapeDtypeStruct((t, D_MODEL), BF16),
            jax.ShapeDtypeStruct((t, SSM_WIDTH), BF16),
        ],
        scratch_shapes=[pltpu.VMEM((8, CONV_WIDTH), F32)],
        compiler_params=_ARB,
        name="mixer_front",
    )(x, norm_g, w_in, conv_w, conv_b, w_conv_out, gate_bias)


def _ssm_kernel(u_ref, bre_ref, bim_ref, cw_ref, are_ref, aim_ref, d_ref,
                y_ref, r_ref, state_ref):
    tm = u_ref.shape[0]
    tiles_per_half = MXU_DIM * 4 // LANES

    @pl.when(pl.program_id(0) == 0)
    def _():
        state_ref[...] = jnp.zeros_like(state_ref)

    u = u_ref[...]
    for k in range(2):
        uk = u[:, k * MXU_DIM:(k + 1) * MXU_DIM]
        re = jnp.dot(uk, bre_ref[k], preferred_element_type=F32)
        im = jnp.dot(uk, bim_ref[k], preferred_element_type=F32)
        for jj in range(tiles_per_half):
            j = k * tiles_per_half + jj
            sl = slice(jj * LANES, (jj + 1) * LANES)
            r_ref[pl.ds(j, tm, stride=SCAN_PITCH), :] = re[:, sl]
            r_ref[pl.ds(HALF_ROWS + j, tm, stride=SCAN_PITCH), :] = im[:, sl]

    a_re = are_ref[...]
    a_im = aim_ref[...]

    def step(t, carry):
        s_re, s_im = carry
        base = t * SCAN_PITCH
        b_re = r_ref[pl.ds(base, HALF_ROWS), :]
        b_im = r_ref[pl.ds(base + HALF_ROWS, HALF_ROWS), :]
        n_re = a_re * s_re - a_im * s_im + b_re
        n_im = a_re * s_im + a_im * s_re + b_im
        r_ref[pl.ds(base, HALF_ROWS), :] = n_re
        r_ref[pl.ds(base + HALF_ROWS, HALF_ROWS), :] = n_im
        return n_re, n_im

    s_re, s_im = lax.fori_loop(
        0, tm, step, (state_ref[0:HALF_ROWS, :], state_ref[HALF_ROWS:, :]), unroll=8)
    state_ref[0:HALF_ROWS, :] = s_re
    state_ref[HALF_ROWS:, :] = s_im

    ys = []
    for k in range(2):
        cols = []
        for half in range(2):
            for jj in range(tiles_per_half):
                j = half * HALF_ROWS + k * tiles_per_half + jj
                cols.append(r_ref[pl.ds(j, tm, stride=SCAN_PITCH), :])
        s = jnp.concatenate(cols, axis=1).astype(BF16)
        ys.append(jnp.dot(s, cw_ref[k], preferred_element_type=F32))
    y = jnp.concatenate(ys, axis=1) + d_ref[...] * u.astype(F32)
    y_ref[...] = jax.nn.gelu(y).astype(BF16)


def _s5_scan(u, bre, bim, cw, a_re, a_im, d_skip, tm):
    t = u.shape[0]
    return pl.pallas_call(
        _ssm_kernel,
        grid=(t // tm,),
        in_specs=[
            _row_spec(tm, SSM_WIDTH),
            _const_spec(bre.shape),
            _const_spec(bim.shape),
            _const_spec(cw.shape),
            _const_spec(a_re.shape),
            _const_spec(a_im.shape),
            _const_spec((1, SSM_WIDTH)),
        ],
        out_specs=_row_spec(tm, SSM_WIDTH),
        out_shape=jax.ShapeDtypeStruct((t, SSM_WIDTH), BF16),
        scratch_shapes=[
            pltpu.VMEM((tm * SCAN_PITCH, LANES), F32),
            pltpu.VMEM((STATE_ROWS, LANES), F32),
        ],
        compiler_params=_ARB,
        name="s5_scan",
    )(u, bre, bim, cw, a_re, a_im, d_skip)


def _s5_weights(a_re, a_im, log_dt, b_re, b_im, c_re, c_im):
    g, n, h = SSM_GROUPS, SSM_STATE, SSM_GROUP
    gpt = MXU_DIM // h
    lam = lax.complex(a_re.astype(F32), a_im.astype(F32))
    dt = jnp.exp(log_dt.astype(F32))[:, None]
    a_bar = jnp.exp(lam * dt)
    b_bar = ((a_bar - 1.0) / lam)[..., None] * lax.complex(b_re.astype(F32), b_im.astype(F32))
    eye = jnp.eye(gpt, dtype=F32)

    def blk_in(w):
        w = w.reshape(g // gpt, gpt, n, h).transpose(0, 1, 3, 2)
        return jnp.einsum("kghn,gf->kghfn", w, eye).reshape(g // gpt, gpt * h, gpt * n)

    def blk_out(c):
        c = c.astype(F32).reshape(g // gpt, gpt, h, n).transpose(0, 1, 3, 2)
        return jnp.einsum("kgnh,gf->kgnfh", c, eye).reshape(g // gpt, gpt * n, gpt * h)

    bre = blk_in(jnp.real(b_bar)).astype(BF16)
    bim = blk_in(jnp.imag(b_bar)).astype(BF16)
    cw = jnp.concatenate([blk_out(c_re), -blk_out(c_im)], axis=1).astype(BF16)
    tile = (HALF_ROWS, LANES)
    return bre, bim, cw, jnp.real(a_bar).reshape(tile), jnp.imag(a_bar).reshape(tile)


def _back_kernel(yg_ref, ma_ref, gb_ref, x_ref, wa_ref, wb_ref, wo_ref, g2_ref, wr_ref, br_ref,
                 x1_ref, h2p_ref, route_ref, counts_ref):
    tm = x_ref.shape[0]

    @pl.when(pl.program_id(0) == 0)
    def _():
        counts_ref[...] = jnp.zeros_like(counts_ref)

    yg = yg_ref[...]
    yb = (jnp.dot(yg, wa_ref[...], preferred_element_type=F32)
          * jax.nn.sigmoid(jnp.dot(yg, wb_ref[...], preferred_element_type=F32)))
    mixed = (ma_ref[...].astype(F32) + gb_ref[...].astype(F32) * yb).astype(BF16)
    x1 = x_ref[...] + jnp.dot(mixed, wo_ref[...], preferred_element_type=F32)
    x1_ref[...] = x1
    h2 = _rms(x1, g2_ref[...])
    h2p_ref[...] = _pack_rows(h2)

    logits = jnp.dot(h2, wr_ref[...], preferred_element_type=F32,
                     precision=lax.Precision.HIGHEST) + br_ref[...]
    lane = lax.broadcasted_iota(jnp.int32, logits.shape, 1)
    lane_f = lane.astype(F32)
    neg = jnp.float32(-jnp.inf)
    big = jnp.float32(LANES)
    is_grp = (lane >= ROUTE_GROUP_LANE) & (lane < ROUTE_GROUP_LANE + N_EXPERT_GROUPS)
    gl = jnp.where(is_grp, logits, neg)
    gmax = jnp.max(gl, axis=1, keepdims=True)
    gidx = jnp.min(jnp.where(gl == gmax, lane_f - ROUTE_GROUP_LANE, big), axis=1, keepdims=True)
    pg_top = 1.0 / jnp.sum(jnp.exp(gl - gmax), axis=1, keepdims=True)
    lane_grp = (lane // EXPERTS_PER_GROUP).astype(F32)
    el = jnp.where((lane < N_EXPERTS) & (lane_grp == gidx), logits, neg)
    l1 = jnp.max(el, axis=1, keepdims=True)
    i1 = jnp.min(jnp.where(el == l1, lane_f, big), axis=1, keepdims=True)
    el2 = jnp.where(lane_f == i1, neg, el)
    l2 = jnp.max(el2, axis=1, keepdims=True)
    i2 = jnp.min(jnp.where(el2 == l2, lane_f, big), axis=1, keepdims=True)
    r = jnp.exp(l2 - l1)
    w1 = pg_top / (1.0 + r)
    w2 = pg_top * r / (1.0 + r)

    oh1 = (lane_f == i1).astype(F32)
    oh2 = (lane_f == i2).astype(F32)
    picked = oh1 + oh2
    rr = lax.broadcasted_iota(jnp.int32, (tm, tm), 0)
    cc = lax.broadcasted_iota(jnp.int32, (tm, tm), 1)
    before = (cc < rr).astype(BF16)
    prior = jnp.dot(before, picked.astype(BF16), preferred_element_type=F32) + counts_ref[...]
    rank1 = jnp.sum(oh1 * prior, axis=1, keepdims=True)
    rank2 = jnp.sum(oh2 * prior, axis=1, keepdims=True)
    counts_ref[...] += jnp.sum(picked, axis=0, keepdims=True)

    route_ref[...] = jnp.where(
        lane == 0, i1, jnp.where(lane == 1, i2, jnp.where(lane == 2, w1, jnp.where(
            lane == 3, w2, jnp.where(lane == 4, rank1, jnp.where(lane == 5, rank2, 0.0))))))


def _mixer_back(yg, ma, gb, x, w_glu_a, w_glu_b, w_o, norm_g, w_route, b_route, tm):
    t = x.shape[0]
    return pl.pallas_call(
        _back_kernel,
        grid=(t // tm,),
        in_specs=[
            _row_spec(tm, SSM_WIDTH),
            _row_spec(tm, D_MODEL),
            _row_spec(tm, D_MODEL),
            _row_spec(tm, D_MODEL),
            _const_spec((SSM_WIDTH, D_MODEL)),
            _const_spec((SSM_WIDTH, D_MODEL)),
            _const_spec((D_MODEL, D_MODEL)),
            _const_spec((1, D_MODEL)),
            _const_spec((D_MODEL, LANES)),
            _const_spec((1, LANES)),
        ],
        out_specs=[_row_spec(tm, D_MODEL), _row_spec(tm, PACKED), _row_spec(tm, LANES),
                   pl.BlockSpec((1, LANES), lambda i: (0, 0))],
        out_shape=[
            jax.ShapeDtypeStruct((t, D_MODEL), F32),
            jax.ShapeDtypeStruct((t, PACKED), I32),
            jax.ShapeDtypeStruct((t, LANES), F32),
            jax.ShapeDtypeStruct((1, LANES), F32),
        ],
        compiler_params=_ARB,
        name="mixer_back",
    )(yg, ma, gb, x, w_glu_a, w_glu_b, w_o, norm_g, w_route, b_route)


_SC_MESH = dict(core_axis_name="c", subcore_axis_name="s")


def _sc_worker():
    return lax.axis_index("s") * SC_CORES + lax.axis_index("c")


def _sc_dispatch(rows, slot0, slot1, nslot):
    t, width = rows.shape
    per_w = t // (SC_CORES * SC_SUBCORES)
    assert per_w % SC_CHUNK == 0

    @functools.partial(
        pl.kernel,
        out_type=jax.ShapeDtypeStruct((nslot, width), rows.dtype),
        mesh=plsc.VectorSubcoreMesh(**_SC_MESH),
        scratch_types=[
            pltpu.VMEM((1, SC_CHUNK), I32),
            pltpu.VMEM((1, SC_CHUNK), I32),
            pltpu.VMEM((SC_CHUNK, width), rows.dtype),
        ],
        name="moe_dispatch",
    )
    def k(rows_hbm, s0_hbm, s1_hbm, out_hbm, i0_v, i1_v, rows_v):
        base = _sc_worker() * per_w

        @pl.loop(0, per_w // SC_CHUNK)
        def _(c):
            off = base + c * SC_CHUNK
            pltpu.sync_copy(s0_hbm.at[:, pl.ds(off, SC_CHUNK)], i0_v)
            pltpu.sync_copy(s1_hbm.at[:, pl.ds(off, SC_CHUNK)], i1_v)
            pltpu.sync_copy(rows_hbm.at[pl.ds(off, SC_CHUNK)], rows_v)
            pltpu.sync_copy(rows_v, out_hbm.at[i0_v.at[0]])
            pltpu.sync_copy(rows_v, out_hbm.at[i1_v.at[0]])

    return k(rows, slot0.reshape(1, t), slot1.reshape(1, t))


def _sc_gather(table, idx):
    n = idx.shape[0]
    width = table.shape[1]
    per_w = n // (SC_CORES * SC_SUBCORES)
    assert per_w % SC_CHUNK == 0

    @functools.partial(
        pl.kernel,
        out_type=jax.ShapeDtypeStruct((n, width), table.dtype),
        mesh=plsc.VectorSubcoreMesh(**_SC_MESH),
        scratch_types=[
            pltpu.VMEM((1, SC_CHUNK), I32),
            pltpu.VMEM((SC_CHUNK, width), table.dtype),
        ],
        name="moe_combine_gather",
    )
    def k(table_hbm, idx_hbm, out_hbm, idx_v, rows_v):
        base = _sc_worker() * per_w

        @pl.loop(0, per_w // SC_CHUNK)
        def _(c):
            off = base + c * SC_CHUNK
            pltpu.sync_copy(idx_hbm.at[:, pl.ds(off, SC_CHUNK)], idx_v)
            pltpu.sync_copy(table_hbm.at[idx_v.at[0]], rows_v)
            pltpu.sync_copy(rows_v, out_hbm.at[pl.ds(off, SC_CHUNK)])

    return k(table, idx.reshape(1, n))


def _moe_kernel(tile_expert_ref, tile_rows_ref, xs_ref, wg_ref, wu_ref, wd_ref, out_ref,
                wg_s, wu_s, wd_s):
    i = pl.program_id(0)
    tm = xs_ref.shape[0]
    valid = tile_rows_ref[i]
    prev = tile_expert_ref[jnp.maximum(i - 1, 0)]

    @pl.when((i == 0) | (tile_expert_ref[i] != prev))
    def _():
        wg_s[...] = wg_ref[0].astype(BF16)
        wu_s[...] = wu_ref[0].astype(BF16)
        wd_s[...] = wd_ref[0].astype(BF16)

    @pl.when(valid > 0)
    def _():
        row = lax.broadcasted_iota(jnp.int32, (tm, PACKED), 0)
        words = jnp.where(row < valid, xs_ref[...], 0)
        lo, hi = _unpack_rows(words)
        lo = lo.astype(BF16)
        hi = hi.astype(BF16)
        gate = (jnp.dot(lo, wg_s[:PACKED, :], preferred_element_type=F32)
                + jnp.dot(hi, wg_s[PACKED:, :], preferred_element_type=F32))
        up = (jnp.dot(lo, wu_s[:PACKED, :], preferred_element_type=F32)
              + jnp.dot(hi, wu_s[PACKED:, :], preferred_element_type=F32))
        act = (gate * jax.nn.sigmoid(gate) * up).astype(BF16)
        out_ref[...] = _pack_rows(jnp.dot(act, wd_s[...], preferred_element_type=F32))

    @pl.when(valid <= 0)
    def _():
        out_ref[...] = jnp.zeros_like(out_ref)


def _moe_experts(tile_expert, tile_rows, xs, w_gate, w_up, w_down, tm):
    nslot = xs.shape[0]
    grid_spec = pltpu.PrefetchScalarGridSpec(
        num_scalar_prefetch=2,
        grid=(nslot // tm,),
        in_specs=[
            _row_spec(tm, PACKED),
            pl.BlockSpec((1, D_MODEL, EXPERT_FF), lambda i, te, tr: (te[i], 0, 0)),
            pl.BlockSpec((1, D_MODEL, EXPERT_FF), lambda i, te, tr: (te[i], 0, 0)),
            pl.BlockSpec((1, EXPERT_FF, D_MODEL), lambda i, te, tr: (te[i], 0, 0)),
        ],
        out_specs=_row_spec(tm, PACKED),
        scratch_shapes=[
            pltpu.VMEM((D_MODEL, EXPERT_FF), BF16),
            pltpu.VMEM((D_MODEL, EXPERT_FF), BF16),
            pltpu.VMEM((EXPERT_FF, D_MODEL), BF16),
        ],
    )
    return pl.pallas_call(
        _moe_kernel,
        grid_spec=grid_spec,
        out_shape=jax.ShapeDtypeStruct((nslot, PACKED), I32),
        compiler_params=_ARB,
        name="moe_experts",
    )(tile_expert, tile_rows, xs, w_gate, w_up, w_down)


def _dispatch_plan(route, counts, tm):
    t = route.shape[0]
    e_ids = route[:, 0:2].astype(I32)
    ranks = route[:, 4:6].astype(I32)
    counts = counts[0, :N_EXPERTS].astype(I32)
    padded = ((counts + tm - 1) // tm) * tm
    ends = jnp.cumsum(padded)
    starts = ends - padded
    slot = starts[e_ids] + ranks
    n_tiles = (2 * t + N_EXPERTS * tm) // tm
    tile_start = jnp.arange(n_tiles, dtype=I32) * tm
    tile_expert = jnp.searchsorted(ends, tile_start, side="right").astype(I32)
    last_used = jnp.maximum(ends[-1] // tm - 1, 0)
    tile_expert = jnp.where(tile_start < ends[-1], tile_expert, tile_expert[last_used])
    tile_expert = jnp.minimum(tile_expert, N_EXPERTS - 1)
    live = starts[tile_expert] + counts[tile_expert] - tile_start
    tile_rows = jnp.where(tile_start < ends[-1], jnp.clip(live, 0, tm), 0).astype(I32)
    return slot[:, 0], slot[:, 1], tile_expert, tile_rows


def _final_kernel(x1_ref, y0_ref, y1_ref, route_ref, g_ref, out_ref):
    route = route_ref[...]
    w1 = route[:, 2:3]
    w2 = route[:, 3:4]
    a_lo, a_hi = _unpack_rows(y0_ref[...])
    b_lo, b_hi = _unpack_rows(y1_ref[...])
    moe = jnp.concatenate([w1 * a_lo + w2 * b_lo, w1 * a_hi + w2 * b_hi], axis=1)
    out_ref[...] = _rms(x1_ref[...] + moe, g_ref[...])


def _final(x1, ycat, route, norm_g, tm):
    t = x1.shape[0]
    nblk = t // tm
    return pl.pallas_call(
        _final_kernel,
        grid=(nblk,),
        in_specs=[
            _row_spec(tm, D_MODEL),
            pl.BlockSpec((tm, PACKED), lambda i: (i, 0)),
            pl.BlockSpec((tm, PACKED), lambda i: (i + nblk, 0)),
            _row_spec(tm, LANES),
            _const_spec((1, D_MODEL)),
        ],
        out_specs=_row_spec(tm, D_MODEL),
        out_shape=jax.ShapeDtypeStruct((t, D_MODEL), F32),
        compiler_params=_ARB,
        name="final_norm",
    )(x1, ycat, ycat, route, norm_g)


def kernel(x, norm_mix, w_in, conv_w, conv_b, w_conv_out, ssm_a_re, ssm_a_im, ssm_log_dt,
           ssm_b_re, ssm_b_im, ssm_c_re, ssm_c_im, ssm_d, w_glu_a, w_glu_b, gate_bias, w_o,
           norm_ffn, w_route_group, b_route_group, w_route_expert, b_route_expert,
           w_gate, w_up, w_down, norm_final):
    bsz, length, d = x.shape
    assert d == D_MODEL and norm_mix.shape[0] == 1
    t = bsz * length
    tm = TOKEN_TILE
    assert bsz == 1 and t % tm == 0
    xt = x.reshape(t, d)
    row = lambda a: a.reshape(1, -1).astype(F32)

    ma, gb, u = _mixer_front(
        xt, row(norm_mix[0]), w_in[0].astype(BF16), conv_w[0].reshape(3, CONV_WIDTH),
        row(conv_b[0]), w_conv_out[0].astype(BF16), gate_bias[0], tm)

    bre, bim, cw, a_re, a_im = _s5_weights(
        ssm_a_re[0], ssm_a_im[0], ssm_log_dt[0], ssm_b_re[0], ssm_b_im[0],
        ssm_c_re[0], ssm_c_im[0])
    yg = _s5_scan(u, bre, bim, cw, a_re, a_im, row(ssm_d[0]), tm)

    pad = LANES - N_EXPERTS - N_EXPERT_GROUPS
    w_route = jnp.concatenate(
        [w_route_expert[0], w_route_group[0], jnp.zeros((d, pad), F32)], axis=1)
    b_route = jnp.concatenate(
        [b_route_expert[0], b_route_group[0], jnp.zeros((pad,), F32)]).reshape(1, LANES)
    x1, h2p, route, counts = _mixer_back(
        yg, ma, gb, xt, w_glu_a[0].astype(BF16), w_glu_b[0].astype(BF16), w_o[0].astype(BF16),
        row(norm_ffn[0]), w_route, b_route, tm)

    slot0, slot1, tile_expert, tile_rows = _dispatch_plan(route, counts, MOE_TILE)
    nslot = 2 * t + N_EXPERTS * MOE_TILE
    xs = _sc_dispatch(h2p, slot0, slot1, nslot)
    ys = _moe_experts(tile_expert, tile_rows, xs, w_gate[0], w_up[0], w_down[0], MOE_TILE)
    ycat = _sc_gather(ys, jnp.concatenate([slot0, slot1]))
    out = _final(x1, ycat, route, row(norm_final), tm)
    return out.reshape(bsz, length, d)
```

```python
import functools

import jax
import jax.numpy as jnp
from jax import lax
from jax.experimental import pallas as pl
from jax.experimental.pallas import tpu as pltpu
from jax.experimental.pallas import tpu_sc as plsc

F32 = jnp.float32
BF16 = jnp.bfloat16
I32 = jnp.int32

D_MODEL = 1024
CONV_WIDTH = 1024
SSM_WIDTH = 512
SSM_GROUP = 16
SSM_GROUPS = 32
SSM_STATE = 64
N_EXPERT_GROUPS = 4
EXPERTS_PER_GROUP = 8
N_EXPERTS = 32
EXPERT_FF = 512
EPS = 1e-6

LANES = 128
MXU_DIM = 256
TOKEN_TILE = 512
MOE_TILE = 256
SCAN_PITCH = 36
STATE_ROWS = 2 * SSM_GROUPS * SSM_STATE // LANES
HALF_ROWS = STATE_ROWS // 2
ROUTE_GROUP_LANE = N_EXPERTS
PACKED = D_MODEL // 2
HI_MASK = -65536
SC_CORES = 2
SC_SUBCORES = 16
SC_CHUNK = 128
VMEM_LIMIT = 56 * 1024 * 1024

_ARB = pltpu.CompilerParams(dimension_semantics=("arbitrary",), vmem_limit_bytes=VMEM_LIMIT)


def _const_spec(shape):
    nd = len(shape)
    return pl.BlockSpec(shape, lambda i, *_: (0,) * nd, pipeline_mode=pl.Buffered(1))


def _row_spec(tm, width):
    return pl.BlockSpec((tm, width), lambda i, *_: (i, 0))


def _rms(x, g):
    ms = jnp.mean(x * x, axis=-1, keepdims=True)
    return x * lax.rsqrt(ms + EPS) * g


def _pack_rows(v):
    bits = lax.bitcast_convert_type(v.astype(BF16).astype(F32), I32)
    lo = lax.shift_right_logical(bits[:, :PACKED], 16)
    hi = bits[:, PACKED:] & HI_MASK
    return lo | hi


def _unpack_rows(w):
    lo = lax.bitcast_convert_type(lax.shift_left(w, 16), F32)
    hi = lax.bitcast_convert_type(w & HI_MASK, F32)
    return lo, hi


def _front_kernel(x_ref, g_ref, win_ref, cw_ref, cb_ref, wco_ref, gbias_ref,
                  ma_ref, gb_ref, u_ref, carry_ref):
    tm = x_ref.shape[0]
    c0, c1, c2 = CONV_WIDTH, 2 * CONV_WIDTH, 3 * CONV_WIDTH
    c3 = c2 + SSM_WIDTH

    @pl.when(pl.program_id(0) == 0)
    def _():
        carry_ref[...] = jnp.zeros_like(carry_ref)

    h = _rms(x_ref[...], g_ref[...]).astype(BF16)

    def proj(lo, hi):
        return jnp.dot(h, win_ref[:, lo:hi], preferred_element_type=F32)

    v = proj(c1, c2) * proj(0, c0)
    row = lax.broadcasted_iota(jnp.int32, v.shape, 0)
    prev1 = carry_ref[7:8, :]
    prev2 = carry_ref[6:7, :]
    v1 = jnp.where(row == 0, prev1, pltpu.roll(v, 1, 0))
    v2 = jnp.where(row == 0, prev2, jnp.where(row == 1, prev1, pltpu.roll(v, 2, 0)))
    carry_ref[...] = v[tm - 8:, :]
    y = cw_ref[0:1, :] * v2 + cw_ref[1:2, :] * v1 + cw_ref[2:3, :] * v + cb_ref[...]
    z = (proj(c0, c1) * y).astype(BF16)
    ya = jnp.dot(z, wco_ref[...], preferred_element_type=F32)
    ga = jax.nn.sigmoid(proj(c3, c3 + D_MODEL) + gbias_ref[0:1, :])
    ma_ref[...] = (ga * ya).astype(BF16)
    gb = jax.nn.sigmoid(proj(c3 + D_MODEL, c3 + 2 * D_MODEL) + gbias_ref[1:2, :])
    gb_ref[...] = gb.astype(BF16)
    u_ref[...] = proj(c2, c3).astype(BF16)


def _mixer_front(x, norm_g, w_in, conv_w, conv_b, w_conv_out, gate_bias, tm):
    t = x.shape[0]
    in_cols = w_in.shape[1]
    return pl.pallas_call(
        _front_kernel,
        grid=(t // tm,),
        in_specs=[
            _row_spec(tm, D_MODEL),
            _const_spec((1, D_MODEL)),
            _const_spec((D_MODEL, in_cols)),
            _const_spec((3, CONV_WIDTH)),
            _const_spec((1, CONV_WIDTH)),
            _const_spec((CONV_WIDTH, D_MODEL)),
            _const_spec((2, D_MODEL)),
        ],
        out_specs=[_row_spec(tm, D_MODEL), _row_spec(tm, D_MODEL), _row_spec(tm, SSM_WIDTH)],
        out_shape=[
            jax.ShapeDtypeStruct((t, D_MODEL), BF16),
            jax.ShapeDtypeStruct((t, D_MODEL), BF16),
            jax.ShapeDtypeStruct((t, SSM_WIDTH), BF16),
        ],
        scratch_shapes=[pltpu.VMEM((8, CONV_WIDTH), F32)],
        compiler_params=_ARB,
        name="mixer_front",
    )(x, norm_g, w_in, conv_w, conv_b, w_conv_out, gate_bias)


def _ssm_kernel(u_ref, bre_ref, bim_ref, cw_ref, are_ref, aim_ref, d_ref,
                y_ref, r_ref, state_ref):
    tm = u_ref.shape[0]
    tiles_per_half = MXU_DIM * 4 // LANES

    @pl.when(pl.program_id(0) == 0)
    def _():
        state_ref[...] = jnp.zeros_like(state_ref)

    u = u_ref[...]
    for k in range(2):
        uk = u[:, k * MXU_DIM:(k + 1) * MXU_DIM]
        re = jnp.dot(uk, bre_ref[k], preferred_element_type=F32)
        im = jnp.dot(uk, bim_ref[k], preferred_element_type=F32)
        for jj in range(tiles_per_half):
            j = k * tiles_per_half + jj
            sl = slice(jj * LANES, (jj + 1) * LANES)
            r_ref[pl.ds(j, tm, stride=SCAN_PITCH), :] = re[:, sl]
            r_ref[pl.ds(HALF_ROWS + j, tm, stride=SCAN_PITCH), :] = im[:, sl]

    a_re = are_ref[...]
    a_im = aim_ref[...]

    def step(t, carry):
        s_re, s_im = carry
        base = t * SCAN_PITCH
        b_re = r_ref[pl.ds(base, HALF_ROWS), :]
        b_im = r_ref[pl.ds(base + HALF_ROWS, HALF_ROWS), :]
        n_re = a_re * s_re - a_im * s_im + b_re
        n_im = a_re * s_im + a_im * s_re + b_im
        r_ref[pl.ds(base, HALF_ROWS), :] = n_re
        r_ref[pl.ds(base + HALF_ROWS, HALF_ROWS), :] = n_im
        return n_re, n_im

    s_re, s_im = lax.fori_loop(
        0, tm, step, (state_ref[0:HALF_ROWS, :], state_ref[HALF_ROWS:, :]), unroll=8)
    state_ref[0:HALF_ROWS, :] = s_re
    state_ref[HALF_ROWS:, :] = s_im

    ys = []
    for k in range(2):
        cols = []
        for half in range(2):
            for jj in range(tiles_per_half):
                j = half * HALF_ROWS + k * tiles_per_half + jj
                cols.append(r_ref[pl.ds(j, tm, stride=SCAN_PITCH), :])
        s = jnp.concatenate(cols, axis=1).astype(BF16)
        ys.append(jnp.dot(s, cw_ref[k], preferred_element_type=F32))
    y = jnp.concatenate(ys, axis=1) + d_ref[...] * u.astype(F32)
    y_ref[...] = jax.nn.gelu(y).astype(BF16)


def _s5_scan(u, bre, bim, cw, a_re, a_im, d_skip, tm):
    t = u.shape[0]
    return pl.pallas_call(
        _ssm_kernel,
        grid=(t // tm,),
        in_specs=[
            _row_spec(tm, SSM_WIDTH),
            _const_spec(bre.shape),
            _const_spec(bim.shape),
            _const_spec(cw.shape),
            _const_spec(a_re.shape),
            _const_spec(a_im.shape),
            _const_spec((1, SSM_WIDTH)),
        ],
        out_specs=_row_spec(tm, SSM_WIDTH),
        out_shape=jax.ShapeDtypeStruct((t, SSM_WIDTH), BF16),
        scratch_shapes=[
            pltpu.VMEM((tm * SCAN_PITCH, LANES), F32),
            pltpu.VMEM((STATE_ROWS, LANES), F32),
        ],
        compiler_params=_ARB,
        name="s5_scan",
    )(u, bre, bim, cw, a_re, a_im, d_skip)


def _s5_weights(a_re, a_im, log_dt, b_re, b_im, c_re, c_im):
    g, n, h = SSM_GROUPS, SSM_STATE, SSM_GROUP
    gpt = MXU_DIM // h
    lam = lax.complex(a_re.astype(F32), a_im.astype(F32))
    dt = jnp.exp(log_dt.astype(F32))[:, None]
    a_bar = jnp.exp(lam * dt)
    b_bar = ((a_bar - 1.0) / lam)[..., None] * lax.complex(b_re.astype(F32), b_im.astype(F32))
    eye = jnp.eye(gpt, dtype=F32)

    def blk_in(w):
        w = w.reshape(g // gpt, gpt, n, h).transpose(0, 1, 3, 2)
        return jnp.einsum("kghn,gf->kghfn", w, eye).reshape(g // gpt, gpt * h, gpt * n)

    def blk_out(c):
        c = c.astype(F32).reshape(g // gpt, gpt, h, n).transpose(0, 1, 3, 2)
        return jnp.einsum("kgnh,gf->kgnfh", c, eye).reshape(g // gpt, gpt * n, gpt * h)

    bre = blk_in(jnp.real(b_bar)).astype(BF16)
    bim = blk_in(jnp.imag(b_bar)).astype(BF16)
    cw = jnp.concatenate([blk_out(c_re), -blk_out(c_im)], axis=1).astype(BF16)
    tile = (HALF_ROWS, LANES)
    return bre, bim, cw, jnp.real(a_bar).reshape(tile), jnp.imag(a_bar).reshape(tile)


def _back_kernel(yg_ref, ma_ref, gb_ref, x_ref, wa_ref, wb_ref, wo_ref, g2_ref, wr_ref, br_ref,
                 x1_ref, h2p_ref, route_ref, counts_ref):
    tm = x_ref.shape[0]

    @pl.when(pl.program_id(0) == 0)
    def _():
        counts_ref[...] = jnp.zeros_like(counts_ref)

    yg = yg_ref[...]
    yb = (jnp.dot(yg, wa_ref[...], preferred_element_type=F32)
          * jax.nn.sigmoid(jnp.dot(yg, wb_ref[...], preferred_element_type=F32)))
    mixed = (ma_ref[...].astype(F32) + gb_ref[...].astype(F32) * yb).astype(BF16)
    x1 = x_ref[...] + jnp.dot(mixed, wo_ref[...], preferred_element_type=F32)
    x1_ref[...] = x1
    h2 = _rms(x1, g2_ref[...])
    h2p_ref[...] = _pack_rows(h2)

    h2_hi = h2.astype(BF16)
    h2_lo = (h2 - h2_hi.astype(F32)).astype(BF16)
    logits = (jnp.dot(h2_hi, wr_ref[0], preferred_element_type=F32)
              + (jnp.dot(h2_hi, wr_ref[1], preferred_element_type=F32)
                 + jnp.dot(h2_lo, wr_ref[0], preferred_element_type=F32))) + br_ref[...]
    lane = lax.broadcasted_iota(jnp.int32, logits.shape, 1)
    lane_f = lane.astype(F32)
    neg = jnp.float32(-jnp.inf)
    big = jnp.float32(LANES)
    is_grp = (lane >= ROUTE_GROUP_LANE) & (lane < ROUTE_GROUP_LANE + N_EXPERT_GROUPS)
    gl = jnp.where(is_grp, logits, neg)
    gmax = jnp.max(gl, axis=1, keepdims=True)
    gidx = jnp.min(jnp.where(gl == gmax, lane_f - ROUTE_GROUP_LANE, big), axis=1, keepdims=True)
    pg_top = 1.0 / jnp.sum(jnp.exp(gl - gmax), axis=1, keepdims=True)
    lane_grp = (lane // EXPERTS_PER_GROUP).astype(F32)
    el = jnp.where((lane < N_EXPERTS) & (lane_grp == gidx), logits, neg)
    l1 = jnp.max(el, axis=1, keepdims=True)
    i1 = jnp.min(jnp.where(el == l1, lane_f, big), axis=1, keepdims=True)
    el2 = jnp.where(lane_f == i1, neg, el)
    l2 = jnp.max(el2, axis=1, keepdims=True)
    i2 = jnp.min(jnp.where(el2 == l2, lane_f, big), axis=1, keepdims=True)
    r = jnp.exp(l2 - l1)
    w1 = pg_top / (1.0 + r)
    w2 = pg_top * r / (1.0 + r)

    oh1 = (lane_f == i1).astype(F32)
    oh2 = (lane_f == i2).astype(F32)
    picked = oh1 + oh2
    rr = lax.broadcasted_iota(jnp.int32, (tm, tm), 0)
    cc = lax.broadcasted_iota(jnp.int32, (tm, tm), 1)
    before = (cc < rr).astype(BF16)
    prior = jnp.dot(before, picked.astype(BF16), preferred_element_type=F32) + counts_ref[...]
    rank1 = jnp.sum(oh1 * prior, axis=1, keepdims=True)
    rank2 = jnp.sum(oh2 * prior, axis=1, keepdims=True)
    counts_ref[...] += jnp.sum(picked, axis=0, keepdims=True)

    route_ref[...] = jnp.where(
        lane == 0, i1, jnp.where(lane == 1, i2, jnp.where(lane == 2, w1, jnp.where(
            lane == 3, w2, jnp.where(lane == 4, rank1, jnp.where(lane == 5, rank2, 0.0))))))


def _mixer_back(yg, ma, gb, x, w_glu_a, w_glu_b, w_o, norm_g, w_route, b_route, tm):
    t = x.shape[0]
    return pl.pallas_call(
        _back_kernel,
        grid=(t // tm,),
        in_specs=[
            _row_spec(tm, SSM_WIDTH),
            _row_spec(tm, D_MODEL),
            _row_spec(tm, D_MODEL),
            _row_spec(tm, D_MODEL),
            _const_spec((SSM_WIDTH, D_MODEL)),
            _const_spec((SSM_WIDTH, D_MODEL)),
            _const_spec((D_MODEL, D_MODEL)),
            _const_spec((1, D_MODEL)),
            _const_spec((2, D_MODEL, LANES)),
            _const_spec((1, LANES)),
        ],
        out_specs=[_row_spec(tm, D_MODEL), _row_spec(tm, PACKED), _row_spec(tm, LANES),
                   pl.BlockSpec((1, LANES), lambda i: (0, 0))],
        out_shape=[
            jax.ShapeDtypeStruct((t, D_MODEL), F32),
            jax.ShapeDtypeStruct((t, PACKED), I32),
            jax.ShapeDtypeStruct((t, LANES), F32),
            jax.ShapeDtypeStruct((1, LANES), F32),
        ],
        compiler_params=_ARB,
        name="mixer_back",
    )(yg, ma, gb, x, w_glu_a, w_glu_b, w_o, norm_g, w_route, b_route)


_SC_MESH = dict(core_axis_name="c", subcore_axis_name="s")


def _sc_worker():
    return lax.axis_index("s") * SC_CORES + lax.axis_index("c")


def _sc_dispatch(rows, slot0, slot1, nslot):
    t, width = rows.shape
    per_w = t // (SC_CORES * SC_SUBCORES)
    assert per_w % SC_CHUNK == 0

    @functools.partial(
        pl.kernel,
        out_type=jax.ShapeDtypeStruct((nslot, width), rows.dtype),
        mesh=plsc.VectorSubcoreMesh(**_SC_MESH),
        scratch_types=[
            pltpu.VMEM((1, SC_CHUNK), I32),
            pltpu.VMEM((1, SC_CHUNK), I32),
            pltpu.VMEM((SC_CHUNK, width), rows.dtype),
        ],
        name="moe_dispatch",
    )
    def k(rows_hbm, s0_hbm, s1_hbm, out_hbm, i0_v, i1_v, rows_v):
        base = _sc_worker() * per_w

        @pl.loop(0, per_w // SC_CHUNK)
        def _(c):
            off = base + c * SC_CHUNK
            pltpu.sync_copy(s0_hbm.at[:, pl.ds(off, SC_CHUNK)], i0_v)
            pltpu.sync_copy(s1_hbm.at[:, pl.ds(off, SC_CHUNK)], i1_v)
            pltpu.sync_copy(rows_hbm.at[pl.ds(off, SC_CHUNK)], rows_v)
            pltpu.sync_copy(rows_v, out_hbm.at[i0_v.at[0]])
            pltpu.sync_copy(rows_v, out_hbm.at[i1_v.at[0]])

    return k(rows, slot0.reshape(1, t), slot1.reshape(1, t))


def _sc_gather(table, idx):
    n = idx.shape[0]
    width = table.shape[1]
    per_w = n // (SC_CORES * SC_SUBCORES)
    assert per_w % SC_CHUNK == 0

    @functools.partial(
        pl.kernel,
        out_type=jax.ShapeDtypeStruct((n, width), table.dtype),
        mesh=plsc.VectorSubcoreMesh(**_SC_MESH),
        scratch_types=[
            pltpu.VMEM((1, SC_CHUNK), I32),
            pltpu.VMEM((SC_CHUNK, width), table.dtype),
        ],
        name="moe_combine_gather",
    )
    def k(table_hbm, idx_hbm, out_hbm, idx_v, rows_v):
        base = _sc_worker() * per_w

        @pl.loop(0, per_w // SC_CHUNK)
        def _(c):
            off = base + c * SC_CHUNK
            pltpu.sync_copy(idx_hbm.at[:, pl.ds(off, SC_CHUNK)], idx_v)
            pltpu.sync_copy(table_hbm.at[idx_v.at[0]], rows_v)
            pltpu.sync_copy(rows_v, out_hbm.at[pl.ds(off, SC_CHUNK)])

    return k(table, idx.reshape(1, n))


def _moe_kernel(tile_expert_ref, tile_rows_ref, xs_ref, wg_ref, wu_ref, wd_ref, out_ref,
                wg_s, wu_s, wd_s):
    i = pl.program_id(0)
    tm = xs_ref.shape[0]
    valid = tile_rows_ref[i]
    prev = tile_expert_ref[jnp.maximum(i - 1, 0)]

    @pl.when((i == 0) | (tile_expert_ref[i] != prev))
    def _():
        wg_s[...] = wg_ref[0].astype(BF16)
        wu_s[...] = wu_ref[0].astype(BF16)
        wd_s[...] = wd_ref[0].astype(BF16)

    @pl.when(valid > 0)
    def _():
        row = lax.broadcasted_iota(jnp.int32, (tm, PACKED), 0)
        words = jnp.where(row < valid, xs_ref[...], 0)
        lo, hi = _unpack_rows(words)
        lo = lo.astype(BF16)
        hi = hi.astype(BF16)
        gate = (jnp.dot(lo, wg_s[:PACKED, :], preferred_element_type=F32)
                + jnp.dot(hi, wg_s[PACKED:, :], preferred_element_type=F32))
        up = (jnp.dot(lo, wu_s[:PACKED, :], preferred_element_type=F32)
              + jnp.dot(hi, wu_s[PACKED:, :], preferred_element_type=F32))
        act = (gate * jax.nn.sigmoid(gate) * up).astype(BF16)
        out_ref[...] = _pack_rows(jnp.dot(act, wd_s[...], preferred_element_type=F32))

    @pl.when(valid <= 0)
    def _():
        out_ref[...] = jnp.zeros_like(out_ref)


def _moe_experts(tile_expert, tile_rows, xs, w_gate, w_up, w_down, tm):
    nslot = xs.shape[0]
    grid_spec = pltpu.PrefetchScalarGridSpec(
        num_scalar_prefetch=2,
        grid=(nslot // tm,),
        in_specs=[
            _row_spec(tm, PACKED),
            pl.BlockSpec((1, D_MODEL, EXPERT_FF), lambda i, te, tr: (te[i], 0, 0)),
            pl.BlockSpec((1, D_MODEL, EXPERT_FF), lambda i, te, tr: (te[i], 0, 0)),
            pl.BlockSpec((1, EXPERT_FF, D_MODEL), lambda i, te, tr: (te[i], 0, 0)),
        ],
        out_specs=_row_spec(tm, PACKED),
        scratch_shapes=[
            pltpu.VMEM((D_MODEL, EXPERT_FF), BF16),
            pltpu.VMEM((D_MODEL, EXPERT_FF), BF16),
            pltpu.VMEM((EXPERT_FF, D_MODEL), BF16),
        ],
    )
    return pl.pallas_call(
        _moe_kernel,
        grid_spec=grid_spec,
        out_shape=jax.ShapeDtypeStruct((nslot, PACKED), I32),
        compiler_params=_ARB,
        name="moe_experts",
    )(tile_expert, tile_rows, xs, w_gate, w_up, w_down)


def _dispatch_plan(route, counts, tm):
    t = route.shape[0]
    e_ids = route[:, 0:2].astype(I32)
    ranks = route[:, 4:6].astype(I32)
    counts = counts[0, :N_EXPERTS].astype(I32)
    padded = ((counts + tm - 1) // tm) * tm
    ends = jnp.cumsum(padded)
    starts = ends - padded
    experts = jnp.arange(N_EXPERTS, dtype=I32)
    slot = jnp.sum(jnp.where(e_ids[..., None] == experts, starts, 0), axis=-1) + ranks
    n_tiles = (2 * t + N_EXPERTS * tm) // tm
    tile_start = jnp.arange(n_tiles, dtype=I32) * tm
    total = ends[-1]
    owner = lambda s: jnp.sum((ends <= s[..., None]).astype(I32), axis=-1)
    tile_expert = jnp.where(tile_start < total, owner(tile_start), owner(total - 1))
    tile_expert = jnp.minimum(tile_expert, N_EXPERTS - 1)
    live_end = jnp.sum(jnp.where(tile_expert[:, None] == experts, starts + counts, 0), axis=-1)
    tile_rows = jnp.where(tile_start < total, jnp.clip(live_end - tile_start, 0, tm), 0)
    return slot[:, 0], slot[:, 1], tile_expert.astype(I32), tile_rows.astype(I32)


def _final_kernel(x1_ref, y0_ref, y1_ref, route_ref, g_ref, out_ref):
    route = route_ref[...]
    w1 = route[:, 2:3]
    w2 = route[:, 3:4]
    a_lo, a_hi = _unpack_rows(y0_ref[...])
    b_lo, b_hi = _unpack_rows(y1_ref[...])
    moe = jnp.concatenate([w1 * a_lo + w2 * b_lo, w1 * a_hi + w2 * b_hi], axis=1)
    out_ref[...] = _rms(x1_ref[...] + moe, g_ref[...])


def _final(x1, ycat, route, norm_g, tm):
    t = x1.shape[0]
    nblk = t // tm
    return pl.pallas_call(
        _final_kernel,
        grid=(nblk,),
        in_specs=[
            _row_spec(tm, D_MODEL),
            pl.BlockSpec((tm, PACKED), lambda i: (i, 0)),
            pl.BlockSpec((tm, PACKED), lambda i: (i + nblk, 0)),
            _row_spec(tm, LANES),
            _const_spec((1, D_MODEL)),
        ],
        out_specs=_row_spec(tm, D_MODEL),
        out_shape=jax.ShapeDtypeStruct((t, D_MODEL), F32),
        compiler_params=_ARB,
        name="final_norm",
    )(x1, ycat, ycat, route, norm_g)


def kernel(x, norm_mix, w_in, conv_w, conv_b, w_conv_out, ssm_a_re, ssm_a_im, ssm_log_dt,
           ssm_b_re, ssm_b_im, ssm_c_re, ssm_c_im, ssm_d, w_glu_a, w_glu_b, gate_bias, w_o,
           norm_ffn, w_route_group, b_route_group, w_route_expert, b_route_expert,
           w_gate, w_up, w_down, norm_final):
    bsz, length, d = x.shape
    assert d == D_MODEL and norm_mix.shape[0] == 1
    t = bsz * length
    tm = TOKEN_TILE
    assert bsz == 1 and t % tm == 0
    xt = x.reshape(t, d)
    row = lambda a: a.reshape(1, -1).astype(F32)

    ma, gb, u = _mixer_front(
        xt, row(norm_mix[0]), w_in[0].astype(BF16), conv_w[0].reshape(3, CONV_WIDTH),
        row(conv_b[0]), w_conv_out[0].astype(BF16), gate_bias[0], tm)

    bre, bim, cw, a_re, a_im = _s5_weights(
        ssm_a_re[0], ssm_a_im[0], ssm_log_dt[0], ssm_b_re[0], ssm_b_im[0],
        ssm_c_re[0], ssm_c_im[0])
    yg = _s5_scan(u, bre, bim, cw, a_re, a_im, row(ssm_d[0]), tm)

    pad = LANES - N_EXPERTS - N_EXPERT_GROUPS
    w_route = jnp.concatenate(
        [w_route_expert[0], w_route_group[0], jnp.zeros((d, pad), F32)], axis=1)
    w_route_hi = w_route.astype(BF16)
    w_route_lo = (w_route - w_route_hi.astype(F32)).astype(BF16)
    b_route = jnp.concatenate(
        [b_route_expert[0], b_route_group[0], jnp.zeros((pad,), F32)]).reshape(1, LANES)
    x1, h2p, route, counts = _mixer_back(
        yg, ma, gb, xt, w_glu_a[0].astype(BF16), w_glu_b[0].astype(BF16), w_o[0].astype(BF16),
        row(norm_ffn[0]), jnp.stack([w_route_hi, w_route_lo]), b_route, tm)

    slot0, slot1, tile_expert, tile_rows = _dispatch_plan(route, counts, MOE_TILE)
    nslot = 2 * t + N_EXPERTS * MOE_TILE
    xs = _sc_dispatch(h2p, slot0, slot1, nslot)
    ys = _moe_experts(tile_expert, tile_rows, xs, w_gate[0], w_up[0], w_down[0], MOE_TILE)
    ycat = _sc_gather(ys, jnp.concatenate([slot0, slot1]))
    out = _final(x1, ycat, route, row(norm_final), tm)
    return out.reshape(bsz, length, d)
```

```python
import functools

import jax
import jax.numpy as jnp
from jax import lax
from jax.experimental import pallas as pl
from jax.experimental.pallas import tpu as pltpu
from jax.experimental.pallas import tpu_sc as plsc

F32 = jnp.float32
BF16 = jnp.bfloat16
I32 = jnp.int32

D_MODEL = 1024
CONV_WIDTH = 1024
SSM_WIDTH = 512
SSM_GROUP = 16
SSM_GROUPS = 32
SSM_STATE = 64
N_EXPERT_GROUPS = 4
EXPERTS_PER_GROUP = 8
N_EXPERTS = 32
EXPERT_FF = 512
EPS = 1e-6

LANES = 128
MXU_DIM = 256
TOKEN_TILE = 512
MOE_TILE = 256
SCAN_PITCH = 36
STATE_ROWS = 2 * SSM_GROUPS * SSM_STATE // LANES
HALF_ROWS = STATE_ROWS // 2
ROUTE_GROUP_LANE = N_EXPERTS
PACKED = D_MODEL // 2
HI_MASK = -65536
SC_CORES = 2
SC_SUBCORES = 16
SC_CHUNK = 128
VMEM_LIMIT = 56 * 1024 * 1024

_ARB = pltpu.CompilerParams(dimension_semantics=("arbitrary",), vmem_limit_bytes=VMEM_LIMIT)


def _const_spec(shape):
    nd = len(shape)
    return pl.BlockSpec(shape, lambda i, *_: (0,) * nd, pipeline_mode=pl.Buffered(1))


def _row_spec(tm, width):
    return pl.BlockSpec((tm, width), lambda i, *_: (i, 0))


def _rms(x, g):
    ms = jnp.mean(x * x, axis=-1, keepdims=True)
    return x * lax.rsqrt(ms + EPS) * g


def _pack_rows(v):
    bits = lax.bitcast_convert_type(v.astype(BF16).astype(F32), I32)
    lo = lax.shift_right_logical(bits[:, :PACKED], 16)
    hi = bits[:, PACKED:] & HI_MASK
    return lo | hi


def _unpack_rows(w):
    lo = lax.bitcast_convert_type(lax.shift_left(w, 16), F32)
    hi = lax.bitcast_convert_type(w & HI_MASK, F32)
    return lo, hi


def _front_kernel(x_ref, g_ref, win_ref, cw_ref, cb_ref, wco_ref, gbias_ref,
                  ma_ref, gb_ref, u_ref, carry_ref):
    tm = x_ref.shape[0]
    c0, c1, c2 = CONV_WIDTH, 2 * CONV_WIDTH, 3 * CONV_WIDTH
    c3 = c2 + SSM_WIDTH

    @pl.when(pl.program_id(0) == 0)
    def _():
        carry_ref[...] = jnp.zeros_like(carry_ref)

    h = _rms(x_ref[...], g_ref[...]).astype(BF16)

    def proj(lo, hi):
        return jnp.dot(h, win_ref[:, lo:hi], preferred_element_type=F32)

    v = proj(c1, c2) * proj(0, c0)
    row = lax.broadcasted_iota(jnp.int32, v.shape, 0)
    prev1 = carry_ref[7:8, :]
    prev2 = carry_ref[6:7, :]
    v1 = jnp.where(row == 0, prev1, pltpu.roll(v, 1, 0))
    v2 = jnp.where(row == 0, prev2, jnp.where(row == 1, prev1, pltpu.roll(v, 2, 0)))
    carry_ref[...] = v[tm - 8:, :]
    y = cw_ref[0:1, :] * v2 + cw_ref[1:2, :] * v1 + cw_ref[2:3, :] * v + cb_ref[...]
    z = (proj(c0, c1) * y).astype(BF16)
    ya = jnp.dot(z, wco_ref[...], preferred_element_type=F32)
    ga = jax.nn.sigmoid(proj(c3, c3 + D_MODEL) + gbias_ref[0:1, :])
    ma_ref[...] = (ga * ya).astype(BF16)
    gb = jax.nn.sigmoid(proj(c3 + D_MODEL, c3 + 2 * D_MODEL) + gbias_ref[1:2, :])
    gb_ref[...] = gb.astype(BF16)
    u_ref[...] = proj(c2, c3).astype(BF16)


def _mixer_front(x, norm_g, w_in, conv_w, conv_b, w_conv_out, gate_bias, tm):
    t = x.shape[0]
    in_cols = w_in.shape[1]
    return pl.pallas_call(
        _front_kernel,
        grid=(t // tm,),
        in_specs=[
            _row_spec(tm, D_MODEL),
            _const_spec((1, D_MODEL)),
            _const_spec((D_MODEL, in_cols)),
            _const_spec((3, CONV_WIDTH)),
            _const_spec((1, CONV_WIDTH)),
            _const_spec((CONV_WIDTH, D_MODEL)),
            _const_spec((2, D_MODEL)),
        ],
        out_specs=[_row_spec(tm, D_MODEL), _row_spec(tm, D_MODEL), _row_spec(tm, SSM_WIDTH)],
        out_shape=[
            jax.ShapeDtypeStruct((t, D_MODEL), BF16),
            jax.ShapeDtypeStruct((t, D_MODEL), BF16),
            jax.ShapeDtypeStruct((t, SSM_WIDTH), BF16),
        ],
        scratch_shapes=[pltpu.VMEM((8, CONV_WIDTH), F32)],
        compiler_params=_ARB,
        name="mixer_front",
    )(x, norm_g, w_in, conv_w, conv_b, w_conv_out, gate_bias)


def _ssm_kernel(u_ref, bre_ref, bim_ref, cw_ref, are_ref, aim_ref, d_ref,
                y_ref, r_ref, state_ref):
    tm = u_ref.shape[0]
    tiles_per_half = MXU_DIM * 4 // LANES

    @pl.when(pl.program_id(0) == 0)
    def _():
        state_ref[...] = jnp.zeros_like(state_ref)

    u = u_ref[...]
    for k in range(2):
        uk = u[:, k * MXU_DIM:(k + 1) * MXU_DIM]
        re = jnp.dot(uk, bre_ref[k], preferred_element_type=F32)
        im = jnp.dot(uk, bim_ref[k], preferred_element_type=F32)
        for jj in range(tiles_per_half):
            j = k * tiles_per_half + jj
            sl = slice(jj * LANES, (jj + 1) * LANES)
            r_ref[pl.ds(j, tm, stride=SCAN_PITCH), :] = re[:, sl]
            r_ref[pl.ds(HALF_ROWS + j, tm, stride=SCAN_PITCH), :] = im[:, sl]

    a_re = are_ref[...]
    a_im = aim_ref[...]

    def step(t, carry):
        s_re, s_im = carry
        base = t * SCAN_PITCH
        b_re = r_ref[pl.ds(base, HALF_ROWS), :]
        b_im = r_ref[pl.ds(base + HALF_ROWS, HALF_ROWS), :]
        n_re = a_re * s_re - a_im * s_im + b_re
        n_im = a_re * s_im + a_im * s_re + b_im
        r_ref[pl.ds(base, HALF_ROWS), :] = n_re
        r_ref[pl.ds(base + HALF_ROWS, HALF_ROWS), :] = n_im
        return n_re, n_im

    s_re, s_im = lax.fori_loop(
        0, tm, step, (state_ref[0:HALF_ROWS, :], state_ref[HALF_ROWS:, :]), unroll=8)
    state_ref[0:HALF_ROWS, :] = s_re
    state_ref[HALF_ROWS:, :] = s_im

    ys = []
    for k in range(2):
        cols = []
        for half in range(2):
            for jj in range(tiles_per_half):
                j = half * HALF_ROWS + k * tiles_per_half + jj
                cols.append(r_ref[pl.ds(j, tm, stride=SCAN_PITCH), :])
        s = jnp.concatenate(cols, axis=1).astype(BF16)
        ys.append(jnp.dot(s, cw_ref[k], preferred_element_type=F32))
    y = jnp.concatenate(ys, axis=1) + d_ref[...] * u.astype(F32)
    y_ref[...] = jax.nn.gelu(y).astype(BF16)


def _s5_scan(u, bre, bim, cw, a_re, a_im, d_skip, tm):
    t = u.shape[0]
    return pl.pallas_call(
        _ssm_kernel,
        grid=(t // tm,),
        in_specs=[
            _row_spec(tm, SSM_WIDTH),
            _const_spec(bre.shape),
            _const_spec(bim.shape),
            _const_spec(cw.shape),
            _const_spec(a_re.shape),
            _const_spec(a_im.shape),
            _const_spec((1, SSM_WIDTH)),
        ],
        out_specs=_row_spec(tm, SSM_WIDTH),
        out_shape=jax.ShapeDtypeStruct((t, SSM_WIDTH), BF16),
        scratch_shapes=[
            pltpu.VMEM((tm * SCAN_PITCH, LANES), F32),
            pltpu.VMEM((STATE_ROWS, LANES), F32),
        ],
        compiler_params=_ARB,
        name="s5_scan",
    )(u, bre, bim, cw, a_re, a_im, d_skip)


def _s5_weights(a_re, a_im, log_dt, b_re, b_im, c_re, c_im):
    g, n, h = SSM_GROUPS, SSM_STATE, SSM_GROUP
    gpt = MXU_DIM // h
    lam = lax.complex(a_re.astype(F32), a_im.astype(F32))
    dt = jnp.exp(log_dt.astype(F32))[:, None]
    a_bar = jnp.exp(lam * dt)
    b_bar = ((a_bar - 1.0) / lam)[..., None] * lax.complex(b_re.astype(F32), b_im.astype(F32))
    eye = jnp.eye(gpt, dtype=F32)

    def blk_in(w):
        w = w.reshape(g // gpt, gpt, n, h).transpose(0, 1, 3, 2)
        return jnp.einsum("kghn,gf->kghfn", w, eye).reshape(g // gpt, gpt * h, gpt * n)

    def blk_out(c):
        c = c.astype(F32).reshape(g // gpt, gpt, h, n).transpose(0, 1, 3, 2)
        return jnp.einsum("kgnh,gf->kgnfh", c, eye).reshape(g // gpt, gpt * n, gpt * h)

    bre = blk_in(jnp.real(b_bar)).astype(BF16)
    bim = blk_in(jnp.imag(b_bar)).astype(BF16)
    cw = jnp.concatenate([blk_out(c_re), -blk_out(c_im)], axis=1).astype(BF16)
    tile = (HALF_ROWS, LANES)
    return bre, bim, cw, jnp.real(a_bar).reshape(tile), jnp.imag(a_bar).reshape(tile)


def _back_kernel(yg_ref, ma_ref, gb_ref, x_ref, wa_ref, wb_ref, wo_ref, g2_ref, wr_ref, br_ref,
                 x1_ref, h2p_ref, route_ref, counts_ref):
    tm = x_ref.shape[0]

    @pl.when(pl.program_id(0) == 0)
    def _():
        counts_ref[...] = jnp.zeros_like(counts_ref)

    yg = yg_ref[...]
    yb = (jnp.dot(yg, wa_ref[...], preferred_element_type=F32)
          * jax.nn.sigmoid(jnp.dot(yg, wb_ref[...], preferred_element_type=F32)))
    mixed = (ma_ref[...].astype(F32) + gb_ref[...].astype(F32) * yb).astype(BF16)
    x1 = x_ref[...] + jnp.dot(mixed, wo_ref[...], preferred_element_type=F32)
    x1_ref[...] = x1
    h2 = _rms(x1, g2_ref[...])
    h2p_ref[...] = _pack_rows(h2)

    h2_hi = h2.astype(BF16)
    h2_lo = (h2 - h2_hi.astype(F32)).astype(BF16)
    logits = (jnp.dot(h2_hi, wr_ref[0], preferred_element_type=F32)
              + (jnp.dot(h2_hi, wr_ref[1], preferred_element_type=F32)
                 + jnp.dot(h2_lo, wr_ref[0], preferred_element_type=F32))) + br_ref[...]
    lane = lax.broadcasted_iota(jnp.int32, logits.shape, 1)
    lane_f = lane.astype(F32)
    neg = jnp.float32(-jnp.inf)
    big = jnp.float32(LANES)
    is_grp = (lane >= ROUTE_GROUP_LANE) & (lane < ROUTE_GROUP_LANE + N_EXPERT_GROUPS)
    gl = jnp.where(is_grp, logits, neg)
    gmax = jnp.max(gl, axis=1, keepdims=True)
    gidx = jnp.min(jnp.where(gl == gmax, lane_f - ROUTE_GROUP_LANE, big), axis=1, keepdims=True)
    pg_top = 1.0 / jnp.sum(jnp.exp(gl - gmax), axis=1, keepdims=True)
    lane_grp = (lane // EXPERTS_PER_GROUP).astype(F32)
    el = jnp.where((lane < N_EXPERTS) & (lane_grp == gidx), logits, neg)
    l1 = jnp.max(el, axis=1, keepdims=True)
    i1 = jnp.min(jnp.where(el == l1, lane_f, big), axis=1, keepdims=True)
    el2 = jnp.where(lane_f == i1, neg, el)
    l2 = jnp.max(el2, axis=1, keepdims=True)
    i2 = jnp.min(jnp.where(el2 == l2, lane_f, big), axis=1, keepdims=True)
    r = jnp.exp(l2 - l1)
    w1 = pg_top / (1.0 + r)
    w2 = pg_top * r / (1.0 + r)

    oh1 = (lane_f == i1).astype(F32)
    oh2 = (lane_f == i2).astype(F32)
    picked = oh1 + oh2
    rr = lax.broadcasted_iota(jnp.int32, (tm, tm), 0)
    cc = lax.broadcasted_iota(jnp.int32, (tm, tm), 1)
    before = (cc < rr).astype(BF16)
    prior = jnp.dot(before, picked.astype(BF16), preferred_element_type=F32) + counts_ref[...]
    rank1 = jnp.sum(oh1 * prior, axis=1, keepdims=True)
    rank2 = jnp.sum(oh2 * prior, axis=1, keepdims=True)
    counts_ref[...] += jnp.sum(picked, axis=0, keepdims=True)

    route_ref[...] = jnp.where(
        lane == 0, i1, jnp.where(lane == 1, i2, jnp.where(lane == 2, w1, jnp.where(
            lane == 3, w2, jnp.where(lane == 4, rank1, jnp.where(lane == 5, rank2, 0.0))))))


def _mixer_back(yg, ma, gb, x, w_glu_a, w_glu_b, w_o, norm_g, w_route, b_route, tm):
    t = x.shape[0]
    return pl.pallas_call(
        _back_kernel,
        grid=(t // tm,),
        in_specs=[
            _row_spec(tm, SSM_WIDTH),
            _row_spec(tm, D_MODEL),
            _row_spec(tm, D_MODEL),
            _row_spec(tm, D_MODEL),
            _const_spec((SSM_WIDTH, D_MODEL)),
            _const_spec((SSM_WIDTH, D_MODEL)),
            _const_spec((D_MODEL, D_MODEL)),
            _const_spec((1, D_MODEL)),
            _const_spec((2, D_MODEL, LANES)),
            _const_spec((1, LANES)),
        ],
        out_specs=[_row_spec(tm, D_MODEL), _row_spec(tm, PACKED), _row_spec(tm, LANES),
                   pl.BlockSpec((1, LANES), lambda i: (0, 0))],
        out_shape=[
            jax.ShapeDtypeStruct((t, D_MODEL), F32),
            jax.ShapeDtypeStruct((t, PACKED), I32),
            jax.ShapeDtypeStruct((t, LANES), F32),
            jax.ShapeDtypeStruct((1, LANES), F32),
        ],
        compiler_params=_ARB,
        name="mixer_back",
    )(yg, ma, gb, x, w_glu_a, w_glu_b, w_o, norm_g, w_route, b_route)


_SC_MESH = dict(core_axis_name="c", subcore_axis_name="s")


def _sc_worker():
    return lax.axis_index("s") * SC_CORES + lax.axis_index("c")


def _sc_dispatch(rows, slot0, slot1, nslot):
    t, width = rows.shape
    per_w = t // (SC_CORES * SC_SUBCORES)
    assert per_w % SC_CHUNK == 0

    @functools.partial(
        pl.kernel,
        out_type=jax.ShapeDtypeStruct((nslot, width), rows.dtype),
        mesh=plsc.VectorSubcoreMesh(**_SC_MESH),
        scratch_types=[
            pltpu.VMEM((1, SC_CHUNK), I32),
            pltpu.VMEM((1, SC_CHUNK), I32),
            pltpu.VMEM((SC_CHUNK, width), rows.dtype),
        ],
        name="moe_dispatch",
    )
    def k(rows_hbm, s0_hbm, s1_hbm, out_hbm, i0_v, i1_v, rows_v):
        base = _sc_worker() * per_w

        @pl.loop(0, per_w // SC_CHUNK)
        def _(c):
            off = base + c * SC_CHUNK
            pltpu.sync_copy(s0_hbm.at[:, pl.ds(off, SC_CHUNK)], i0_v)
            pltpu.sync_copy(s1_hbm.at[:, pl.ds(off, SC_CHUNK)], i1_v)
            pltpu.sync_copy(rows_hbm.at[pl.ds(off, SC_CHUNK)], rows_v)
            pltpu.sync_copy(rows_v, out_hbm.at[i0_v.at[0]])
            pltpu.sync_copy(rows_v, out_hbm.at[i1_v.at[0]])

    return k(rows, slot0.reshape(1, t), slot1.reshape(1, t))


def _sc_gather(table, idx):
    n = idx.shape[0]
    width = table.shape[1]
    per_w = n // (SC_CORES * SC_SUBCORES)
    assert per_w % SC_CHUNK == 0

    @functools.partial(
        pl.kernel,
        out_type=jax.ShapeDtypeStruct((n, width), table.dtype),
        mesh=plsc.VectorSubcoreMesh(**_SC_MESH),
        scratch_types=[
            pltpu.VMEM((1, SC_CHUNK), I32),
            pltpu.VMEM((SC_CHUNK, width), table.dtype),
        ],
        name="moe_combine_gather",
    )
    def k(table_hbm, idx_hbm, out_hbm, idx_v, rows_v):
        base = _sc_worker() * per_w

        @pl.loop(0, per_w // SC_CHUNK)
        def _(c):
            off = base + c * SC_CHUNK
            pltpu.sync_copy(idx_hbm.at[:, pl.ds(off, SC_CHUNK)], idx_v)
            pltpu.sync_copy(table_hbm.at[idx_v.at[0]], rows_v)
            pltpu.sync_copy(rows_v, out_hbm.at[pl.ds(off, SC_CHUNK)])

    return k(table, idx.reshape(1, n))


def _moe_kernel(tile_expert_ref, tile_rows_ref, first_ref, slot_ref, next_ref,
                xs_ref, wg_hbm, wu_hbm, wd_hbm, out_ref,
                wg_f, wu_f, wd_f, wg_s, wu_s, wd_s, sems):
    i = pl.program_id(0)
    tm = xs_ref.shape[0]
    valid = tile_rows_ref[i]
    expert = tile_expert_ref[i]
    slot = slot_ref[i]

    def weight_copies(e, s):
        return (pltpu.make_async_copy(wg_hbm.at[e], wg_f.at[s], sems.at[s, 0]),
                pltpu.make_async_copy(wu_hbm.at[e], wu_f.at[s], sems.at[s, 1]),
                pltpu.make_async_copy(wd_hbm.at[e], wd_f.at[s], sems.at[s, 2]))

    @pl.when(i == 0)
    def _():
        for c in weight_copies(expert, slot):
            c.start()

    @pl.when(first_ref[i] == 1)
    def _():
        nxt = next_ref[i]

        @pl.when(nxt >= 0)
        def _():
            for c in weight_copies(nxt, 1 - slot):
                c.start()

        for c in weight_copies(expert, slot):
            c.wait()
        wg_s[...] = wg_f[slot].astype(BF16)
        wu_s[...] = wu_f[slot].astype(BF16)
        wd_s[...] = wd_f[slot].astype(BF16)

    @pl.when(valid > 0)
    def _():
        row = lax.broadcasted_iota(jnp.int32, (tm, PACKED), 0)
        words = jnp.where(row < valid, xs_ref[...], 0)
        lo, hi = _unpack_rows(words)
        lo = lo.astype(BF16)
        hi = hi.astype(BF16)
        gate = (jnp.dot(lo, wg_s[:PACKED, :], preferred_element_type=F32)
                + jnp.dot(hi, wg_s[PACKED:, :], preferred_element_type=F32))
        up = (jnp.dot(lo, wu_s[:PACKED, :], preferred_element_type=F32)
              + jnp.dot(hi, wu_s[PACKED:, :], preferred_element_type=F32))
        act = (gate * jax.nn.sigmoid(gate) * up).astype(BF16)
        out_ref[...] = _pack_rows(jnp.dot(act, wd_s[...], preferred_element_type=F32))

    @pl.when(valid <= 0)
    def _():
        out_ref[...] = jnp.zeros_like(out_ref)


def _moe_experts(plan, xs, w_gate, w_up, w_down, tm):
    nslot = xs.shape[0]
    any_spec = pl.BlockSpec(memory_space=pl.ANY)
    grid_spec = pltpu.PrefetchScalarGridSpec(
        num_scalar_prefetch=len(plan),
        grid=(nslot // tm,),
        in_specs=[_row_spec(tm, PACKED), any_spec, any_spec, any_spec],
        out_specs=_row_spec(tm, PACKED),
        scratch_shapes=[
            pltpu.VMEM((2, D_MODEL, EXPERT_FF), F32),
            pltpu.VMEM((2, D_MODEL, EXPERT_FF), F32),
            pltpu.VMEM((2, EXPERT_FF, D_MODEL), F32),
            pltpu.VMEM((D_MODEL, EXPERT_FF), BF16),
            pltpu.VMEM((D_MODEL, EXPERT_FF), BF16),
            pltpu.VMEM((EXPERT_FF, D_MODEL), BF16),
            pltpu.SemaphoreType.DMA((2, 3)),
        ],
    )
    return pl.pallas_call(
        _moe_kernel,
        grid_spec=grid_spec,
        out_shape=jax.ShapeDtypeStruct((nslot, PACKED), I32),
        compiler_params=_ARB,
        name="moe_experts",
    )(*plan, xs, w_gate, w_up, w_down)


def _dispatch_plan(route, counts, tm):
    t = route.shape[0]
    e_ids = route[:, 0:2].astype(I32)
    ranks = route[:, 4:6].astype(I32)
    counts = counts[0, :N_EXPERTS].astype(I32)
    padded = ((counts + tm - 1) // tm) * tm
    ends = jnp.cumsum(padded)
    starts = ends - padded
    experts = jnp.arange(N_EXPERTS, dtype=I32)
    slot = jnp.sum(jnp.where(e_ids[..., None] == experts, starts, 0), axis=-1) + ranks
    n_tiles = (2 * t + N_EXPERTS * tm) // tm
    tile_start = jnp.arange(n_tiles, dtype=I32) * tm
    total = ends[-1]
    owner = lambda s: jnp.sum((ends <= s[..., None]).astype(I32), axis=-1)
    tile_expert = jnp.where(tile_start < total, owner(tile_start), owner(total - 1))
    tile_expert = jnp.minimum(tile_expert, N_EXPERTS - 1)
    live_end = jnp.sum(jnp.where(tile_expert[:, None] == experts, starts + counts, 0), axis=-1)
    tile_rows = jnp.where(tile_start < total, jnp.clip(live_end - tile_start, 0, tm), 0)
    used = tile_start < total
    changed = jnp.concatenate([jnp.ones((1,), bool), tile_expert[1:] != tile_expert[:-1]])
    first = used & changed
    stage_slot = (jnp.cumsum(first.astype(I32)) - 1) % 2
    group_end = jnp.sum(jnp.where(tile_expert[:, None] == experts, ends, 0), axis=-1)
    next_expert = jnp.where(group_end < total, jnp.minimum(owner(group_end), N_EXPERTS - 1), -1)
    plan = (tile_expert, tile_rows, first, stage_slot, next_expert)
    return slot[:, 0], slot[:, 1], tuple(p.astype(I32) for p in plan)


def _final_kernel(x1_ref, y0_ref, y1_ref, route_ref, g_ref, out_ref):
    route = route_ref[...]
    w1 = route[:, 2:3]
    w2 = route[:, 3:4]
    a_lo, a_hi = _unpack_rows(y0_ref[...])
    b_lo, b_hi = _unpack_rows(y1_ref[...])
    moe = jnp.concatenate([w1 * a_lo + w2 * b_lo, w1 * a_hi + w2 * b_hi], axis=1)
    out_ref[...] = _rms(x1_ref[...] + moe, g_ref[...])


def _final(x1, ycat, route, norm_g, tm):
    t = x1.shape[0]
    nblk = t // tm
    return pl.pallas_call(
        _final_kernel,
        grid=(nblk,),
        in_specs=[
            _row_spec(tm, D_MODEL),
            pl.BlockSpec((tm, PACKED), lambda i: (i, 0)),
            pl.BlockSpec((tm, PACKED), lambda i: (i + nblk, 0)),
            _row_spec(tm, LANES),
            _const_spec((1, D_MODEL)),
        ],
        out_specs=_row_spec(tm, D_MODEL),
        out_shape=jax.ShapeDtypeStruct((t, D_MODEL), F32),
        compiler_params=_ARB,
        name="final_norm",
    )(x1, ycat, ycat, route, norm_g)


def kernel(x, norm_mix, w_in, conv_w, conv_b, w_conv_out, ssm_a_re, ssm_a_im, ssm_log_dt,
           ssm_b_re, ssm_b_im, ssm_c_re, ssm_c_im, ssm_d, w_glu_a, w_glu_b, gate_bias, w_o,
           norm_ffn, w_route_group, b_route_group, w_route_expert, b_route_expert,
           w_gate, w_up, w_down, norm_final):
    bsz, length, d = x.shape
    assert d == D_MODEL and norm_mix.shape[0] == 1
    t = bsz * length
    tm = TOKEN_TILE
    assert bsz == 1 and t % tm == 0
    xt = x.reshape(t, d)
    row = lambda a: a.reshape(1, -1).astype(F32)

    ma, gb, u = _mixer_front(
        xt, row(norm_mix[0]), w_in[0].astype(BF16), conv_w[0].reshape(3, CONV_WIDTH),
        row(conv_b[0]), w_conv_out[0].astype(BF16), gate_bias[0], tm)

    bre, bim, cw, a_re, a_im = _s5_weights(
        ssm_a_re[0], ssm_a_im[0], ssm_log_dt[0], ssm_b_re[0], ssm_b_im[0],
        ssm_c_re[0], ssm_c_im[0])
    yg = _s5_scan(u, bre, bim, cw, a_re, a_im, row(ssm_d[0]), tm)

    pad = LANES - N_EXPERTS - N_EXPERT_GROUPS
    w_route = jnp.concatenate(
        [w_route_expert[0], w_route_group[0], jnp.zeros((d, pad), F32)], axis=1)
    w_route_hi = w_route.astype(BF16)
    w_route_lo = (w_route - w_route_hi.astype(F32)).astype(BF16)
    b_route = jnp.concatenate(
        [b_route_expert[0], b_route_group[0], jnp.zeros((pad,), F32)]).reshape(1, LANES)
    x1, h2p, route, counts = _mixer_back(
        yg, ma, gb, xt, w_glu_a[0].astype(BF16), w_glu_b[0].astype(BF16), w_o[0].astype(BF16),
        row(norm_ffn[0]), jnp.stack([w_route_hi, w_route_lo]), b_route, tm)

    slot0, slot1, plan = _dispatch_plan(route, counts, MOE_TILE)
    nslot = 2 * t + N_EXPERTS * MOE_TILE
    xs = _sc_dispatch(h2p, slot0, slot1, nslot)
    ys = _moe_experts(plan, xs, w_gate[0], w_up[0], w_down[0], MOE_TILE)
    ycat = _sc_gather(ys, jnp.concatenate([slot0, slot1]))
    out = _final(x1, ycat, route, row(norm_final), tm)
    return out.reshape(bsz, length, d)
```

```python
import functools

import jax
import jax.numpy as jnp
from jax import lax
from jax.experimental import pallas as pl
from jax.experimental.pallas import tpu as pltpu
from jax.experimental.pallas import tpu_sc as plsc

F32 = jnp.float32
BF16 = jnp.bfloat16
I32 = jnp.int32

D_MODEL = 1024
CONV_WIDTH = 1024
SSM_WIDTH = 512
SSM_GROUP = 16
SSM_GROUPS = 32
SSM_STATE = 64
N_EXPERT_GROUPS = 4
EXPERTS_PER_GROUP = 8
N_EXPERTS = 32
EXPERT_FF = 512
EPS = 1e-6

LANES = 128
MXU_DIM = 256
TOKEN_TILE = 512
MOE_TILE = 512
SCAN_PITCH = 36
STATE_ROWS = 2 * SSM_GROUPS * SSM_STATE // LANES
HALF_ROWS = STATE_ROWS // 2
ROUTE_GROUP_LANE = N_EXPERTS
PACKED = D_MODEL // 2
HI_MASK = -65536
SC_CORES = 2
SC_SUBCORES = 16
SC_CHUNK = 128
VMEM_LIMIT = 56 * 1024 * 1024

_ARB = pltpu.CompilerParams(dimension_semantics=("arbitrary",), vmem_limit_bytes=VMEM_LIMIT)


def _const_spec(shape):
    nd = len(shape)
    return pl.BlockSpec(shape, lambda i, *_: (0,) * nd, pipeline_mode=pl.Buffered(1))


def _row_spec(tm, width):
    return pl.BlockSpec((tm, width), lambda i, *_: (i, 0))


def _rms(x, g):
    ms = jnp.mean(x * x, axis=-1, keepdims=True)
    return x * lax.rsqrt(ms + EPS) * g


def _pack_rows(v):
    bits = lax.bitcast_convert_type(v.astype(BF16).astype(F32), I32)
    lo = lax.shift_right_logical(bits[:, :PACKED], 16)
    hi = bits[:, PACKED:] & HI_MASK
    return lo | hi


def _unpack_rows(w):
    lo = lax.bitcast_convert_type(lax.shift_left(w, 16), F32)
    hi = lax.bitcast_convert_type(w & HI_MASK, F32)
    return lo, hi


def _front_kernel(x_ref, g_ref, win_ref, cw_ref, cb_ref, wco_ref, gbias_ref,
                  ma_ref, gb_ref, u_ref, carry_ref):
    tm = x_ref.shape[0]
    c0, c1, c2 = CONV_WIDTH, 2 * CONV_WIDTH, 3 * CONV_WIDTH
    c3 = c2 + SSM_WIDTH

    @pl.when(pl.program_id(0) == 0)
    def _():
        carry_ref[...] = jnp.zeros_like(carry_ref)

    h = _rms(x_ref[...], g_ref[...]).astype(BF16)

    def proj(lo, hi):
        return jnp.dot(h, win_ref[:, lo:hi], preferred_element_type=F32)

    v = proj(c1, c2) * proj(0, c0)
    row = lax.broadcasted_iota(jnp.int32, v.shape, 0)
    prev1 = carry_ref[7:8, :]
    prev2 = carry_ref[6:7, :]
    v1 = jnp.where(row == 0, prev1, pltpu.roll(v, 1, 0))
    v2 = jnp.where(row == 0, prev2, jnp.where(row == 1, prev1, pltpu.roll(v, 2, 0)))
    carry_ref[...] = v[tm - 8:, :]
    y = cw_ref[0:1, :] * v2 + cw_ref[1:2, :] * v1 + cw_ref[2:3, :] * v + cb_ref[...]
    z = (proj(c0, c1) * y).astype(BF16)
    ya = jnp.dot(z, wco_ref[...], preferred_element_type=F32)
    ga = jax.nn.sigmoid(proj(c3, c3 + D_MODEL) + gbias_ref[0:1, :])
    ma_ref[...] = (ga * ya).astype(BF16)
    gb = jax.nn.sigmoid(proj(c3 + D_MODEL, c3 + 2 * D_MODEL) + gbias_ref[1:2, :])
    gb_ref[...] = gb.astype(BF16)
    u_ref[...] = proj(c2, c3).astype(BF16)


def _mixer_front(x, norm_g, w_in, conv_w, conv_b, w_conv_out, gate_bias, tm):
    t = x.shape[0]
    in_cols = w_in.shape[1]
    return pl.pallas_call(
        _front_kernel,
        grid=(t // tm,),
        in_specs=[
            _row_spec(tm, D_MODEL),
            _const_spec((1, D_MODEL)),
            _const_spec((D_MODEL, in_cols)),
            _const_spec((3, CONV_WIDTH)),
            _const_spec((1, CONV_WIDTH)),
            _const_spec((CONV_WIDTH, D_MODEL)),
            _const_spec((2, D_MODEL)),
        ],
        out_specs=[_row_spec(tm, D_MODEL), _row_spec(tm, D_MODEL), _row_spec(tm, SSM_WIDTH)],
        out_shape=[
            jax.ShapeDtypeStruct((t, D_MODEL), BF16),
            jax.ShapeDtypeStruct((t, D_MODEL), BF16),
            jax.ShapeDtypeStruct((t, SSM_WIDTH), BF16),
        ],
        scratch_shapes=[pltpu.VMEM((8, CONV_WIDTH), F32)],
        compiler_params=_ARB,
        name="mixer_front",
    )(x, norm_g, w_in, conv_w, conv_b, w_conv_out, gate_bias)


def _ssm_kernel(u_ref, bre_ref, bim_ref, cw_ref, are_ref, aim_ref, d_ref,
                y_ref, r_ref, state_ref):
    tm = u_ref.shape[0]
    tiles_per_half = MXU_DIM * 4 // LANES

    @pl.when(pl.program_id(0) == 0)
    def _():
        state_ref[...] = jnp.zeros_like(state_ref)

    u = u_ref[...]
    for k in range(2):
        uk = u[:, k * MXU_DIM:(k + 1) * MXU_DIM]
        re = jnp.dot(uk, bre_ref[k], preferred_element_type=F32)
        im = jnp.dot(uk, bim_ref[k], preferred_element_type=F32)
        for jj in range(tiles_per_half):
            j = k * tiles_per_half + jj
            sl = slice(jj * LANES, (jj + 1) * LANES)
            r_ref[pl.ds(j, tm, stride=SCAN_PITCH), :] = re[:, sl]
            r_ref[pl.ds(HALF_ROWS + j, tm, stride=SCAN_PITCH), :] = im[:, sl]

    a_re = are_ref[...]
    a_im = aim_ref[...]

    def step(t, carry):
        s_re, s_im = carry
        base = t * SCAN_PITCH
        b_re = r_ref[pl.ds(base, HALF_ROWS), :]
        b_im = r_ref[pl.ds(base + HALF_ROWS, HALF_ROWS), :]
        n_re = a_re * s_re - a_im * s_im + b_re
        n_im = a_re * s_im + a_im * s_re + b_im
        r_ref[pl.ds(base, HALF_ROWS), :] = n_re
        r_ref[pl.ds(base + HALF_ROWS, HALF_ROWS), :] = n_im
        return n_re, n_im

    s_re, s_im = lax.fori_loop(
        0, tm, step, (state_ref[0:HALF_ROWS, :], state_ref[HALF_ROWS:, :]), unroll=8)
    state_ref[0:HALF_ROWS, :] = s_re
    state_ref[HALF_ROWS:, :] = s_im

    ys = []
    for k in range(2):
        cols = []
        for half in range(2):
            for jj in range(tiles_per_half):
                j = half * HALF_ROWS + k * tiles_per_half + jj
                cols.append(r_ref[pl.ds(j, tm, stride=SCAN_PITCH), :])
        s = jnp.concatenate(cols, axis=1).astype(BF16)
        ys.append(jnp.dot(s, cw_ref[k], preferred_element_type=F32))
    y = jnp.concatenate(ys, axis=1) + d_ref[...] * u.astype(F32)
    y_ref[...] = jax.nn.gelu(y).astype(BF16)


def _s5_scan(u, bre, bim, cw, a_re, a_im, d_skip, tm):
    t = u.shape[0]
    return pl.pallas_call(
        _ssm_kernel,
        grid=(t // tm,),
        in_specs=[
            _row_spec(tm, SSM_WIDTH),
            _const_spec(bre.shape),
            _const_spec(bim.shape),
            _const_spec(cw.shape),
            _const_spec(a_re.shape),
            _const_spec(a_im.shape),
            _const_spec((1, SSM_WIDTH)),
        ],
        out_specs=_row_spec(tm, SSM_WIDTH),
        out_shape=jax.ShapeDtypeStruct((t, SSM_WIDTH), BF16),
        scratch_shapes=[
            pltpu.VMEM((tm * SCAN_PITCH, LANES), F32),
            pltpu.VMEM((STATE_ROWS, LANES), F32),
        ],
        compiler_params=_ARB,
        name="s5_scan",
    )(u, bre, bim, cw, a_re, a_im, d_skip)


def _s5_weights(a_re, a_im, log_dt, b_re, b_im, c_re, c_im):
    g, n, h = SSM_GROUPS, SSM_STATE, SSM_GROUP
    gpt = MXU_DIM // h
    lam = lax.complex(a_re.astype(F32), a_im.astype(F32))
    dt = jnp.exp(log_dt.astype(F32))[:, None]
    a_bar = jnp.exp(lam * dt)
    b_bar = ((a_bar - 1.0) / lam)[..., None] * lax.complex(b_re.astype(F32), b_im.astype(F32))
    eye = jnp.eye(gpt, dtype=F32)

    def blk_in(w):
        w = w.reshape(g // gpt, gpt, n, h).transpose(0, 1, 3, 2)
        return jnp.einsum("kghn,gf->kghfn", w, eye).reshape(g // gpt, gpt * h, gpt * n)

    def blk_out(c):
        c = c.astype(F32).reshape(g // gpt, gpt, h, n).transpose(0, 1, 3, 2)
        return jnp.einsum("kgnh,gf->kgnfh", c, eye).reshape(g // gpt, gpt * n, gpt * h)

    bre = blk_in(jnp.real(b_bar)).astype(BF16)
    bim = blk_in(jnp.imag(b_bar)).astype(BF16)
    cw = jnp.concatenate([blk_out(c_re), -blk_out(c_im)], axis=1).astype(BF16)
    tile = (HALF_ROWS, LANES)
    return bre, bim, cw, jnp.real(a_bar).reshape(tile), jnp.imag(a_bar).reshape(tile)


def _back_kernel(yg_ref, ma_ref, gb_ref, x_ref, wa_ref, wb_ref, wo_ref, g2_ref, wr_ref, br_ref,
                 x1_ref, h2p_ref, route_ref, counts_ref):
    tm = x_ref.shape[0]

    @pl.when(pl.program_id(0) == 0)
    def _():
        counts_ref[...] = jnp.zeros_like(counts_ref)

    yg = yg_ref[...]
    yb = (jnp.dot(yg, wa_ref[...], preferred_element_type=F32)
          * jax.nn.sigmoid(jnp.dot(yg, wb_ref[...], preferred_element_type=F32)))
    mixed = (ma_ref[...].astype(F32) + gb_ref[...].astype(F32) * yb).astype(BF16)
    x1 = x_ref[...] + jnp.dot(mixed, wo_ref[...], preferred_element_type=F32)
    x1_ref[...] = x1
    h2 = _rms(x1, g2_ref[...])
    h2p_ref[...] = _pack_rows(h2)

    h2_hi = h2.astype(BF16)
    h2_lo = (h2 - h2_hi.astype(F32)).astype(BF16)
    logits = (jnp.dot(h2_hi, wr_ref[0], preferred_element_type=F32)
              + (jnp.dot(h2_hi, wr_ref[1], preferred_element_type=F32)
                 + jnp.dot(h2_lo, wr_ref[0], preferred_element_type=F32))) + br_ref[...]
    lane = lax.broadcasted_iota(jnp.int32, logits.shape, 1)
    lane_f = lane.astype(F32)
    neg = jnp.float32(-jnp.inf)
    big = jnp.float32(LANES)
    is_grp = (lane >= ROUTE_GROUP_LANE) & (lane < ROUTE_GROUP_LANE + N_EXPERT_GROUPS)
    gl = jnp.where(is_grp, logits, neg)
    gmax = jnp.max(gl, axis=1, keepdims=True)
    gidx = jnp.min(jnp.where(gl == gmax, lane_f - ROUTE_GROUP_LANE, big), axis=1, keepdims=True)
    pg_top = 1.0 / jnp.sum(jnp.exp(gl - gmax), axis=1, keepdims=True)
    lane_grp = (lane // EXPERTS_PER_GROUP).astype(F32)
    el = jnp.where((lane < N_EXPERTS) & (lane_grp == gidx), logits, neg)
    l1 = jnp.max(el, axis=1, keepdims=True)
    i1 = jnp.min(jnp.where(el == l1, lane_f, big), axis=1, keepdims=True)
    el2 = jnp.where(lane_f == i1, neg, el)
    l2 = jnp.max(el2, axis=1, keepdims=True)
    i2 = jnp.min(jnp.where(el2 == l2, lane_f, big), axis=1, keepdims=True)
    r = jnp.exp(l2 - l1)
    w1 = pg_top / (1.0 + r)
    w2 = pg_top * r / (1.0 + r)

    oh1 = (lane_f == i1).astype(F32)
    oh2 = (lane_f == i2).astype(F32)
    picked = oh1 + oh2
    rr = lax.broadcasted_iota(jnp.int32, (tm, tm), 0)
    cc = lax.broadcasted_iota(jnp.int32, (tm, tm), 1)
    before = (cc < rr).astype(BF16)
    prior = jnp.dot(before, picked.astype(BF16), preferred_element_type=F32) + counts_ref[...]
    rank1 = jnp.sum(oh1 * prior, axis=1, keepdims=True)
    rank2 = jnp.sum(oh2 * prior, axis=1, keepdims=True)
    counts_ref[...] += jnp.sum(picked, axis=0, keepdims=True)

    route_ref[...] = jnp.where(
        lane == 0, i1, jnp.where(lane == 1, i2, jnp.where(lane == 2, w1, jnp.where(
            lane == 3, w2, jnp.where(lane == 4, rank1, jnp.where(lane == 5, rank2, 0.0))))))


def _mixer_back(yg, ma, gb, x, w_glu_a, w_glu_b, w_o, norm_g, w_route, b_route, tm):
    t = x.shape[0]
    return pl.pallas_call(
        _back_kernel,
        grid=(t // tm,),
        in_specs=[
            _row_spec(tm, SSM_WIDTH),
            _row_spec(tm, D_MODEL),
            _row_spec(tm, D_MODEL),
            _row_spec(tm, D_MODEL),
            _const_spec((SSM_WIDTH, D_MODEL)),
            _const_spec((SSM_WIDTH, D_MODEL)),
            _const_spec((D_MODEL, D_MODEL)),
            _const_spec((1, D_MODEL)),
            _const_spec((2, D_MODEL, LANES)),
            _const_spec((1, LANES)),
        ],
        out_specs=[_row_spec(tm, D_MODEL), _row_spec(tm, PACKED), _row_spec(tm, LANES),
                   pl.BlockSpec((1, LANES), lambda i: (0, 0))],
        out_shape=[
            jax.ShapeDtypeStruct((t, D_MODEL), F32),
            jax.ShapeDtypeStruct((t, PACKED), I32),
            jax.ShapeDtypeStruct((t, LANES), F32),
            jax.ShapeDtypeStruct((1, LANES), F32),
        ],
        compiler_params=_ARB,
        name="mixer_back",
    )(yg, ma, gb, x, w_glu_a, w_glu_b, w_o, norm_g, w_route, b_route)


_SC_MESH = dict(core_axis_name="c", subcore_axis_name="s")


def _sc_worker():
    return lax.axis_index("s") * SC_CORES + lax.axis_index("c")


def _sc_dispatch(rows, slot0, slot1, nslot):
    t, width = rows.shape
    per_w = t // (SC_CORES * SC_SUBCORES)
    assert per_w % SC_CHUNK == 0

    @functools.partial(
        pl.kernel,
        out_type=jax.ShapeDtypeStruct((nslot, width), rows.dtype),
        mesh=plsc.VectorSubcoreMesh(**_SC_MESH),
        scratch_types=[
            pltpu.VMEM((1, SC_CHUNK), I32),
            pltpu.VMEM((1, SC_CHUNK), I32),
            pltpu.VMEM((SC_CHUNK, width), rows.dtype),
        ],
        name="moe_dispatch",
    )
    def k(rows_hbm, s0_hbm, s1_hbm, out_hbm, i0_v, i1_v, rows_v):
        base = _sc_worker() * per_w

        @pl.loop(0, per_w // SC_CHUNK)
        def _(c):
            off = base + c * SC_CHUNK
            pltpu.sync_copy(s0_hbm.at[:, pl.ds(off, SC_CHUNK)], i0_v)
            pltpu.sync_copy(s1_hbm.at[:, pl.ds(off, SC_CHUNK)], i1_v)
            pltpu.sync_copy(rows_hbm.at[pl.ds(off, SC_CHUNK)], rows_v)
            pltpu.sync_copy(rows_v, out_hbm.at[i0_v.at[0]])
            pltpu.sync_copy(rows_v, out_hbm.at[i1_v.at[0]])

    return k(rows, slot0.reshape(1, t), slot1.reshape(1, t))


def _sc_gather(table, idx):
    n = idx.shape[0]
    width = table.shape[1]
    per_w = n // (SC_CORES * SC_SUBCORES)
    assert per_w % SC_CHUNK == 0

    @functools.partial(
        pl.kernel,
        out_type=jax.ShapeDtypeStruct((n, width), table.dtype),
        mesh=plsc.VectorSubcoreMesh(**_SC_MESH),
        scratch_types=[
            pltpu.VMEM((1, SC_CHUNK), I32),
            pltpu.VMEM((SC_CHUNK, width), table.dtype),
        ],
        name="moe_combine_gather",
    )
    def k(table_hbm, idx_hbm, out_hbm, idx_v, rows_v):
        base = _sc_worker() * per_w

        @pl.loop(0, per_w // SC_CHUNK)
        def _(c):
            off = base + c * SC_CHUNK
            pltpu.sync_copy(idx_hbm.at[:, pl.ds(off, SC_CHUNK)], idx_v)
            pltpu.sync_copy(table_hbm.at[idx_v.at[0]], rows_v)
            pltpu.sync_copy(rows_v, out_hbm.at[pl.ds(off, SC_CHUNK)])

    return k(table, idx.reshape(1, n))


def _moe_kernel(first_ref, last_ref, count_ref, total_ref,
                xs_hbm, wg_ref, wu_ref, wd_ref, out_hbm,
                xbuf, obuf, wg_s, wu_s, wd_s, xsem, osem):
    e = pl.program_id(0)
    ch = MOE_TILE
    half = ch // 2
    first, last, count, total = first_ref[e], last_ref[e], count_ref[e], total_ref[0]

    def x_copy(c, s):
        rows = pl.ds(pl.multiple_of(c * ch, ch), ch)
        return pltpu.make_async_copy(xs_hbm.at[rows], xbuf.at[s], xsem.at[s])

    def o_copy(c, s):
        rows = pl.ds(pl.multiple_of(c * ch, ch), ch)
        return pltpu.make_async_copy(obuf.at[s], out_hbm.at[rows], osem.at[s])

    @pl.when(e == 0)
    def _():
        x_copy(0, 0).start()

    @pl.when(last > first)
    def _():
        wg_s[...] = wg_ref[0].astype(BF16)
        wu_s[...] = wu_ref[0].astype(BF16)
        wd_s[...] = wd_ref[0].astype(BF16)

    def experts_rows(s, rows, valid):
        row = lax.broadcasted_iota(jnp.int32, (rows, PACKED), 0)
        words = jnp.where(row < valid, xbuf[s, 0:rows, :], 0)
        lo, hi = _unpack_rows(words)
        lo = lo.astype(BF16)
        hi = hi.astype(BF16)
        gate = (jnp.dot(lo, wg_s[:PACKED, :], preferred_element_type=F32)
                + jnp.dot(hi, wg_s[PACKED:, :], preferred_element_type=F32))
        up = (jnp.dot(lo, wu_s[:PACKED, :], preferred_element_type=F32)
              + jnp.dot(hi, wu_s[PACKED:, :], preferred_element_type=F32))
        act = (gate * jax.nn.sigmoid(gate) * up).astype(BF16)
        obuf[s, 0:rows, :] = _pack_rows(jnp.dot(act, wd_s[...], preferred_element_type=F32))

    def chunk(c, carry):
        s = c % 2
        x_copy(c, s).wait()

        @pl.when(c + 1 < total)
        def _():
            x_copy(c + 1, 1 - s).start()

        @pl.when(c >= 2)
        def _():
            o_copy(c - 2, s).wait()

        valid = count - (c - first) * ch

        @pl.when(valid > half)
        def _():
            experts_rows(s, ch, valid)

        @pl.when(valid <= half)
        def _():
            experts_rows(s, half, valid)
            obuf[s, half:, :] = jnp.zeros((ch - half, PACKED), I32)

        o_copy(c, s).start()
        return carry

    lax.fori_loop(first, last, chunk, 0)

    @pl.when(e == pl.num_programs(0) - 1)
    def _():
        @pl.when(total >= 2)
        def _():
            o_copy(total - 2, total % 2).wait()

        o_copy(total - 1, (total - 1) % 2).wait()


def _moe_experts(plan, xs, w_gate, w_up, w_down):
    nslot = xs.shape[0]
    any_spec = pl.BlockSpec(memory_space=pl.ANY)
    grid_spec = pltpu.PrefetchScalarGridSpec(
        num_scalar_prefetch=len(plan),
        grid=(N_EXPERTS,),
        in_specs=[
            any_spec,
            pl.BlockSpec((1, D_MODEL, EXPERT_FF), lambda e, *_: (e, 0, 0)),
            pl.BlockSpec((1, D_MODEL, EXPERT_FF), lambda e, *_: (e, 0, 0)),
            pl.BlockSpec((1, EXPERT_FF, D_MODEL), lambda e, *_: (e, 0, 0)),
        ],
        out_specs=any_spec,
        scratch_shapes=[
            pltpu.VMEM((2, MOE_TILE, PACKED), I32),
            pltpu.VMEM((2, MOE_TILE, PACKED), I32),
            pltpu.VMEM((D_MODEL, EXPERT_FF), BF16),
            pltpu.VMEM((D_MODEL, EXPERT_FF), BF16),
            pltpu.VMEM((EXPERT_FF, D_MODEL), BF16),
            pltpu.SemaphoreType.DMA((2,)),
            pltpu.SemaphoreType.DMA((2,)),
        ],
    )
    return pl.pallas_call(
        _moe_kernel,
        grid_spec=grid_spec,
        out_shape=jax.ShapeDtypeStruct((nslot, PACKED), I32),
        compiler_params=_ARB,
        name="moe_experts",
    )(*plan, xs, w_gate, w_up, w_down)


def _dispatch_plan(route, counts, tm):
    t = route.shape[0]
    e_ids = route[:, 0:2].astype(I32)
    ranks = route[:, 4:6].astype(I32)
    counts = counts[0, :N_EXPERTS].astype(I32)
    padded = ((counts + tm - 1) // tm) * tm
    ends = jnp.cumsum(padded)
    starts = ends - padded
    experts = jnp.arange(N_EXPERTS, dtype=I32)
    slot = jnp.sum(jnp.where(e_ids[..., None] == experts, starts, 0), axis=-1) + ranks
    plan = (starts // tm, ends // tm, counts, ends[-1:] // tm)
    return slot[:, 0], slot[:, 1], tuple(p.astype(I32) for p in plan)


def _final_kernel(x1_ref, y0_ref, y1_ref, route_ref, g_ref, out_ref):
    route = route_ref[...]
    w1 = route[:, 2:3]
    w2 = route[:, 3:4]
    a_lo, a_hi = _unpack_rows(y0_ref[...])
    b_lo, b_hi = _unpack_rows(y1_ref[...])
    moe = jnp.concatenate([w1 * a_lo + w2 * b_lo, w1 * a_hi + w2 * b_hi], axis=1)
    out_ref[...] = _rms(x1_ref[...] + moe, g_ref[...])


def _final(x1, ycat, route, norm_g, tm):
    t = x1.shape[0]
    nblk = t // tm
    return pl.pallas_call(
        _final_kernel,
        grid=(nblk,),
        in_specs=[
            _row_spec(tm, D_MODEL),
            pl.BlockSpec((tm, PACKED), lambda i: (i, 0)),
            pl.BlockSpec((tm, PACKED), lambda i: (i + nblk, 0)),
            _row_spec(tm, LANES),
            _const_spec((1, D_MODEL)),
        ],
        out_specs=_row_spec(tm, D_MODEL),
        out_shape=jax.ShapeDtypeStruct((t, D_MODEL), F32),
        compiler_params=_ARB,
        name="final_norm",
    )(x1, ycat, ycat, route, norm_g)


def kernel(x, norm_mix, w_in, conv_w, conv_b, w_conv_out, ssm_a_re, ssm_a_im, ssm_log_dt,
           ssm_b_re, ssm_b_im, ssm_c_re, ssm_c_im, ssm_d, w_glu_a, w_glu_b, gate_bias, w_o,
           norm_ffn, w_route_group, b_route_group, w_route_expert, b_route_expert,
           w_gate, w_up, w_down, norm_final):
    bsz, length, d = x.shape
    assert d == D_MODEL and norm_mix.shape[0] == 1
    t = bsz * length
    tm = TOKEN_TILE
    assert bsz == 1 and t % tm == 0
    xt = x.reshape(t, d)
    row = lambda a: a.reshape(1, -1).astype(F32)

    ma, gb, u = _mixer_front(
        xt, row(norm_mix[0]), w_in[0].astype(BF16), conv_w[0].reshape(3, CONV_WIDTH),
        row(conv_b[0]), w_conv_out[0].astype(BF16), gate_bias[0], tm)

    bre, bim, cw, a_re, a_im = _s5_weights(
        ssm_a_re[0], ssm_a_im[0], ssm_log_dt[0], ssm_b_re[0], ssm_b_im[0],
        ssm_c_re[0], ssm_c_im[0])
    yg = _s5_scan(u, bre, bim, cw, a_re, a_im, row(ssm_d[0]), tm)

    pad = LANES - N_EXPERTS - N_EXPERT_GROUPS
    w_route = jnp.concatenate(
        [w_route_expert[0], w_route_group[0], jnp.zeros((d, pad), F32)], axis=1)
    w_route_hi = w_route.astype(BF16)
    w_route_lo = (w_route - w_route_hi.astype(F32)).astype(BF16)
    b_route = jnp.concatenate(
        [b_route_expert[0], b_route_group[0], jnp.zeros((pad,), F32)]).reshape(1, LANES)
    x1, h2p, route, counts = _mixer_back(
        yg, ma, gb, xt, w_glu_a[0].astype(BF16), w_glu_b[0].astype(BF16), w_o[0].astype(BF16),
        row(norm_ffn[0]), jnp.stack([w_route_hi, w_route_lo]), b_route, tm)

    slot0, slot1, plan = _dispatch_plan(route, counts, MOE_TILE)
    nslot = 2 * t + N_EXPERTS * MOE_TILE
    xs = _sc_dispatch(h2p, slot0, slot1, nslot)
    ys = _moe_experts(plan, xs, w_gate[0], w_up[0], w_down[0])
    ycat = _sc_gather(ys, jnp.concatenate([slot0, slot1]))
    out = _final(x1, ycat, route, row(norm_final), tm)
    return out.reshape(bsz, length, d)
```

```python
import functools

import jax
import jax.numpy as jnp
from jax import lax
from jax.experimental import pallas as pl
from jax.experimental.pallas import tpu as pltpu
from jax.experimental.pallas import tpu_sc as plsc

F32 = jnp.float32
BF16 = jnp.bfloat16
I32 = jnp.int32

D_MODEL = 1024
CONV_WIDTH = 1024
SSM_WIDTH = 512
SSM_GROUP = 16
SSM_GROUPS = 32
SSM_STATE = 64
N_EXPERT_GROUPS = 4
EXPERTS_PER_GROUP = 8
N_EXPERTS = 32
EXPERT_FF = 512
EPS = 1e-6

LANES = 128
MXU_DIM = 256
TOKEN_TILE = 512
MOE_TILE = 512
SCAN_PITCH = 36
STATE_ROWS = 2 * SSM_GROUPS * SSM_STATE // LANES
HALF_ROWS = STATE_ROWS // 2
ROUTE_GROUP_LANE = N_EXPERTS
PACKED = D_MODEL // 2
HI_MASK = -65536
SC_CORES = 2
SC_SUBCORES = 16
SC_CHUNK = 128
VMEM_LIMIT = 56 * 1024 * 1024

_ARB = pltpu.CompilerParams(dimension_semantics=("arbitrary",), vmem_limit_bytes=VMEM_LIMIT)


def _const_spec(shape):
    nd = len(shape)
    return pl.BlockSpec(shape, lambda i, *_: (0,) * nd, pipeline_mode=pl.Buffered(1))


def _row_spec(tm, width):
    return pl.BlockSpec((tm, width), lambda i, *_: (i, 0))


def _rms(x, g):
    ms = jnp.mean(x * x, axis=-1, keepdims=True)
    return x * lax.rsqrt(ms + EPS) * g


def _pack_rows(v):
    bits = lax.bitcast_convert_type(v.astype(BF16).astype(F32), I32)
    lo = lax.shift_right_logical(bits[:, :PACKED], 16)
    hi = bits[:, PACKED:] & HI_MASK
    return lo | hi


def _unpack_rows(w):
    lo = lax.bitcast_convert_type(lax.shift_left(w, 16), F32)
    hi = lax.bitcast_convert_type(w & HI_MASK, F32)
    return lo, hi


def _front_kernel(x_ref, g_ref, win_ref, cw_ref, cb_ref, wco_ref, gbias_ref,
                  ma_ref, gb_ref, u_ref, carry_ref):
    tm = x_ref.shape[0]
    c0, c1, c2 = CONV_WIDTH, 2 * CONV_WIDTH, 3 * CONV_WIDTH
    c3 = c2 + SSM_WIDTH

    @pl.when(pl.program_id(0) == 0)
    def _():
        carry_ref[...] = jnp.zeros_like(carry_ref)

    h = _rms(x_ref[...], g_ref[...]).astype(BF16)

    def proj(lo, hi):
        return jnp.dot(h, win_ref[:, lo:hi], preferred_element_type=F32)

    v = proj(c1, c2) * proj(0, c0)
    row = lax.broadcasted_iota(jnp.int32, v.shape, 0)
    prev1 = carry_ref[7:8, :]
    prev2 = carry_ref[6:7, :]
    v1 = jnp.where(row == 0, prev1, pltpu.roll(v, 1, 0))
    v2 = jnp.where(row == 0, prev2, jnp.where(row == 1, prev1, pltpu.roll(v, 2, 0)))
    carry_ref[...] = v[tm - 8:, :]
    y = cw_ref[0:1, :] * v2 + cw_ref[1:2, :] * v1 + cw_ref[2:3, :] * v + cb_ref[...]
    z = (proj(c0, c1) * y).astype(BF16)
    ya = jnp.dot(z, wco_ref[...], preferred_element_type=F32)
    ga = jax.nn.sigmoid(proj(c3, c3 + D_MODEL) + gbias_ref[0:1, :])
    ma_ref[...] = (ga * ya).astype(BF16)
    gb = jax.nn.sigmoid(proj(c3 + D_MODEL, c3 + 2 * D_MODEL) + gbias_ref[1:2, :])
    gb_ref[...] = gb.astype(BF16)
    u_ref[...] = proj(c2, c3).astype(BF16)


def _mixer_front(x, norm_g, w_in, conv_w, conv_b, w_conv_out, gate_bias, tm):
    t = x.shape[0]
    in_cols = w_in.shape[1]
    return pl.pallas_call(
        _front_kernel,
        grid=(t // tm,),
        in_specs=[
            _row_spec(tm, D_MODEL),
            _const_spec((1, D_MODEL)),
            _const_spec((D_MODEL, in_cols)),
            _const_spec((3, CONV_WIDTH)),
            _const_spec((1, CONV_WIDTH)),
            _const_spec((CONV_WIDTH, D_MODEL)),
            _const_spec((2, D_MODEL)),
        ],
        out_specs=[_row_spec(tm, D_MODEL), _row_spec(tm, D_MODEL), _row_spec(tm, SSM_WIDTH)],
        out_shape=[
            jax.ShapeDtypeStruct((t, D_MODEL), BF16),
            jax.ShapeDtypeStruct((t, D_MODEL), BF16),
            jax.ShapeDtypeStruct((t, SSM_WIDTH), BF16),
        ],
        scratch_shapes=[pltpu.VMEM((8, CONV_WIDTH), F32)],
        compiler_params=_ARB,
        name="mixer_front",
    )(x, norm_g, w_in, conv_w, conv_b, w_conv_out, gate_bias)


def _ssm_kernel(u_ref, bre_ref, bim_ref, cw_ref, are_ref, aim_ref, d_ref,
                y_ref, r_ref, state_ref):
    tm = u_ref.shape[0]
    tiles_per_half = MXU_DIM * 4 // LANES

    @pl.when(pl.program_id(0) == 0)
    def _():
        state_ref[...] = jnp.zeros_like(state_ref)

    u = u_ref[...]
    for k in range(2):
        uk = u[:, k * MXU_DIM:(k + 1) * MXU_DIM]
        re = jnp.dot(uk, bre_ref[k], preferred_element_type=F32)
        im = jnp.dot(uk, bim_ref[k], preferred_element_type=F32)
        for jj in range(tiles_per_half):
            j = k * tiles_per_half + jj
            sl = slice(jj * LANES, (jj + 1) * LANES)
            r_ref[pl.ds(j, tm, stride=SCAN_PITCH), :] = re[:, sl]
            r_ref[pl.ds(HALF_ROWS + j, tm, stride=SCAN_PITCH), :] = im[:, sl]

    a_re = are_ref[...]
    a_im = aim_ref[...]

    def step(t, carry):
        s_re, s_im = carry
        base = t * SCAN_PITCH
        b_re = r_ref[pl.ds(base, HALF_ROWS), :]
        b_im = r_ref[pl.ds(base + HALF_ROWS, HALF_ROWS), :]
        n_re = a_re * s_re - a_im * s_im + b_re
        n_im = a_re * s_im + a_im * s_re + b_im
        r_ref[pl.ds(base, HALF_ROWS), :] = n_re
        r_ref[pl.ds(base + HALF_ROWS, HALF_ROWS), :] = n_im
        return n_re, n_im

    s_re, s_im = lax.fori_loop(
        0, tm, step, (state_ref[0:HALF_ROWS, :], state_ref[HALF_ROWS:, :]), unroll=8)
    state_ref[0:HALF_ROWS, :] = s_re
    state_ref[HALF_ROWS:, :] = s_im

    ys = []
    for k in range(2):
        cols = []
        for half in range(2):
            for jj in range(tiles_per_half):
                j = half * HALF_ROWS + k * tiles_per_half + jj
                cols.append(r_ref[pl.ds(j, tm, stride=SCAN_PITCH), :])
        s = jnp.concatenate(cols, axis=1).astype(BF16)
        ys.append(jnp.dot(s, cw_ref[k], preferred_element_type=F32))
    y = jnp.concatenate(ys, axis=1) + d_ref[...] * u.astype(F32)
    y_ref[...] = jax.nn.gelu(y).astype(BF16)


def _s5_scan(u, bre, bim, cw, a_re, a_im, d_skip, tm):
    t = u.shape[0]
    return pl.pallas_call(
        _ssm_kernel,
        grid=(t // tm,),
        in_specs=[
            _row_spec(tm, SSM_WIDTH),
            _const_spec(bre.shape),
            _const_spec(bim.shape),
            _const_spec(cw.shape),
            _const_spec(a_re.shape),
            _const_spec(a_im.shape),
            _const_spec((1, SSM_WIDTH)),
        ],
        out_specs=_row_spec(tm, SSM_WIDTH),
        out_shape=jax.ShapeDtypeStruct((t, SSM_WIDTH), BF16),
        scratch_shapes=[
            pltpu.VMEM((tm * SCAN_PITCH, LANES), F32),
            pltpu.VMEM((STATE_ROWS, LANES), F32),
        ],
        compiler_params=_ARB,
        name="s5_scan",
    )(u, bre, bim, cw, a_re, a_im, d_skip)


def _s5_weights(a_re, a_im, log_dt, b_re, b_im, c_re, c_im):
    g, n, h = SSM_GROUPS, SSM_STATE, SSM_GROUP
    gpt = MXU_DIM // h
    lam = lax.complex(a_re.astype(F32), a_im.astype(F32))
    dt = jnp.exp(log_dt.astype(F32))[:, None]
    a_bar = jnp.exp(lam * dt)
    b_bar = ((a_bar - 1.0) / lam)[..., None] * lax.complex(b_re.astype(F32), b_im.astype(F32))
    eye = jnp.eye(gpt, dtype=F32)

    def blk_in(w):
        w = w.reshape(g // gpt, gpt, n, h).transpose(0, 1, 3, 2)
        return jnp.einsum("kghn,gf->kghfn", w, eye).reshape(g // gpt, gpt * h, gpt * n)

    def blk_out(c):
        c = c.astype(F32).reshape(g // gpt, gpt, h, n).transpose(0, 1, 3, 2)
        return jnp.einsum("kgnh,gf->kgnfh", c, eye).reshape(g // gpt, gpt * n, gpt * h)

    bre = blk_in(jnp.real(b_bar)).astype(BF16)
    bim = blk_in(jnp.imag(b_bar)).astype(BF16)
    cw = jnp.concatenate([blk_out(c_re), -blk_out(c_im)], axis=1).astype(BF16)
    tile = (HALF_ROWS, LANES)
    return bre, bim, cw, jnp.real(a_bar).reshape(tile), jnp.imag(a_bar).reshape(tile)


def _back_kernel(yg_ref, ma_ref, gb_ref, x_ref, wa_ref, wb_ref, wo_ref, g2_ref, wr_ref, br_ref,
                 x1_ref, h2p_ref, route_ref, counts_ref):
    tm = x_ref.shape[0]

    @pl.when(pl.program_id(0) == 0)
    def _():
        counts_ref[...] = jnp.zeros_like(counts_ref)

    yg = yg_ref[...]
    yb = (jnp.dot(yg, wa_ref[...], preferred_element_type=F32)
          * jax.nn.sigmoid(jnp.dot(yg, wb_ref[...], preferred_element_type=F32)))
    mixed = (ma_ref[...].astype(F32) + gb_ref[...].astype(F32) * yb).astype(BF16)
    x1 = x_ref[...] + jnp.dot(mixed, wo_ref[...], preferred_element_type=F32)
    x1_ref[...] = x1
    h2 = _rms(x1, g2_ref[...])
    h2p_ref[...] = _pack_rows(h2)

    h2_hi = h2.astype(BF16)
    h2_lo = (h2 - h2_hi.astype(F32)).astype(BF16)
    both = jnp.dot(h2_hi, wr_ref[...], preferred_element_type=F32)
    logits = (both[:, :LANES]
              + (both[:, LANES:]
                 + jnp.dot(h2_lo, wr_ref[:, :LANES], preferred_element_type=F32))) + br_ref[...]
    lane = lax.broadcasted_iota(jnp.int32, logits.shape, 1)
    lane_f = lane.astype(F32)
    neg = jnp.float32(-jnp.inf)
    big = jnp.float32(LANES)
    is_grp = (lane >= ROUTE_GROUP_LANE) & (lane < ROUTE_GROUP_LANE + N_EXPERT_GROUPS)
    gl = jnp.where(is_grp, logits, neg)
    gmax = jnp.max(gl, axis=1, keepdims=True)
    gidx = jnp.min(jnp.where(gl == gmax, lane_f - ROUTE_GROUP_LANE, big), axis=1, keepdims=True)
    pg_top = 1.0 / jnp.sum(jnp.exp(gl - gmax), axis=1, keepdims=True)
    lane_grp = (lane // EXPERTS_PER_GROUP).astype(F32)
    el = jnp.where((lane < N_EXPERTS) & (lane_grp == gidx), logits, neg)
    l1 = jnp.max(el, axis=1, keepdims=True)
    i1 = jnp.min(jnp.where(el == l1, lane_f, big), axis=1, keepdims=True)
    el2 = jnp.where(lane_f == i1, neg, el)
    l2 = jnp.max(el2, axis=1, keepdims=True)
    i2 = jnp.min(jnp.where(el2 == l2, lane_f, big), axis=1, keepdims=True)
    r = jnp.exp(l2 - l1)
    w1 = pg_top / (1.0 + r)
    w2 = pg_top * r / (1.0 + r)

    oh1 = (lane_f == i1).astype(F32)
    oh2 = (lane_f == i2).astype(F32)
    picked = oh1 + oh2
    rr = lax.broadcasted_iota(jnp.int32, (tm, tm), 0)
    cc = lax.broadcasted_iota(jnp.int32, (tm, tm), 1)
    before = (cc < rr).astype(BF16)
    prior = jnp.dot(before, picked.astype(BF16), preferred_element_type=F32) + counts_ref[...]
    rank1 = jnp.sum(oh1 * prior, axis=1, keepdims=True)
    rank2 = jnp.sum(oh2 * prior, axis=1, keepdims=True)
    counts_ref[...] += jnp.sum(picked, axis=0, keepdims=True)

    route_ref[...] = jnp.where(
        lane == 0, i1, jnp.where(lane == 1, i2, jnp.where(lane == 2, w1, jnp.where(
            lane == 3, w2, jnp.where(lane == 4, rank1, jnp.where(lane == 5, rank2, 0.0))))))


def _mixer_back(yg, ma, gb, x, w_glu_a, w_glu_b, w_o, norm_g, w_route, b_route, tm):
    t = x.shape[0]
    return pl.pallas_call(
        _back_kernel,
        grid=(t // tm,),
        in_specs=[
            _row_spec(tm, SSM_WIDTH),
            _row_spec(tm, D_MODEL),
            _row_spec(tm, D_MODEL),
            _row_spec(tm, D_MODEL),
            _const_spec((SSM_WIDTH, D_MODEL)),
            _const_spec((SSM_WIDTH, D_MODEL)),
            _const_spec((D_MODEL, D_MODEL)),
            _const_spec((1, D_MODEL)),
            _const_spec((D_MODEL, 2 * LANES)),
            _const_spec((1, LANES)),
        ],
        out_specs=[_row_spec(tm, D_MODEL), _row_spec(tm, PACKED), _row_spec(tm, LANES),
                   pl.BlockSpec((1, LANES), lambda i: (0, 0))],
        out_shape=[
            jax.ShapeDtypeStruct((t, D_MODEL), F32),
            jax.ShapeDtypeStruct((t, PACKED), I32),
            jax.ShapeDtypeStruct((t, LANES), F32),
            jax.ShapeDtypeStruct((1, LANES), F32),
        ],
        compiler_params=_ARB,
        name="mixer_back",
    )(yg, ma, gb, x, w_glu_a, w_glu_b, w_o, norm_g, w_route, b_route)


_SC_MESH = dict(core_axis_name="c", subcore_axis_name="s")


def _sc_worker():
    return lax.axis_index("s") * SC_CORES + lax.axis_index("c")


def _sc_dispatch(rows, slot0, slot1, nslot):
    t, width = rows.shape
    per_w = t // (SC_CORES * SC_SUBCORES)
    assert per_w % SC_CHUNK == 0

    @functools.partial(
        pl.kernel,
        out_type=jax.ShapeDtypeStruct((nslot, width), rows.dtype),
        mesh=plsc.VectorSubcoreMesh(**_SC_MESH),
        scratch_types=[
            pltpu.VMEM((1, SC_CHUNK), I32),
            pltpu.VMEM((1, SC_CHUNK), I32),
            pltpu.VMEM((SC_CHUNK, width), rows.dtype),
        ],
        name="moe_dispatch",
    )
    def k(rows_hbm, s0_hbm, s1_hbm, out_hbm, i0_v, i1_v, rows_v):
        base = _sc_worker() * per_w

        @pl.loop(0, per_w // SC_CHUNK)
        def _(c):
            off = base + c * SC_CHUNK
            pltpu.sync_copy(s0_hbm.at[:, pl.ds(off, SC_CHUNK)], i0_v)
            pltpu.sync_copy(s1_hbm.at[:, pl.ds(off, SC_CHUNK)], i1_v)
            pltpu.sync_copy(rows_hbm.at[pl.ds(off, SC_CHUNK)], rows_v)
            pltpu.sync_copy(rows_v, out_hbm.at[i0_v.at[0]])
            pltpu.sync_copy(rows_v, out_hbm.at[i1_v.at[0]])

    return k(rows, slot0.reshape(1, t), slot1.reshape(1, t))


def _sc_gather(table, idx):
    n = idx.shape[0]
    width = table.shape[1]
    per_w = n // (SC_CORES * SC_SUBCORES)
    assert per_w % SC_CHUNK == 0

    @functools.partial(
        pl.kernel,
        out_type=jax.ShapeDtypeStruct((n, width), table.dtype),
        mesh=plsc.VectorSubcoreMesh(**_SC_MESH),
        scratch_types=[
            pltpu.VMEM((1, SC_CHUNK), I32),
            pltpu.VMEM((SC_CHUNK, width), table.dtype),
        ],
        name="moe_combine_gather",
    )
    def k(table_hbm, idx_hbm, out_hbm, idx_v, rows_v):
        base = _sc_worker() * per_w

        @pl.loop(0, per_w // SC_CHUNK)
        def _(c):
            off = base + c * SC_CHUNK
            pltpu.sync_copy(idx_hbm.at[:, pl.ds(off, SC_CHUNK)], idx_v)
            pltpu.sync_copy(table_hbm.at[idx_v.at[0]], rows_v)
            pltpu.sync_copy(rows_v, out_hbm.at[pl.ds(off, SC_CHUNK)])

    return k(table, idx.reshape(1, n))


def _moe_kernel(first_ref, last_ref, count_ref, total_ref,
                xs_hbm, wg_ref, wu_ref, wd_ref, out_hbm,
                xbuf, obuf, wg_s, wu_s, wd_s, xsem, osem):
    e = pl.program_id(0)
    ch = MOE_TILE
    half = ch // 2
    first, last, count, total = first_ref[e], last_ref[e], count_ref[e], total_ref[0]

    def x_copy(c, s):
        rows = pl.ds(pl.multiple_of(c * ch, ch), ch)
        return pltpu.make_async_copy(xs_hbm.at[rows], xbuf.at[s], xsem.at[s])

    def o_copy(c, s):
        rows = pl.ds(pl.multiple_of(c * ch, ch), ch)
        return pltpu.make_async_copy(obuf.at[s], out_hbm.at[rows], osem.at[s])

    @pl.when(e == 0)
    def _():
        x_copy(0, 0).start()

    @pl.when(last > first)
    def _():
        wg_s[...] = wg_ref[0].astype(BF16)
        wu_s[...] = wu_ref[0].astype(BF16)
        wd_s[...] = wd_ref[0].astype(BF16)

    def experts_rows(s, rows, valid):
        row = lax.broadcasted_iota(jnp.int32, (rows, PACKED), 0)
        words = jnp.where(row < valid, xbuf[s, 0:rows, :], 0)
        lo, hi = _unpack_rows(words)
        lo = lo.astype(BF16)
        hi = hi.astype(BF16)
        gate = (jnp.dot(lo, wg_s[:PACKED, :], preferred_element_type=F32)
                + jnp.dot(hi, wg_s[PACKED:, :], preferred_element_type=F32))
        up = (jnp.dot(lo, wu_s[:PACKED, :], preferred_element_type=F32)
              + jnp.dot(hi, wu_s[PACKED:, :], preferred_element_type=F32))
        act = (gate * jax.nn.sigmoid(gate) * up).astype(BF16)
        obuf[s, 0:rows, :] = _pack_rows(jnp.dot(act, wd_s[...], preferred_element_type=F32))

    def chunk(c, carry):
        s = c % 2
        x_copy(c, s).wait()

        @pl.when(c + 1 < total)
        def _():
            x_copy(c + 1, 1 - s).start(priority=1)

        @pl.when(c >= 2)
        def _():
            o_copy(c - 2, s).wait()

        valid = count - (c - first) * ch

        @pl.when(valid > half)
        def _():
            experts_rows(s, ch, valid)

        @pl.when(valid <= half)
        def _():
            experts_rows(s, half, valid)
            obuf[s, half:, :] = jnp.zeros((ch - half, PACKED), I32)

        o_copy(c, s).start(priority=1)
        return carry

    lax.fori_loop(first, last, chunk, 0)

    @pl.when(e == pl.num_programs(0) - 1)
    def _():
        @pl.when(total >= 2)
        def _():
            o_copy(total - 2, total % 2).wait()

        o_copy(total - 1, (total - 1) % 2).wait()


def _moe_experts(plan, xs, w_gate, w_up, w_down):
    nslot = xs.shape[0]
    any_spec = pl.BlockSpec(memory_space=pl.ANY)
    grid_spec = pltpu.PrefetchScalarGridSpec(
        num_scalar_prefetch=len(plan),
        grid=(N_EXPERTS,),
        in_specs=[
            any_spec,
            pl.BlockSpec((1, D_MODEL, EXPERT_FF), lambda e, *_: (e, 0, 0)),
            pl.BlockSpec((1, D_MODEL, EXPERT_FF), lambda e, *_: (e, 0, 0)),
            pl.BlockSpec((1, EXPERT_FF, D_MODEL), lambda e, *_: (e, 0, 0)),
        ],
        out_specs=any_spec,
        scratch_shapes=[
            pltpu.VMEM((2, MOE_TILE, PACKED), I32),
            pltpu.VMEM((2, MOE_TILE, PACKED), I32),
            pltpu.VMEM((D_MODEL, EXPERT_FF), BF16),
            pltpu.VMEM((D_MODEL, EXPERT_FF), BF16),
            pltpu.VMEM((EXPERT_FF, D_MODEL), BF16),
            pltpu.SemaphoreType.DMA((2,)),
            pltpu.SemaphoreType.DMA((2,)),
        ],
    )
    return pl.pallas_call(
        _moe_kernel,
        grid_spec=grid_spec,
        out_shape=jax.ShapeDtypeStruct((nslot, PACKED), I32),
        compiler_params=_ARB,
        name="moe_experts",
    )(*plan, xs, w_gate, w_up, w_down)


def _dispatch_plan(route, counts, tm):
    t = route.shape[0]
    e_ids = route[:, 0:2].astype(I32)
    ranks = route[:, 4:6].astype(I32)
    counts = counts[0, :N_EXPERTS].astype(I32)
    padded = ((counts + tm - 1) // tm) * tm
    ends = jnp.cumsum(padded)
    starts = ends - padded
    experts = jnp.arange(N_EXPERTS, dtype=I32)
    slot = jnp.sum(jnp.where(e_ids[..., None] == experts, starts, 0), axis=-1) + ranks
    plan = (starts // tm, ends // tm, counts, ends[-1:] // tm)
    return slot[:, 0], slot[:, 1], tuple(p.astype(I32) for p in plan)


def _final_kernel(x1_ref, y0_ref, y1_ref, route_ref, g_ref, out_ref):
    route = route_ref[...]
    w1 = route[:, 2:3]
    w2 = route[:, 3:4]
    a_lo, a_hi = _unpack_rows(y0_ref[...])
    b_lo, b_hi = _unpack_rows(y1_ref[...])
    moe = jnp.concatenate([w1 * a_lo + w2 * b_lo, w1 * a_hi + w2 * b_hi], axis=1)
    out_ref[...] = _rms(x1_ref[...] + moe, g_ref[...])


def _final(x1, ycat, route, norm_g, tm):
    t = x1.shape[0]
    nblk = t // tm
    return pl.pallas_call(
        _final_kernel,
        grid=(nblk,),
        in_specs=[
            _row_spec(tm, D_MODEL),
            pl.BlockSpec((tm, PACKED), lambda i: (i, 0)),
            pl.BlockSpec((tm, PACKED), lambda i: (i + nblk, 0)),
            _row_spec(tm, LANES),
            _const_spec((1, D_MODEL)),
        ],
        out_specs=_row_spec(tm, D_MODEL),
        out_shape=jax.ShapeDtypeStruct((t, D_MODEL), F32),
        compiler_params=_ARB,
        name="final_norm",
    )(x1, ycat, ycat, route, norm_g)


def kernel(x, norm_mix, w_in, conv_w, conv_b, w_conv_out, ssm_a_re, ssm_a_im, ssm_log_dt,
           ssm_b_re, ssm_b_im, ssm_c_re, ssm_c_im, ssm_d, w_glu_a, w_glu_b, gate_bias, w_o,
           norm_ffn, w_route_group, b_route_group, w_route_expert, b_route_expert,
           w_gate, w_up, w_down, norm_final):
    bsz, length, d = x.shape
    assert d == D_MODEL and norm_mix.shape[0] == 1
    t = bsz * length
    tm = TOKEN_TILE
    assert bsz == 1 and t % tm == 0
    xt = x.reshape(t, d)
    row = lambda a: a.reshape(1, -1).astype(F32)

    ma, gb, u = _mixer_front(
        xt, row(norm_mix[0]), w_in[0].astype(BF16), conv_w[0].reshape(3, CONV_WIDTH),
        row(conv_b[0]), w_conv_out[0].astype(BF16), gate_bias[0], tm)

    bre, bim, cw, a_re, a_im = _s5_weights(
        ssm_a_re[0], ssm_a_im[0], ssm_log_dt[0], ssm_b_re[0], ssm_b_im[0],
        ssm_c_re[0], ssm_c_im[0])
    yg = _s5_scan(u, bre, bim, cw, a_re, a_im, row(ssm_d[0]), tm)

    pad = LANES - N_EXPERTS - N_EXPERT_GROUPS
    w_route = jnp.concatenate(
        [w_route_expert[0], w_route_group[0], jnp.zeros((d, pad), F32)], axis=1)
    w_route_hi = w_route.astype(BF16)
    w_route_lo = (w_route - w_route_hi.astype(F32)).astype(BF16)
    b_route = jnp.concatenate(
        [b_route_expert[0], b_route_group[0], jnp.zeros((pad,), F32)]).reshape(1, LANES)
    x1, h2p, route, counts = _mixer_back(
        yg, ma, gb, xt, w_glu_a[0].astype(BF16), w_glu_b[0].astype(BF16), w_o[0].astype(BF16),
        row(norm_ffn[0]), jnp.concatenate([w_route_hi, w_route_lo], axis=1), b_route, tm)

    slot0, slot1, plan = _dispatch_plan(route, counts, MOE_TILE)
    nslot = 2 * t + N_EXPERTS * MOE_TILE
    xs = _sc_dispatch(h2p, slot0, slot1, nslot)
    ys = _moe_experts(plan, xs, w_gate[0], w_up[0], w_down[0])
    ycat = _sc_gather(ys, jnp.concatenate([slot0, slot1]))
    out = _final(x1, ycat, route, row(norm_final), tm)
    return out.reshape(bsz, length, d)
```

```python
import functools

import jax
import jax.numpy as jnp
from jax import lax
from jax.experimental import pallas as pl
from jax.experimental.pallas import tpu as pltpu
from jax.experimental.pallas import tpu_sc as plsc

F32 = jnp.float32
BF16 = jnp.bfloat16
I32 = jnp.int32

D_MODEL = 1024
CONV_WIDTH = 1024
SSM_WIDTH = 512
SSM_GROUP = 16
SSM_GROUPS = 32
SSM_STATE = 64
N_EXPERT_GROUPS = 4
EXPERTS_PER_GROUP = 8
N_EXPERTS = 32
EXPERT_FF = 512
EPS = 1e-6

LANES = 128
MXU_DIM = 256
TOKEN_TILE = 512
MOE_TILE = 512
SCAN_PITCH = 36
STATE_ROWS = 2 * SSM_GROUPS * SSM_STATE // LANES
HALF_ROWS = STATE_ROWS // 2
ROUTE_GROUP_LANE = N_EXPERTS
PACKED = D_MODEL // 2
HI_MASK = -65536
SC_CORES = 2
SC_SUBCORES = 16
SC_CHUNK = 128
VMEM_LIMIT = 56 * 1024 * 1024

_ARB = pltpu.CompilerParams(dimension_semantics=("arbitrary",), vmem_limit_bytes=VMEM_LIMIT)


def _const_spec(shape):
    nd = len(shape)
    return pl.BlockSpec(shape, lambda i, *_: (0,) * nd, pipeline_mode=pl.Buffered(1))


def _row_spec(tm, width):
    return pl.BlockSpec((tm, width), lambda i, *_: (i, 0))


def _rms(x, g):
    ms = jnp.mean(x * x, axis=-1, keepdims=True)
    return x * lax.rsqrt(ms + EPS) * g


def _pack_rows(v):
    bits = lax.bitcast_convert_type(v.astype(BF16).astype(F32), I32)
    lo = lax.shift_right_logical(bits[:, :PACKED], 16)
    hi = bits[:, PACKED:] & HI_MASK
    return lo | hi


def _unpack_rows(w):
    lo = lax.bitcast_convert_type(lax.shift_left(w, 16), F32)
    hi = lax.bitcast_convert_type(w & HI_MASK, F32)
    return lo, hi


def _front_kernel(x_ref, g_ref, win_ref, cw_ref, cb_ref, wco_ref, gbias_ref,
                  ma_ref, gb_ref, u_ref, carry_ref):
    tm = x_ref.shape[0]
    c0, c1, c2 = CONV_WIDTH, 2 * CONV_WIDTH, 3 * CONV_WIDTH
    c3 = c2 + SSM_WIDTH

    @pl.when(pl.program_id(0) == 0)
    def _():
        carry_ref[...] = jnp.zeros_like(carry_ref)

    h = _rms(x_ref[...], g_ref[...]).astype(BF16)

    def proj(lo, hi):
        return jnp.dot(h, win_ref[:, lo:hi], preferred_element_type=F32)

    v = proj(c1, c2) * proj(0, c0)
    row = lax.broadcasted_iota(jnp.int32, v.shape, 0)
    prev1 = carry_ref[7:8, :]
    prev2 = carry_ref[6:7, :]
    v1 = jnp.where(row == 0, prev1, pltpu.roll(v, 1, 0))
    v2 = jnp.where(row == 0, prev2, jnp.where(row == 1, prev1, pltpu.roll(v, 2, 0)))
    carry_ref[...] = v[tm - 8:, :]
    y = cw_ref[0:1, :] * v2 + cw_ref[1:2, :] * v1 + cw_ref[2:3, :] * v + cb_ref[...]
    z = (proj(c0, c1) * y).astype(BF16)
    ya = jnp.dot(z, wco_ref[...], preferred_element_type=F32)
    ga = jax.nn.sigmoid(proj(c3, c3 + D_MODEL) + gbias_ref[0:1, :])
    ma_ref[...] = (ga * ya).astype(BF16)
    gb = jax.nn.sigmoid(proj(c3 + D_MODEL, c3 + 2 * D_MODEL) + gbias_ref[1:2, :])
    gb_ref[...] = gb.astype(BF16)
    u_ref[...] = proj(c2, c3).astype(BF16)


def _mixer_front(x, norm_g, w_in, conv_w, conv_b, w_conv_out, gate_bias, tm):
    t = x.shape[0]
    in_cols = w_in.shape[1]
    return pl.pallas_call(
        _front_kernel,
        grid=(t // tm,),
        in_specs=[
            _row_spec(tm, D_MODEL),
            _const_spec((1, D_MODEL)),
            _const_spec((D_MODEL, in_cols)),
            _const_spec((3, CONV_WIDTH)),
            _const_spec((1, CONV_WIDTH)),
            _const_spec((CONV_WIDTH, D_MODEL)),
            _const_spec((2, D_MODEL)),
        ],
        out_specs=[_row_spec(tm, D_MODEL), _row_spec(tm, D_MODEL), _row_spec(tm, SSM_WIDTH)],
        out_shape=[
            jax.ShapeDtypeStruct((t, D_MODEL), BF16),
            jax.ShapeDtypeStruct((t, D_MODEL), BF16),
            jax.ShapeDtypeStruct((t, SSM_WIDTH), BF16),
        ],
        scratch_shapes=[pltpu.VMEM((8, CONV_WIDTH), F32)],
        compiler_params=_ARB,
        name="mixer_front",
    )(x, norm_g, w_in, conv_w, conv_b, w_conv_out, gate_bias)


def _ssm_kernel(u_ref, bre_ref, bim_ref, cw_ref, are_ref, aim_ref, d_ref,
                y_ref, r_ref, state_ref):
    tm = u_ref.shape[0]
    tiles_per_half = MXU_DIM * 4 // LANES

    @pl.when(pl.program_id(0) == 0)
    def _():
        state_ref[...] = jnp.zeros_like(state_ref)

    u = u_ref[...]
    for k in range(2):
        uk = u[:, k * MXU_DIM:(k + 1) * MXU_DIM]
        re = jnp.dot(uk, bre_ref[k], preferred_element_type=F32)
        im = jnp.dot(uk, bim_ref[k], preferred_element_type=F32)
        for jj in range(tiles_per_half):
            j = k * tiles_per_half + jj
            sl = slice(jj * LANES, (jj + 1) * LANES)
            r_ref[pl.ds(j, tm, stride=SCAN_PITCH), :] = re[:, sl]
            r_ref[pl.ds(HALF_ROWS + j, tm, stride=SCAN_PITCH), :] = im[:, sl]

    a_re = are_ref[...]
    a_im = aim_ref[...]

    def step(t, carry):
        s_re, s_im = carry
        base = t * SCAN_PITCH
        b_re = r_ref[pl.ds(base, HALF_ROWS), :]
        b_im = r_ref[pl.ds(base + HALF_ROWS, HALF_ROWS), :]
        n_re = a_re * s_re - a_im * s_im + b_re
        n_im = a_re * s_im + a_im * s_re + b_im
        r_ref[pl.ds(base, HALF_ROWS), :] = n_re
        r_ref[pl.ds(base + HALF_ROWS, HALF_ROWS), :] = n_im
        return n_re, n_im

    s_re, s_im = lax.fori_loop(
        0, tm, step, (state_ref[0:HALF_ROWS, :], state_ref[HALF_ROWS:, :]), unroll=8)
    state_ref[0:HALF_ROWS, :] = s_re
    state_ref[HALF_ROWS:, :] = s_im

    ys = []
    for k in range(2):
        cols = []
        for half in range(2):
            for jj in range(tiles_per_half):
                j = half * HALF_ROWS + k * tiles_per_half + jj
                cols.append(r_ref[pl.ds(j, tm, stride=SCAN_PITCH), :])
        s = jnp.concatenate(cols, axis=1).astype(BF16)
        ys.append(jnp.dot(s, cw_ref[k], preferred_element_type=F32))
    y = jnp.concatenate(ys, axis=1) + d_ref[...] * u.astype(F32)
    y_ref[...] = jax.nn.gelu(y).astype(BF16)


def _s5_scan(u, bre, bim, cw, a_re, a_im, d_skip, tm):
    t = u.shape[0]
    return pl.pallas_call(
        _ssm_kernel,
        grid=(t // tm,),
        in_specs=[
            _row_spec(tm, SSM_WIDTH),
            _const_spec(bre.shape),
            _const_spec(bim.shape),
            _const_spec(cw.shape),
            _const_spec(a_re.shape),
            _const_spec(a_im.shape),
            _const_spec((1, SSM_WIDTH)),
        ],
        out_specs=_row_spec(tm, SSM_WIDTH),
        out_shape=jax.ShapeDtypeStruct((t, SSM_WIDTH), BF16),
        scratch_shapes=[
            pltpu.VMEM((tm * SCAN_PITCH, LANES), F32),
            pltpu.VMEM((STATE_ROWS, LANES), F32),
        ],
        compiler_params=_ARB,
        name="s5_scan",
    )(u, bre, bim, cw, a_re, a_im, d_skip)


def _s5_weights(a_re, a_im, log_dt, b_re, b_im, c_re, c_im):
    g, n, h = SSM_GROUPS, SSM_STATE, SSM_GROUP
    gpt = MXU_DIM // h
    lam = lax.complex(a_re.astype(F32), a_im.astype(F32))
    dt = jnp.exp(log_dt.astype(F32))[:, None]
    a_bar = jnp.exp(lam * dt)
    b_bar = ((a_bar - 1.0) / lam)[..., None] * lax.complex(b_re.astype(F32), b_im.astype(F32))
    eye = jnp.eye(gpt, dtype=F32)

    def blk_in(w):
        w = w.reshape(g // gpt, gpt, n, h).transpose(0, 1, 3, 2)
        return jnp.einsum("kghn,gf->kghfn", w, eye).reshape(g // gpt, gpt * h, gpt * n)

    def blk_out(c):
        c = c.astype(F32).reshape(g // gpt, gpt, h, n).transpose(0, 1, 3, 2)
        return jnp.einsum("kgnh,gf->kgnfh", c, eye).reshape(g // gpt, gpt * n, gpt * h)

    bre = blk_in(jnp.real(b_bar)).astype(BF16)
    bim = blk_in(jnp.imag(b_bar)).astype(BF16)
    cw = jnp.concatenate([blk_out(c_re), -blk_out(c_im)], axis=1).astype(BF16)
    tile = (HALF_ROWS, LANES)
    return bre, bim, cw, jnp.real(a_bar).reshape(tile), jnp.imag(a_bar).reshape(tile)


def _back_kernel(yg_ref, ma_ref, gb_ref, x_ref, wa_ref, wb_ref, wo_ref, g2_ref, wr_ref, br_ref,
                 x1_ref, h2p_ref, route_ref, counts_ref):
    tm = x_ref.shape[0]

    @pl.when(pl.program_id(0) == 0)
    def _():
        counts_ref[...] = jnp.zeros_like(counts_ref)

    yg = yg_ref[...]
    yb = (jnp.dot(yg, wa_ref[...], preferred_element_type=F32)
          * jax.nn.sigmoid(jnp.dot(yg, wb_ref[...], preferred_element_type=F32)))
    mixed = (ma_ref[...].astype(F32) + gb_ref[...].astype(F32) * yb).astype(BF16)
    x1 = x_ref[...] + jnp.dot(mixed, wo_ref[...], preferred_element_type=F32)
    x1_ref[...] = x1
    h2 = _rms(x1, g2_ref[...])
    h2p_ref[...] = _pack_rows(h2)

    h2_hi = h2.astype(BF16)
    h2_lo = (h2 - h2_hi.astype(F32)).astype(BF16)
    both = jnp.dot(h2_hi, wr_ref[...], preferred_element_type=F32)
    logits = (both[:, :LANES]
              + (both[:, LANES:]
                 + jnp.dot(h2_lo, wr_ref[:, :LANES], preferred_element_type=F32))) + br_ref[...]
    lane = lax.broadcasted_iota(jnp.int32, logits.shape, 1)
    lane_f = lane.astype(F32)
    neg = jnp.float32(-jnp.inf)
    big = jnp.float32(LANES)
    is_grp = (lane >= ROUTE_GROUP_LANE) & (lane < ROUTE_GROUP_LANE + N_EXPERT_GROUPS)
    gl = jnp.where(is_grp, logits, neg)
    gmax = jnp.max(gl, axis=1, keepdims=True)
    gidx = jnp.min(jnp.where(gl == gmax, lane_f - ROUTE_GROUP_LANE, big), axis=1, keepdims=True)
    pg_top = 1.0 / jnp.sum(jnp.exp(gl - gmax), axis=1, keepdims=True)
    lane_grp = (lane // EXPERTS_PER_GROUP).astype(F32)
    el = jnp.where((lane < N_EXPERTS) & (lane_grp == gidx), logits, neg)
    l1 = jnp.max(el, axis=1, keepdims=True)
    i1 = jnp.min(jnp.where(el == l1, lane_f, big), axis=1, keepdims=True)
    el2 = jnp.where(lane_f == i1, neg, el)
    l2 = jnp.max(el2, axis=1, keepdims=True)
    i2 = jnp.min(jnp.where(el2 == l2, lane_f, big), axis=1, keepdims=True)
    r = jnp.exp(l2 - l1)
    w1 = pg_top / (1.0 + r)
    w2 = pg_top * r / (1.0 + r)

    oh1 = (lane_f == i1).astype(F32)
    oh2 = (lane_f == i2).astype(F32)
    picked = oh1 + oh2
    rr = lax.broadcasted_iota(jnp.int32, (tm, tm), 0)
    cc = lax.broadcasted_iota(jnp.int32, (tm, tm), 1)
    before = (cc < rr).astype(BF16)
    prior = jnp.dot(before, picked.astype(BF16), preferred_element_type=F32) + counts_ref[...]
    rank1 = jnp.sum(oh1 * prior, axis=1, keepdims=True)
    rank2 = jnp.sum(oh2 * prior, axis=1, keepdims=True)
    counts_ref[...] += jnp.sum(picked, axis=0, keepdims=True)

    route_ref[...] = jnp.where(
        lane == 0, i1, jnp.where(lane == 1, i2, jnp.where(lane == 2, w1, jnp.where(
            lane == 3, w2, jnp.where(lane == 4, rank1, jnp.where(lane == 5, rank2, 0.0))))))


def _mixer_back(yg, ma, gb, x, w_glu_a, w_glu_b, w_o, norm_g, w_route, b_route, tm):
    t = x.shape[0]
    return pl.pallas_call(
        _back_kernel,
        grid=(t // tm,),
        in_specs=[
            _row_spec(tm, SSM_WIDTH),
            _row_spec(tm, D_MODEL),
            _row_spec(tm, D_MODEL),
            _row_spec(tm, D_MODEL),
            _const_spec((SSM_WIDTH, D_MODEL)),
            _const_spec((SSM_WIDTH, D_MODEL)),
            _const_spec((D_MODEL, D_MODEL)),
            _const_spec((1, D_MODEL)),
            _const_spec((D_MODEL, 2 * LANES)),
            _const_spec((1, LANES)),
        ],
        out_specs=[_row_spec(tm, D_MODEL), _row_spec(tm, PACKED), _row_spec(tm, LANES),
                   pl.BlockSpec((1, LANES), lambda i: (0, 0))],
        out_shape=[
            jax.ShapeDtypeStruct((t, D_MODEL), F32),
            jax.ShapeDtypeStruct((t, PACKED), I32),
            jax.ShapeDtypeStruct((t, LANES), F32),
            jax.ShapeDtypeStruct((1, LANES), F32),
        ],
        compiler_params=_ARB,
        name="mixer_back",
    )(yg, ma, gb, x, w_glu_a, w_glu_b, w_o, norm_g, w_route, b_route)


_SC_MESH = dict(core_axis_name="c", subcore_axis_name="s")


def _sc_worker():
    return lax.axis_index("s") * SC_CORES + lax.axis_index("c")


def _sc_dispatch(rows, slot0, slot1, nslot):
    t, width = rows.shape
    per_w = t // (SC_CORES * SC_SUBCORES)
    assert per_w % SC_CHUNK == 0

    @functools.partial(
        pl.kernel,
        out_type=jax.ShapeDtypeStruct((nslot, width), rows.dtype),
        mesh=plsc.VectorSubcoreMesh(**_SC_MESH),
        scratch_types=[
            pltpu.VMEM((1, SC_CHUNK), I32),
            pltpu.VMEM((1, SC_CHUNK), I32),
            pltpu.VMEM((SC_CHUNK, width), rows.dtype),
        ],
        name="moe_dispatch",
    )
    def k(rows_hbm, s0_hbm, s1_hbm, out_hbm, i0_v, i1_v, rows_v):
        base = _sc_worker() * per_w

        @pl.loop(0, per_w // SC_CHUNK)
        def _(c):
            off = base + c * SC_CHUNK
            pltpu.sync_copy(s0_hbm.at[:, pl.ds(off, SC_CHUNK)], i0_v)
            pltpu.sync_copy(s1_hbm.at[:, pl.ds(off, SC_CHUNK)], i1_v)
            pltpu.sync_copy(rows_hbm.at[pl.ds(off, SC_CHUNK)], rows_v)
            pltpu.sync_copy(rows_v, out_hbm.at[i0_v.at[0]])
            pltpu.sync_copy(rows_v, out_hbm.at[i1_v.at[0]])

    return k(rows, slot0.reshape(1, t), slot1.reshape(1, t))


def _sc_gather(table, idx):
    n = idx.shape[0]
    width = table.shape[1]
    per_w = n // (SC_CORES * SC_SUBCORES)
    assert per_w % SC_CHUNK == 0

    @functools.partial(
        pl.kernel,
        out_type=jax.ShapeDtypeStruct((n, width), table.dtype),
        mesh=plsc.VectorSubcoreMesh(**_SC_MESH),
        scratch_types=[
            pltpu.VMEM((1, SC_CHUNK), I32),
            pltpu.VMEM((SC_CHUNK, width), table.dtype),
        ],
        name="moe_combine_gather",
    )
    def k(table_hbm, idx_hbm, out_hbm, idx_v, rows_v):
        base = _sc_worker() * per_w

        @pl.loop(0, per_w // SC_CHUNK)
        def _(c):
            off = base + c * SC_CHUNK
            pltpu.sync_copy(idx_hbm.at[:, pl.ds(off, SC_CHUNK)], idx_v)
            pltpu.sync_copy(table_hbm.at[idx_v.at[0]], rows_v)
            pltpu.sync_copy(rows_v, out_hbm.at[pl.ds(off, SC_CHUNK)])

    return k(table, idx.reshape(1, n))


def _moe_kernel(first_ref, last_ref, count_ref, total_ref,
                xs_hbm, wg_ref, wu_ref, wd_ref, out_hbm,
                xbuf0, xbuf1, obuf0, obuf1, wg_s, wu_s, wd_s, xsem, osem):
    e = pl.program_id(0)
    ch = MOE_TILE
    half = ch // 2
    first, last, count, total = first_ref[e], last_ref[e], count_ref[e], total_ref[0]
    xbufs = (xbuf0, xbuf1)
    obufs = (obuf0, obuf1)

    def x_copy(c, s):
        rows = pl.ds(pl.multiple_of(c * ch, ch), ch)
        return pltpu.make_async_copy(xs_hbm.at[rows], xbufs[s], xsem.at[s])

    def o_copy(c, s):
        rows = pl.ds(pl.multiple_of(c * ch, ch), ch)
        return pltpu.make_async_copy(obufs[s], out_hbm.at[rows], osem.at[s])

    @pl.when(e == 0)
    def _():
        x_copy(0, 0).start()

    @pl.when(last > first)
    def _():
        wg_s[...] = wg_ref[0].astype(BF16)
        wu_s[...] = wu_ref[0].astype(BF16)
        wd_s[...] = wd_ref[0].astype(BF16)

    def experts_rows(s, rows, valid):
        row = lax.broadcasted_iota(jnp.int32, (rows, PACKED), 0)
        words = jnp.where(row < valid, xbufs[s][0:rows, :], 0)
        lo, hi = _unpack_rows(words)
        lo = lo.astype(BF16)
        hi = hi.astype(BF16)
        gate = (jnp.dot(lo, wg_s[:PACKED, :], preferred_element_type=F32)
                + jnp.dot(hi, wg_s[PACKED:, :], preferred_element_type=F32))
        up = (jnp.dot(lo, wu_s[:PACKED, :], preferred_element_type=F32)
              + jnp.dot(hi, wu_s[PACKED:, :], preferred_element_type=F32))
        act = (gate * jax.nn.sigmoid(gate) * up).astype(BF16)
        obufs[s][0:rows, :] = _pack_rows(jnp.dot(act, wd_s[...], preferred_element_type=F32))

    def chunk_in_slot(c, s):
        x_copy(c, s).wait()

        @pl.when(c + 1 < total)
        def _():
            x_copy(c + 1, 1 - s).start()

        @pl.when(c >= 2)
        def _():
            o_copy(c - 2, s).wait()

        valid = count - (c - first) * ch

        @pl.when(valid > half)
        def _():
            experts_rows(s, ch, valid)

        @pl.when(valid <= half)
        def _():
            experts_rows(s, half, valid)
            obufs[s][half:, :] = jnp.zeros((ch - half, PACKED), I32)

        o_copy(c, s).start()

    def chunk(c, carry):
        for s in range(2):
            @pl.when(c % 2 == s)
            def _():
                chunk_in_slot(c, s)
        return carry

    lax.fori_loop(first, last, chunk, 0)

    @pl.when(e == pl.num_programs(0) - 1)
    def _():
        for s in range(2):
            @pl.when(((total - 1) % 2 == s) | (total >= 2))
            def _():
                o_copy(0, s).wait()


def _moe_experts(plan, xs, w_gate, w_up, w_down):
    nslot = xs.shape[0]
    any_spec = pl.BlockSpec(memory_space=pl.ANY)
    grid_spec = pltpu.PrefetchScalarGridSpec(
        num_scalar_prefetch=len(plan),
        grid=(N_EXPERTS,),
        in_specs=[
            any_spec,
            pl.BlockSpec((1, D_MODEL, EXPERT_FF), lambda e, *_: (e, 0, 0)),
            pl.BlockSpec((1, D_MODEL, EXPERT_FF), lambda e, *_: (e, 0, 0)),
            pl.BlockSpec((1, EXPERT_FF, D_MODEL), lambda e, *_: (e, 0, 0)),
        ],
        out_specs=any_spec,
        scratch_shapes=[
            pltpu.VMEM((MOE_TILE, PACKED), I32),
            pltpu.VMEM((MOE_TILE, PACKED), I32),
            pltpu.VMEM((MOE_TILE, PACKED), I32),
            pltpu.VMEM((MOE_TILE, PACKED), I32),
            pltpu.VMEM((D_MODEL, EXPERT_FF), BF16),
            pltpu.VMEM((D_MODEL, EXPERT_FF), BF16),
            pltpu.VMEM((EXPERT_FF, D_MODEL), BF16),
            pltpu.SemaphoreType.DMA((2,)),
            pltpu.SemaphoreType.DMA((2,)),
        ],
    )
    return pl.pallas_call(
        _moe_kernel,
        grid_spec=grid_spec,
        out_shape=jax.ShapeDtypeStruct((nslot, PACKED), I32),
        compiler_params=_ARB,
        name="moe_experts",
    )(*plan, xs, w_gate, w_up, w_down)


def _dispatch_plan(route, counts, tm):
    t = route.shape[0]
    e_ids = route[:, 0:2].astype(I32)
    ranks = route[:, 4:6].astype(I32)
    counts = counts[0, :N_EXPERTS].astype(I32)
    padded = ((counts + tm - 1) // tm) * tm
    ends = jnp.cumsum(padded)
    starts = ends - padded
    experts = jnp.arange(N_EXPERTS, dtype=I32)
    slot = jnp.sum(jnp.where(e_ids[..., None] == experts, starts, 0), axis=-1) + ranks
    plan = (starts // tm, ends // tm, counts, ends[-1:] // tm)
    return slot[:, 0], slot[:, 1], tuple(p.astype(I32) for p in plan)


def _final_kernel(x1_ref, y0_ref, y1_ref, route_ref, g_ref, out_ref):
    route = route_ref[...]
    w1 = route[:, 2:3]
    w2 = route[:, 3:4]
    a_lo, a_hi = _unpack_rows(y0_ref[...])
    b_lo, b_hi = _unpack_rows(y1_ref[...])
    moe = jnp.concatenate([w1 * a_lo + w2 * b_lo, w1 * a_hi + w2 * b_hi], axis=1)
    out_ref[...] = _rms(x1_ref[...] + moe, g_ref[...])


def _final(x1, ycat, route, norm_g, tm):
    t = x1.shape[0]
    nblk = t // tm
    return pl.pallas_call(
        _final_kernel,
        grid=(nblk,),
        in_specs=[
            _row_spec(tm, D_MODEL),
            pl.BlockSpec((tm, PACKED), lambda i: (i, 0)),
            pl.BlockSpec((tm, PACKED), lambda i: (i + nblk, 0)),
            _row_spec(tm, LANES),
            _const_spec((1, D_MODEL)),
        ],
        out_specs=_row_spec(tm, D_MODEL),
        out_shape=jax.ShapeDtypeStruct((t, D_MODEL), F32),
        compiler_params=_ARB,
        name="final_norm",
    )(x1, ycat, ycat, route, norm_g)


def kernel(x, norm_mix, w_in, conv_w, conv_b, w_conv_out, ssm_a_re, ssm_a_im, ssm_log_dt,
           ssm_b_re, ssm_b_im, ssm_c_re, ssm_c_im, ssm_d, w_glu_a, w_glu_b, gate_bias, w_o,
           norm_ffn, w_route_group, b_route_group, w_route_expert, b_route_expert,
           w_gate, w_up, w_down, norm_final):
    bsz, length, d = x.shape
    assert d == D_MODEL and norm_mix.shape[0] == 1
    t = bsz * length
    tm = TOKEN_TILE
    assert bsz == 1 and t % tm == 0
    xt = x.reshape(t, d)
    row = lambda a: a.reshape(1, -1).astype(F32)

    ma, gb, u = _mixer_front(
        xt, row(norm_mix[0]), w_in[0].astype(BF16), conv_w[0].reshape(3, CONV_WIDTH),
        row(conv_b[0]), w_conv_out[0].astype(BF16), gate_bias[0], tm)

    bre, bim, cw, a_re, a_im = _s5_weights(
        ssm_a_re[0], ssm_a_im[0], ssm_log_dt[0], ssm_b_re[0], ssm_b_im[0],
        ssm_c_re[0], ssm_c_im[0])
    yg = _s5_scan(u, bre, bim, cw, a_re, a_im, row(ssm_d[0]), tm)

    pad = LANES - N_EXPERTS - N_EXPERT_GROUPS
    w_route = jnp.concatenate(
        [w_route_expert[0], w_route_group[0], jnp.zeros((d, pad), F32)], axis=1)
    w_route_hi = w_route.astype(BF16)
    w_route_lo = (w_route - w_route_hi.astype(F32)).astype(BF16)
    b_route = jnp.concatenate(
        [b_route_expert[0], b_route_group[0], jnp.zeros((pad,), F32)]).reshape(1, LANES)
    x1, h2p, route, counts = _mixer_back(
        yg, ma, gb, xt, w_glu_a[0].astype(BF16), w_glu_b[0].astype(BF16), w_o[0].astype(BF16),
        row(norm_ffn[0]), jnp.concatenate([w_route_hi, w_route_lo], axis=1), b_route, tm)

    slot0, slot1, plan = _dispatch_plan(route, counts, MOE_TILE)
    nslot = 2 * t + N_EXPERTS * MOE_TILE
    xs = _sc_dispatch(h2p, slot0, slot1, nslot)
    ys = _moe_experts(plan, xs, w_gate[0], w_up[0], w_down[0])
    ycat = _sc_gather(ys, jnp.concatenate([slot0, slot1]))
    out = _final(x1, ycat, route, row(norm_final), tm)
    return out.reshape(bsz, length, d)
```

```python
import functools

import jax
import jax.numpy as jnp
from jax import lax
from jax.experimental import pallas as pl
from jax.experimental.pallas import tpu as pltpu
from jax.experimental.pallas import tpu_sc as plsc

F32 = jnp.float32
BF16 = jnp.bfloat16
I32 = jnp.int32

D_MODEL = 1024
CONV_WIDTH = 1024
SSM_WIDTH = 512
SSM_GROUP = 16
SSM_GROUPS = 32
SSM_STATE = 64
N_EXPERT_GROUPS = 4
EXPERTS_PER_GROUP = 8
N_EXPERTS = 32
EXPERT_FF = 512
EPS = 1e-6

LANES = 128
MXU_DIM = 256
TOKEN_TILE = 512
MOE_TILE = 512
SCAN_PITCH = 36
STATE_ROWS = 2 * SSM_GROUPS * SSM_STATE // LANES
HALF_ROWS = STATE_ROWS // 2
ROUTE_GROUP_LANE = N_EXPERTS
ROUTE_ROWS = 8
FINAL_CHUNKS = 4
PACKED = D_MODEL // 2
HI_MASK = -65536
SC_CORES = 2
SC_SUBCORES = 16
SC_CHUNK = 128
VMEM_LIMIT = 56 * 1024 * 1024

_ARB = pltpu.CompilerParams(dimension_semantics=("arbitrary",), vmem_limit_bytes=VMEM_LIMIT)


def _const_spec(shape):
    nd = len(shape)
    return pl.BlockSpec(shape, lambda i, *_: (0,) * nd, pipeline_mode=pl.Buffered(1))


def _row_spec(tm, width):
    return pl.BlockSpec((tm, width), lambda i, *_: (i, 0))


def _rms(x, g):
    ms = jnp.mean(x * x, axis=-1, keepdims=True)
    return x * lax.rsqrt(ms + EPS) * g


def _pack_rows(v):
    bits = lax.bitcast_convert_type(v.astype(BF16).astype(F32), I32)
    lo = lax.shift_right_logical(bits[:, :PACKED], 16)
    hi = bits[:, PACKED:] & HI_MASK
    return lo | hi


def _unpack_rows(w):
    lo = lax.bitcast_convert_type(lax.shift_left(w, 16), F32)
    hi = lax.bitcast_convert_type(w & HI_MASK, F32)
    return lo, hi


def _front_kernel(x_ref, g_ref, win_ref, cw_ref, cb_ref, wco_ref, gbias_ref,
                  ma_ref, gb_ref, u_ref, carry_ref):
    tm = x_ref.shape[0]
    c0, c1, c2 = CONV_WIDTH, 2 * CONV_WIDTH, 3 * CONV_WIDTH
    c3 = c2 + SSM_WIDTH

    @pl.when(pl.program_id(0) == 0)
    def _():
        carry_ref[...] = jnp.zeros_like(carry_ref)

    h = _rms(x_ref[...], g_ref[...]).astype(BF16)

    def proj(lo, hi):
        return jnp.dot(h, win_ref[:, lo:hi], preferred_element_type=F32)

    v = proj(c1, c2) * proj(0, c0)
    row = lax.broadcasted_iota(jnp.int32, v.shape, 0)
    prev1 = carry_ref[7:8, :]
    prev2 = carry_ref[6:7, :]
    v1 = jnp.where(row == 0, prev1, pltpu.roll(v, 1, 0))
    v2 = jnp.where(row == 0, prev2, jnp.where(row == 1, prev1, pltpu.roll(v, 2, 0)))
    carry_ref[...] = v[tm - 8:, :]
    y = cw_ref[0:1, :] * v2 + cw_ref[1:2, :] * v1 + cw_ref[2:3, :] * v + cb_ref[...]
    z = (proj(c0, c1) * y).astype(BF16)
    ya = jnp.dot(z, wco_ref[...], preferred_element_type=F32)
    ga = jax.nn.sigmoid(proj(c3, c3 + D_MODEL) + gbias_ref[0:1, :])
    ma_ref[...] = (ga * ya).astype(BF16)
    gb = jax.nn.sigmoid(proj(c3 + D_MODEL, c3 + 2 * D_MODEL) + gbias_ref[1:2, :])
    gb_ref[...] = gb.astype(BF16)
    u_ref[...] = proj(c2, c3).astype(BF16)


def _mixer_front(x, norm_g, w_in, conv_w, conv_b, w_conv_out, gate_bias, tm):
    t = x.shape[0]
    in_cols = w_in.shape[1]
    return pl.pallas_call(
        _front_kernel,
        grid=(t // tm,),
        in_specs=[
            _row_spec(tm, D_MODEL),
            _const_spec((1, D_MODEL)),
            _const_spec((D_MODEL, in_cols)),
            _const_spec((3, CONV_WIDTH)),
            _const_spec((1, CONV_WIDTH)),
            _const_spec((CONV_WIDTH, D_MODEL)),
            _const_spec((2, D_MODEL)),
        ],
        out_specs=[_row_spec(tm, D_MODEL), _row_spec(tm, D_MODEL), _row_spec(tm, SSM_WIDTH)],
        out_shape=[
            jax.ShapeDtypeStruct((t, D_MODEL), BF16),
            jax.ShapeDtypeStruct((t, D_MODEL), BF16),
            jax.ShapeDtypeStruct((t, SSM_WIDTH), BF16),
        ],
        scratch_shapes=[pltpu.VMEM((8, CONV_WIDTH), F32)],
        compiler_params=_ARB,
        name="mixer_front",
    )(x, norm_g, w_in, conv_w, conv_b, w_conv_out, gate_bias)


def _ssm_kernel(u_ref, bre_ref, bim_ref, cw_ref, are_ref, aim_ref, d_ref,
                y_ref, r_ref, state_ref):
    tm = u_ref.shape[0]
    tiles_per_half = MXU_DIM * 4 // LANES

    @pl.when(pl.program_id(0) == 0)
    def _():
        state_ref[...] = jnp.zeros_like(state_ref)

    u = u_ref[...]
    for k in range(2):
        uk = u[:, k * MXU_DIM:(k + 1) * MXU_DIM]
        re = jnp.dot(uk, bre_ref[k], preferred_element_type=F32)
        im = jnp.dot(uk, bim_ref[k], preferred_element_type=F32)
        for jj in range(tiles_per_half):
            j = k * tiles_per_half + jj
            sl = slice(jj * LANES, (jj + 1) * LANES)
            r_ref[pl.ds(j, tm, stride=SCAN_PITCH), :] = re[:, sl]
            r_ref[pl.ds(HALF_ROWS + j, tm, stride=SCAN_PITCH), :] = im[:, sl]

    a_re = are_ref[...]
    a_im = aim_ref[...]

    def step(t, carry):
        s_re, s_im = carry
        base = t * SCAN_PITCH
        b_re = r_ref[pl.ds(base, HALF_ROWS), :]
        b_im = r_ref[pl.ds(base + HALF_ROWS, HALF_ROWS), :]
        n_re = a_re * s_re - a_im * s_im + b_re
        n_im = a_re * s_im + a_im * s_re + b_im
        r_ref[pl.ds(base, HALF_ROWS), :] = n_re
        r_ref[pl.ds(base + HALF_ROWS, HALF_ROWS), :] = n_im
        return n_re, n_im

    s_re, s_im = lax.fori_loop(
        0, tm, step, (state_ref[0:HALF_ROWS, :], state_ref[HALF_ROWS:, :]), unroll=8)
    state_ref[0:HALF_ROWS, :] = s_re
    state_ref[HALF_ROWS:, :] = s_im

    ys = []
    for k in range(2):
        cols = []
        for half in range(2):
            for jj in range(tiles_per_half):
                j = half * HALF_ROWS + k * tiles_per_half + jj
                cols.append(r_ref[pl.ds(j, tm, stride=SCAN_PITCH), :])
        s = jnp.concatenate(cols, axis=1).astype(BF16)
        ys.append(jnp.dot(s, cw_ref[k], preferred_element_type=F32))
    y = jnp.concatenate(ys, axis=1) + d_ref[...] * u.astype(F32)
    y_ref[...] = jax.nn.gelu(y).astype(BF16)


def _s5_scan(u, bre, bim, cw, a_re, a_im, d_skip, tm):
    t = u.shape[0]
    return pl.pallas_call(
        _ssm_kernel,
        grid=(t // tm,),
        in_specs=[
            _row_spec(tm, SSM_WIDTH),
            _const_spec(bre.shape),
            _const_spec(bim.shape),
            _const_spec(cw.shape),
            _const_spec(a_re.shape),
            _const_spec(a_im.shape),
            _const_spec((1, SSM_WIDTH)),
        ],
        out_specs=_row_spec(tm, SSM_WIDTH),
        out_shape=jax.ShapeDtypeStruct((t, SSM_WIDTH), BF16),
        scratch_shapes=[
            pltpu.VMEM((tm * SCAN_PITCH, LANES), F32),
            pltpu.VMEM((STATE_ROWS, LANES), F32),
        ],
        compiler_params=_ARB,
        name="s5_scan",
    )(u, bre, bim, cw, a_re, a_im, d_skip)


def _s5_weights(a_re, a_im, log_dt, b_re, b_im, c_re, c_im):
    g, n, h = SSM_GROUPS, SSM_STATE, SSM_GROUP
    gpt = MXU_DIM // h
    lam = lax.complex(a_re.astype(F32), a_im.astype(F32))
    dt = jnp.exp(log_dt.astype(F32))[:, None]
    a_bar = jnp.exp(lam * dt)
    b_bar = ((a_bar - 1.0) / lam)[..., None] * lax.complex(b_re.astype(F32), b_im.astype(F32))
    eye = jnp.eye(gpt, dtype=F32)

    def blk_in(w):
        w = w.reshape(g // gpt, gpt, n, h).transpose(0, 1, 3, 2)
        return jnp.einsum("kghn,gf->kghfn", w, eye).reshape(g // gpt, gpt * h, gpt * n)

    def blk_out(c):
        c = c.astype(F32).reshape(g // gpt, gpt, h, n).transpose(0, 1, 3, 2)
        return jnp.einsum("kgnh,gf->kgnfh", c, eye).reshape(g // gpt, gpt * n, gpt * h)

    bre = blk_in(jnp.real(b_bar)).astype(BF16)
    bim = blk_in(jnp.imag(b_bar)).astype(BF16)
    cw = jnp.concatenate([blk_out(c_re), -blk_out(c_im)], axis=1).astype(BF16)
    tile = (HALF_ROWS, LANES)
    return bre, bim, cw, jnp.real(a_bar).reshape(tile), jnp.imag(a_bar).reshape(tile)


def _back_kernel(yg_ref, ma_ref, gb_ref, x_ref, wa_ref, wb_ref, wo_ref, g2_ref, wr_ref, br_ref,
                 x1_ref, h2p_ref, route_ref, table_ref, counts_ref):
    tm = x_ref.shape[0]

    @pl.when(pl.program_id(0) == 0)
    def _():
        counts_ref[...] = jnp.zeros_like(counts_ref)

    yg = yg_ref[...]
    yb = (jnp.dot(yg, wa_ref[...], preferred_element_type=F32)
          * jax.nn.sigmoid(jnp.dot(yg, wb_ref[...], preferred_element_type=F32)))
    mixed = (ma_ref[...].astype(F32) + gb_ref[...].astype(F32) * yb).astype(BF16)
    x1 = x_ref[...] + jnp.dot(mixed, wo_ref[...], preferred_element_type=F32)
    x1_ref[...] = x1
    h2 = _rms(x1, g2_ref[...])
    h2p_ref[...] = _pack_rows(h2)

    h2_hi = h2.astype(BF16)
    h2_lo = (h2 - h2_hi.astype(F32)).astype(BF16)
    both = jnp.dot(h2_hi, wr_ref[...], preferred_element_type=F32)
    logits = (both[:, :LANES]
              + (both[:, LANES:]
                 + jnp.dot(h2_lo, wr_ref[:, :LANES], preferred_element_type=F32))) + br_ref[...]
    lane = lax.broadcasted_iota(jnp.int32, logits.shape, 1)
    lane_f = lane.astype(F32)
    neg = jnp.float32(-jnp.inf)
    big = jnp.float32(LANES)
    is_grp = (lane >= ROUTE_GROUP_LANE) & (lane < ROUTE_GROUP_LANE + N_EXPERT_GROUPS)
    gl = jnp.where(is_grp, logits, neg)
    gmax = jnp.max(gl, axis=1, keepdims=True)
    gidx = jnp.min(jnp.where(gl == gmax, lane_f - ROUTE_GROUP_LANE, big), axis=1, keepdims=True)
    pg_top = 1.0 / jnp.sum(jnp.exp(gl - gmax), axis=1, keepdims=True)
    lane_grp = (lane // EXPERTS_PER_GROUP).astype(F32)
    el = jnp.where((lane < N_EXPERTS) & (lane_grp == gidx), logits, neg)
    l1 = jnp.max(el, axis=1, keepdims=True)
    i1 = jnp.min(jnp.where(el == l1, lane_f, big), axis=1, keepdims=True)
    el2 = jnp.where(lane_f == i1, neg, el)
    l2 = jnp.max(el2, axis=1, keepdims=True)
    i2 = jnp.min(jnp.where(el2 == l2, lane_f, big), axis=1, keepdims=True)
    r = jnp.exp(l2 - l1)
    w1 = pg_top / (1.0 + r)
    w2 = pg_top * r / (1.0 + r)

    oh1 = (lane_f == i1).astype(F32)
    oh2 = (lane_f == i2).astype(F32)
    picked = oh1 + oh2
    rr = lax.broadcasted_iota(jnp.int32, (tm, tm), 0)
    cc = lax.broadcasted_iota(jnp.int32, (tm, tm), 1)
    before = (cc < rr).astype(BF16)
    prior = jnp.dot(before, picked.astype(BF16), preferred_element_type=F32) + counts_ref[...]
    rank1 = jnp.sum(oh1 * prior, axis=1, keepdims=True)
    rank2 = jnp.sum(oh2 * prior, axis=1, keepdims=True)
    counts_ref[...] += jnp.sum(picked, axis=0, keepdims=True)

    route = jnp.where(
        lane == 0, i1, jnp.where(lane == 1, i2, jnp.where(lane == 2, w1, jnp.where(
            lane == 3, w2, jnp.where(lane == 4, rank1, jnp.where(lane == 5, rank2, 0.0))))))
    route_ref[...] = route
    table_ref[...] = route.T[0:ROUTE_ROWS, :]


def _mixer_back(yg, ma, gb, x, w_glu_a, w_glu_b, w_o, norm_g, w_route, b_route, tm):
    t = x.shape[0]
    return pl.pallas_call(
        _back_kernel,
        grid=(t // tm,),
        in_specs=[
            _row_spec(tm, SSM_WIDTH),
            _row_spec(tm, D_MODEL),
            _row_spec(tm, D_MODEL),
            _row_spec(tm, D_MODEL),
            _const_spec((SSM_WIDTH, D_MODEL)),
            _const_spec((SSM_WIDTH, D_MODEL)),
            _const_spec((D_MODEL, D_MODEL)),
            _const_spec((1, D_MODEL)),
            _const_spec((D_MODEL, 2 * LANES)),
            _const_spec((1, LANES)),
        ],
        out_specs=[_row_spec(tm, D_MODEL), _row_spec(tm, PACKED), _row_spec(tm, LANES),
                   pl.BlockSpec((ROUTE_ROWS, tm), lambda i: (0, i)),
                   pl.BlockSpec((1, LANES), lambda i: (0, 0))],
        out_shape=[
            jax.ShapeDtypeStruct((t, D_MODEL), F32),
            jax.ShapeDtypeStruct((t, PACKED), I32),
            jax.ShapeDtypeStruct((t, LANES), F32),
            jax.ShapeDtypeStruct((ROUTE_ROWS, t), F32),
            jax.ShapeDtypeStruct((1, LANES), F32),
        ],
        compiler_params=_ARB,
        name="mixer_back",
    )(yg, ma, gb, x, w_glu_a, w_glu_b, w_o, norm_g, w_route, b_route)


_SC_MESH = dict(core_axis_name="c", subcore_axis_name="s")


def _sc_worker():
    return lax.axis_index("s") * SC_CORES + lax.axis_index("c")


def _sc_dispatch(rows, slot0, slot1, nslot):
    t, width = rows.shape
    per_w = t // (SC_CORES * SC_SUBCORES)
    assert per_w % SC_CHUNK == 0

    @functools.partial(
        pl.kernel,
        out_type=jax.ShapeDtypeStruct((nslot, width), rows.dtype),
        mesh=plsc.VectorSubcoreMesh(**_SC_MESH),
        scratch_types=[
            pltpu.VMEM((1, SC_CHUNK), I32),
            pltpu.VMEM((1, SC_CHUNK), I32),
            pltpu.VMEM((SC_CHUNK, width), rows.dtype),
        ],
        name="moe_dispatch",
    )
    def k(rows_hbm, s0_hbm, s1_hbm, out_hbm, i0_v, i1_v, rows_v):
        base = _sc_worker() * per_w

        @pl.loop(0, per_w // SC_CHUNK)
        def _(c):
            off = base + c * SC_CHUNK
            pltpu.sync_copy(s0_hbm.at[:, pl.ds(off, SC_CHUNK)], i0_v)
            pltpu.sync_copy(s1_hbm.at[:, pl.ds(off, SC_CHUNK)], i1_v)
            pltpu.sync_copy(rows_hbm.at[pl.ds(off, SC_CHUNK)], rows_v)
            pltpu.sync_copy(rows_v, out_hbm.at[i0_v.at[0]])
            pltpu.sync_copy(rows_v, out_hbm.at[i1_v.at[0]])

    return k(rows, slot0.reshape(1, t), slot1.reshape(1, t))


def _sc_gather(table, idx):
    n = idx.shape[0]
    width = table.shape[1]
    per_w = n // (SC_CORES * SC_SUBCORES)
    assert per_w % SC_CHUNK == 0

    @functools.partial(
        pl.kernel,
        out_type=jax.ShapeDtypeStruct((n, width), table.dtype),
        mesh=plsc.VectorSubcoreMesh(**_SC_MESH),
        scratch_types=[
            pltpu.VMEM((1, SC_CHUNK), I32),
            pltpu.VMEM((SC_CHUNK, width), table.dtype),
        ],
        name="moe_combine_gather",
    )
    def k(table_hbm, idx_hbm, out_hbm, idx_v, rows_v):
        base = _sc_worker() * per_w

        @pl.loop(0, per_w // SC_CHUNK)
        def _(c):
            off = base + c * SC_CHUNK
            pltpu.sync_copy(idx_hbm.at[:, pl.ds(off, SC_CHUNK)], idx_v)
            pltpu.sync_copy(table_hbm.at[idx_v.at[0]], rows_v)
            pltpu.sync_copy(rows_v, out_hbm.at[pl.ds(off, SC_CHUNK)])

    return k(table, idx.reshape(1, n))


def _moe_kernel(first_ref, last_ref, count_ref, total_ref,
                xs_hbm, wg_ref, wu_ref, wd_ref, out_hbm,
                xbuf0, xbuf1, obuf0, obuf1, wg_s, wu_s, wd_s, xsem, osem):
    e = pl.program_id(0)
    ch = MOE_TILE
    half = ch // 2
    first, last, count, total = first_ref[e], last_ref[e], count_ref[e], total_ref[0]
    xbufs = (xbuf0, xbuf1)
    obufs = (obuf0, obuf1)

    def x_copy(c, s):
        rows = pl.ds(pl.multiple_of(c * ch, ch), ch)
        return pltpu.make_async_copy(xs_hbm.at[rows], xbufs[s], xsem.at[s])

    def o_copy(c, s):
        rows = pl.ds(pl.multiple_of(c * ch, ch), ch)
        return pltpu.make_async_copy(obufs[s], out_hbm.at[rows], osem.at[s])

    @pl.when(e == 0)
    def _():
        x_copy(0, 0).start()

    @pl.when(last > first)
    def _():
        wg_s[...] = wg_ref[0].astype(BF16)
        wu_s[...] = wu_ref[0].astype(BF16)
        wd_s[...] = wd_ref[0].astype(BF16)

    def experts_rows(s, rows, valid):
        row = lax.broadcasted_iota(jnp.int32, (rows, PACKED), 0)
        words = jnp.where(row < valid, xbufs[s][0:rows, :], 0)
        lo, hi = _unpack_rows(words)
        lo = lo.astype(BF16)
        hi = hi.astype(BF16)
        gate = (jnp.dot(lo, wg_s[:PACKED, :], preferred_element_type=F32)
                + jnp.dot(hi, wg_s[PACKED:, :], preferred_element_type=F32))
        up = (jnp.dot(lo, wu_s[:PACKED, :], preferred_element_type=F32)
              + jnp.dot(hi, wu_s[PACKED:, :], preferred_element_type=F32))
        act = (gate * jax.nn.sigmoid(gate) * up).astype(BF16)
        obufs[s][0:rows, :] = _pack_rows(jnp.dot(act, wd_s[...], preferred_element_type=F32))

    def chunk_in_slot(c, s):
        x_copy(c, s).wait()

        @pl.when(c + 1 < total)
        def _():
            x_copy(c + 1, 1 - s).start()

        @pl.when(c >= 2)
        def _():
            o_copy(c - 2, s).wait()

        valid = count - (c - first) * ch

        @pl.when(valid > half)
        def _():
            experts_rows(s, ch, valid)

        @pl.when(valid <= half)
        def _():
            experts_rows(s, half, valid)
            obufs[s][half:, :] = jnp.zeros((ch - half, PACKED), I32)

        o_copy(c, s).start()

    def chunk(c, carry):
        for s in range(2):
            @pl.when(c % 2 == s)
            def _():
                chunk_in_slot(c, s)
        return carry

    lax.fori_loop(first, last, chunk, 0)

    @pl.when(e == pl.num_programs(0) - 1)
    def _():
        for s in range(2):
            @pl.when(((total - 1) % 2 == s) | (total >= 2))
            def _():
                o_copy(0, s).wait()


def _moe_experts(plan, xs, w_gate, w_up, w_down):
    nslot = xs.shape[0]
    any_spec = pl.BlockSpec(memory_space=pl.ANY)
    grid_spec = pltpu.PrefetchScalarGridSpec(
        num_scalar_prefetch=len(plan),
        grid=(N_EXPERTS,),
        in_specs=[
            any_spec,
            pl.BlockSpec((1, D_MODEL, EXPERT_FF), lambda e, *_: (e, 0, 0)),
            pl.BlockSpec((1, D_MODEL, EXPERT_FF), lambda e, *_: (e, 0, 0)),
            pl.BlockSpec((1, EXPERT_FF, D_MODEL), lambda e, *_: (e, 0, 0)),
        ],
        out_specs=any_spec,
        scratch_shapes=[
            pltpu.VMEM((MOE_TILE, PACKED), I32),
            pltpu.VMEM((MOE_TILE, PACKED), I32),
            pltpu.VMEM((MOE_TILE, PACKED), I32),
            pltpu.VMEM((MOE_TILE, PACKED), I32),
            pltpu.VMEM((D_MODEL, EXPERT_FF), BF16),
            pltpu.VMEM((D_MODEL, EXPERT_FF), BF16),
            pltpu.VMEM((EXPERT_FF, D_MODEL), BF16),
            pltpu.SemaphoreType.DMA((2,)),
            pltpu.SemaphoreType.DMA((2,)),
        ],
    )
    return pl.pallas_call(
        _moe_kernel,
        grid_spec=grid_spec,
        out_shape=jax.ShapeDtypeStruct((nslot, PACKED), I32),
        compiler_params=_ARB,
        name="moe_experts",
    )(*plan, xs, w_gate, w_up, w_down)


def _dispatch_plan(table, counts, tm):
    e_ids = table[0:2].astype(I32)
    ranks = table[4:6].astype(I32)
    counts = counts[0, :N_EXPERTS].astype(I32)
    padded = ((counts + tm - 1) // tm) * tm
    ends = jnp.cumsum(padded)
    starts = ends - padded
    experts = jnp.arange(N_EXPERTS, dtype=I32)
    slot = jnp.sum(jnp.where(e_ids[..., None] == experts, starts, 0), axis=-1) + ranks
    plan = (starts // tm, ends // tm, counts, ends[-1:] // tm)
    return slot, tuple(p.astype(I32) for p in plan)


def _final_kernel(x1_ref, y0_ref, y1_ref, route_ref, g_ref, *rest):
    out_ref = rest[-1]
    route = route_ref[...]
    w1 = route[:, 2:3]
    w2 = route[:, 3:4]
    a_lo, a_hi = _unpack_rows(y0_ref[...])
    b_lo, b_hi = _unpack_rows(y1_ref[...])
    moe = jnp.concatenate([w1 * a_lo + w2 * b_lo, w1 * a_hi + w2 * b_hi], axis=1)
    out_ref[...] = _rms(x1_ref[...] + moe, g_ref[...])


def _final(x1, ycat, route, norm_g, tm, chunk, prev):
    t = x1.shape[0]
    nblk = t // tm // FINAL_CHUNKS
    off = chunk * nblk
    in_specs = [
        pl.BlockSpec((tm, D_MODEL), lambda i: (i + off, 0)),
        pl.BlockSpec((tm, PACKED), lambda i: (i, 0)),
        pl.BlockSpec((tm, PACKED), lambda i: (i + nblk, 0)),
        pl.BlockSpec((tm, LANES), lambda i: (i + off, 0)),
        _const_spec((1, D_MODEL)),
    ]
    args = [x1, ycat, ycat, route, norm_g]
    aliases = {}
    if prev is not None:
        in_specs.append(pl.BlockSpec(memory_space=pl.ANY))
        args.append(prev)
        aliases = {len(args) - 1: 0}
    return pl.pallas_call(
        _final_kernel,
        grid=(nblk,),
        in_specs=in_specs,
        out_specs=pl.BlockSpec((tm, D_MODEL), lambda i: (i + off, 0)),
        out_shape=jax.ShapeDtypeStruct((t, D_MODEL), F32),
        input_output_aliases=aliases,
        compiler_params=_ARB,
        name="final_norm",
    )(*args)


def kernel(x, norm_mix, w_in, conv_w, conv_b, w_conv_out, ssm_a_re, ssm_a_im, ssm_log_dt,
           ssm_b_re, ssm_b_im, ssm_c_re, ssm_c_im, ssm_d, w_glu_a, w_glu_b, gate_bias, w_o,
           norm_ffn, w_route_group, b_route_group, w_route_expert, b_route_expert,
           w_gate, w_up, w_down, norm_final):
    bsz, length, d = x.shape
    assert d == D_MODEL and norm_mix.shape[0] == 1
    t = bsz * length
    tm = TOKEN_TILE
    assert bsz == 1 and t % (tm * FINAL_CHUNKS) == 0
    xt = x.reshape(t, d)
    row = lambda a: a.reshape(1, -1).astype(F32)

    ma, gb, u = _mixer_front(
        xt, row(norm_mix[0]), w_in[0].astype(BF16), conv_w[0].reshape(3, CONV_WIDTH),
        row(conv_b[0]), w_conv_out[0].astype(BF16), gate_bias[0], tm)

    bre, bim, cw, a_re, a_im = _s5_weights(
        ssm_a_re[0], ssm_a_im[0], ssm_log_dt[0], ssm_b_re[0], ssm_b_im[0],
        ssm_c_re[0], ssm_c_im[0])
    yg = _s5_scan(u, bre, bim, cw, a_re, a_im, row(ssm_d[0]), tm)

    pad = LANES - N_EXPERTS - N_EXPERT_GROUPS
    w_route = jnp.concatenate(
        [w_route_expert[0], w_route_group[0], jnp.zeros((d, pad), F32)], axis=1)
    w_route_hi = w_route.astype(BF16)
    w_route_lo = (w_route - w_route_hi.astype(F32)).astype(BF16)
    b_route = jnp.concatenate(
        [b_route_expert[0], b_route_group[0], jnp.zeros((pad,), F32)]).reshape(1, LANES)
    x1, h2p, route, table, counts = _mixer_back(
        yg, ma, gb, xt, w_glu_a[0].astype(BF16), w_glu_b[0].astype(BF16), w_o[0].astype(BF16),
        row(norm_ffn[0]), jnp.concatenate([w_route_hi, w_route_lo], axis=1), b_route, tm)

    slot, plan = _dispatch_plan(table, counts, MOE_TILE)
    nslot = 2 * t + N_EXPERTS * MOE_TILE
    xs = _sc_dispatch(h2p, slot[0], slot[1], nslot)
    ys = _moe_experts(plan, xs, w_gate[0], w_up[0], w_down[0])
    out = None
    tc = t // FINAL_CHUNKS
    for k in range(FINAL_CHUNKS):
        ycat = _sc_gather(ys, slot[:, k * tc:(k + 1) * tc].reshape(-1))
        out = _final(x1, ycat, route, row(norm_final), tm, k, out)
    return out.reshape(bsz, length, d)
```

```python
import functools

import jax
import jax.numpy as jnp
from jax import lax
from jax.experimental import pallas as pl
from jax.experimental.pallas import tpu as pltpu
from jax.experimental.pallas import tpu_sc as plsc

F32 = jnp.float32
BF16 = jnp.bfloat16
I32 = jnp.int32

D_MODEL = 1024
CONV_WIDTH = 1024
SSM_WIDTH = 512
SSM_GROUP = 16
SSM_GROUPS = 32
SSM_STATE = 64
N_EXPERT_GROUPS = 4
EXPERTS_PER_GROUP = 8
N_EXPERTS = 32
EXPERT_FF = 512
EPS = 1e-6

LANES = 128
MXU_DIM = 256
TOKEN_TILE = 512
MOE_TILE = 512
SCAN_PITCH = 36
STATE_ROWS = 2 * SSM_GROUPS * SSM_STATE // LANES
HALF_ROWS = STATE_ROWS // 2
ROUTE_GROUP_LANE = N_EXPERTS
ROUTE_ROWS = 8
FINAL_CHUNKS = 4
PACKED = D_MODEL // 2
HI_MASK = -65536
SC_CORES = 2
SC_SUBCORES = 16
SC_CHUNK = 128
VMEM_LIMIT = 56 * 1024 * 1024

_ARB = pltpu.CompilerParams(dimension_semantics=("arbitrary",), vmem_limit_bytes=VMEM_LIMIT)


def _const_spec(shape):
    nd = len(shape)
    return pl.BlockSpec(shape, lambda i, *_: (0,) * nd, pipeline_mode=pl.Buffered(1))


def _row_spec(tm, width):
    return pl.BlockSpec((tm, width), lambda i, *_: (i, 0))


def _rms(x, g):
    ms = jnp.mean(x * x, axis=-1, keepdims=True)
    return x * lax.rsqrt(ms + EPS) * g


def _pack_rows(v):
    bits = lax.bitcast_convert_type(v.astype(BF16).astype(F32), I32)
    lo = lax.shift_right_logical(bits[:, :PACKED], 16)
    hi = bits[:, PACKED:] & HI_MASK
    return lo | hi


def _unpack_rows(w):
    lo = lax.bitcast_convert_type(lax.shift_left(w, 16), F32)
    hi = lax.bitcast_convert_type(w & HI_MASK, F32)
    return lo, hi


def _front_kernel(x_ref, g_ref, win_ref, cw_ref, cb_ref, wco_ref, gbias_ref,
                  bre_ref, bim_ref, cw2_ref, are_ref, aim_ref, d_ref,
                  ma_ref, gb_ref, y_ref, carry_ref, u_ref, r_ref, state_ref):
    tm = x_ref.shape[0]
    c0, c1, c2 = CONV_WIDTH, 2 * CONV_WIDTH, 3 * CONV_WIDTH
    c3 = c2 + SSM_WIDTH

    @pl.when(pl.program_id(0) == 0)
    def _():
        carry_ref[...] = jnp.zeros_like(carry_ref)
        u_ref[...] = jnp.zeros_like(u_ref)
        state_ref[...] = jnp.zeros_like(state_ref)

    h = _rms(x_ref[...], g_ref[...]).astype(BF16)
    u_prev = u_ref[...]
    _s5_scan_tile(u_prev, bre_ref, bim_ref, are_ref, aim_ref, r_ref, state_ref)

    def proj(lo, hi):
        return jnp.dot(h, win_ref[:, lo:hi], preferred_element_type=F32)

    v = proj(c1, c2) * proj(0, c0)
    row = lax.broadcasted_iota(jnp.int32, v.shape, 0)
    prev1 = carry_ref[7:8, :]
    prev2 = carry_ref[6:7, :]
    v1 = jnp.where(row == 0, prev1, pltpu.roll(v, 1, 0))
    v2 = jnp.where(row == 0, prev2, jnp.where(row == 1, prev1, pltpu.roll(v, 2, 0)))
    carry_ref[...] = v[tm - 8:, :]
    y = cw_ref[0:1, :] * v2 + cw_ref[1:2, :] * v1 + cw_ref[2:3, :] * v + cb_ref[...]
    z = (proj(c0, c1) * y).astype(BF16)
    ya = jnp.dot(z, wco_ref[...], preferred_element_type=F32)
    ga = jax.nn.sigmoid(proj(c3, c3 + D_MODEL) + gbias_ref[0:1, :])
    ma_ref[...] = (ga * ya).astype(BF16)
    gb = jax.nn.sigmoid(proj(c3 + D_MODEL, c3 + 2 * D_MODEL) + gbias_ref[1:2, :])
    gb_ref[...] = gb.astype(BF16)
    u_new = proj(c2, c3).astype(BF16)
    _s5_readout(u_prev, cw2_ref, d_ref, y_ref, r_ref)
    u_ref[...] = u_new


def _mixer_front(x, norm_g, w_in, conv_w, conv_b, w_conv_out, gate_bias, s5_weights, d_skip, tm):
    bre, bim, cw, a_re, a_im = s5_weights
    t = x.shape[0]
    in_cols = w_in.shape[1]
    n = t // tm
    last = lambda i: (jnp.minimum(i, n - 1), 0)
    lag = lambda i: (jnp.maximum(i - 1, 0), 0)
    return pl.pallas_call(
        _front_kernel,
        grid=(n + 1,),
        in_specs=[
            pl.BlockSpec((tm, D_MODEL), last),
            _const_spec((1, D_MODEL)),
            _const_spec((D_MODEL, in_cols)),
            _const_spec((3, CONV_WIDTH)),
            _const_spec((1, CONV_WIDTH)),
            _const_spec((CONV_WIDTH, D_MODEL)),
            _const_spec((2, D_MODEL)),
            _const_spec(bre.shape),
            _const_spec(bim.shape),
            _const_spec(cw.shape),
            _const_spec(a_re.shape),
            _const_spec(a_im.shape),
            _const_spec((1, SSM_WIDTH)),
        ],
        out_specs=[_row_spec(tm, D_MODEL), _row_spec(tm, D_MODEL),
                   pl.BlockSpec((tm, SSM_WIDTH), lag)],
        out_shape=[
            jax.ShapeDtypeStruct((t + tm, D_MODEL), BF16),
            jax.ShapeDtypeStruct((t + tm, D_MODEL), BF16),
            jax.ShapeDtypeStruct((t, SSM_WIDTH), BF16),
        ],
        scratch_shapes=[
            pltpu.VMEM((8, CONV_WIDTH), F32),
            pltpu.VMEM((tm, SSM_WIDTH), BF16),
            pltpu.VMEM((tm * SCAN_PITCH, LANES), F32),
            pltpu.VMEM((STATE_ROWS, LANES), F32),
        ],
        compiler_params=_ARB,
        name="mixer_front",
    )(x, norm_g, w_in, conv_w, conv_b, w_conv_out, gate_bias, bre, bim, cw, a_re, a_im, d_skip)


def _s5_scan_tile(u, bre_ref, bim_ref, are_ref, aim_ref, r_ref, state_ref):
    tm = u.shape[0]
    tiles_per_half = MXU_DIM * 4 // LANES

    for k in range(2):
        uk = u[:, k * MXU_DIM:(k + 1) * MXU_DIM]
        re = jnp.dot(uk, bre_ref[k], preferred_element_type=F32)
        im = jnp.dot(uk, bim_ref[k], preferred_element_type=F32)
        for jj in range(tiles_per_half):
            j = k * tiles_per_half + jj
            sl = slice(jj * LANES, (jj + 1) * LANES)
            r_ref[pl.ds(j, tm, stride=SCAN_PITCH), :] = re[:, sl]
            r_ref[pl.ds(HALF_ROWS + j, tm, stride=SCAN_PITCH), :] = im[:, sl]

    a_re = are_ref[...]
    a_im = aim_ref[...]

    s_re = state_ref[0:HALF_ROWS, :]
    s_im = state_ref[HALF_ROWS:, :]
    for t in range(tm):
        base = t * SCAN_PITCH
        b_re = r_ref[pl.ds(base, HALF_ROWS), :]
        b_im = r_ref[pl.ds(base + HALF_ROWS, HALF_ROWS), :]
        s_re, s_im = (a_re * s_re - a_im * s_im + b_re,
                      a_re * s_im + a_im * s_re + b_im)
        r_ref[pl.ds(base, HALF_ROWS), :] = s_re
        r_ref[pl.ds(base + HALF_ROWS, HALF_ROWS), :] = s_im
    state_ref[0:HALF_ROWS, :] = s_re
    state_ref[HALF_ROWS:, :] = s_im


def _s5_readout(u, cw_ref, d_ref, y_ref, r_ref):
    tm = u.shape[0]
    tiles_per_half = MXU_DIM * 4 // LANES
    ys = []
    for k in range(2):
        cols = []
        for half in range(2):
            for jj in range(tiles_per_half):
                j = half * HALF_ROWS + k * tiles_per_half + jj
                cols.append(r_ref[pl.ds(j, tm, stride=SCAN_PITCH), :])
        s = jnp.concatenate(cols, axis=1).astype(BF16)
        ys.append(jnp.dot(s, cw_ref[k], preferred_element_type=F32))
    y = jnp.concatenate(ys, axis=1) + d_ref[...] * u.astype(F32)
    y_ref[...] = jax.nn.gelu(y).astype(BF16)


def _s5_weights(a_re, a_im, log_dt, b_re, b_im, c_re, c_im):
    g, n, h = SSM_GROUPS, SSM_STATE, SSM_GROUP
    gpt = MXU_DIM // h
    lam = lax.complex(a_re.astype(F32), a_im.astype(F32))
    dt = jnp.exp(log_dt.astype(F32))[:, None]
    a_bar = jnp.exp(lam * dt)
    b_bar = ((a_bar - 1.0) / lam)[..., None] * lax.complex(b_re.astype(F32), b_im.astype(F32))
    eye = jnp.eye(gpt, dtype=F32)

    def blk_in(w):
        w = w.reshape(g // gpt, gpt, n, h).transpose(0, 1, 3, 2)
        return jnp.einsum("kghn,gf->kghfn", w, eye).reshape(g // gpt, gpt * h, gpt * n)

    def blk_out(c):
        c = c.astype(F32).reshape(g // gpt, gpt, h, n).transpose(0, 1, 3, 2)
        return jnp.einsum("kgnh,gf->kgnfh", c, eye).reshape(g // gpt, gpt * n, gpt * h)

    bre = blk_in(jnp.real(b_bar)).astype(BF16)
    bim = blk_in(jnp.imag(b_bar)).astype(BF16)
    cw = jnp.concatenate([blk_out(c_re), -blk_out(c_im)], axis=1).astype(BF16)
    tile = (HALF_ROWS, LANES)
    return bre, bim, cw, jnp.real(a_bar).reshape(tile), jnp.imag(a_bar).reshape(tile)


def _back_kernel(yg_ref, ma_ref, gb_ref, x_ref, wa_ref, wb_ref, wo_ref, g2_ref, wr_ref, br_ref,
                 x1_ref, h2p_ref, route_ref, table_ref, counts_ref):
    tm = x_ref.shape[0]

    @pl.when(pl.program_id(0) == 0)
    def _():
        counts_ref[...] = jnp.zeros_like(counts_ref)

    yg = yg_ref[...]
    yb = (jnp.dot(yg, wa_ref[...], preferred_element_type=F32)
          * jax.nn.sigmoid(jnp.dot(yg, wb_ref[...], preferred_element_type=F32)))
    mixed = (ma_ref[...].astype(F32) + gb_ref[...].astype(F32) * yb).astype(BF16)
    x1 = x_ref[...] + jnp.dot(mixed, wo_ref[...], preferred_element_type=F32)
    x1_ref[...] = x1
    h2 = _rms(x1, g2_ref[...])
    h2p_ref[...] = _pack_rows(h2)

    h2_hi = h2.astype(BF16)
    h2_lo = (h2 - h2_hi.astype(F32)).astype(BF16)
    both = jnp.dot(h2_hi, wr_ref[...], preferred_element_type=F32)
    logits = (both[:, :LANES]
              + (both[:, LANES:]
                 + jnp.dot(h2_lo, wr_ref[:, :LANES], preferred_element_type=F32))) + br_ref[...]
    lane = lax.broadcasted_iota(jnp.int32, logits.shape, 1)
    lane_f = lane.astype(F32)
    neg = jnp.float32(-jnp.inf)
    big = jnp.float32(LANES)
    is_grp = (lane >= ROUTE_GROUP_LANE) & (lane < ROUTE_GROUP_LANE + N_EXPERT_GROUPS)
    gl = jnp.where(is_grp, logits, neg)
    gmax = jnp.max(gl, axis=1, keepdims=True)
    gidx = jnp.min(jnp.where(gl == gmax, lane_f - ROUTE_GROUP_LANE, big), axis=1, keepdims=True)
    pg_top = 1.0 / jnp.sum(jnp.exp(gl - gmax), axis=1, keepdims=True)
    lane_grp = (lane // EXPERTS_PER_GROUP).astype(F32)
    el = jnp.where((lane < N_EXPERTS) & (lane_grp == gidx), logits, neg)
    l1 = jnp.max(el, axis=1, keepdims=True)
    i1 = jnp.min(jnp.where(el == l1, lane_f, big), axis=1, keepdims=True)
    el2 = jnp.where(lane_f == i1, neg, el)
    l2 = jnp.max(el2, axis=1, keepdims=True)
    i2 = jnp.min(jnp.where(el2 == l2, lane_f, big), axis=1, keepdims=True)
    r = jnp.exp(l2 - l1)
    w1 = pg_top / (1.0 + r)
    w2 = pg_top * r / (1.0 + r)

    oh1 = (lane_f == i1).astype(F32)
    oh2 = (lane_f == i2).astype(F32)
    picked = oh1 + oh2
    rr = lax.broadcasted_iota(jnp.int32, (tm, tm), 0)
    cc = lax.broadcasted_iota(jnp.int32, (tm, tm), 1)
    before = (cc < rr).astype(BF16)
    prior = jnp.dot(before, picked.astype(BF16), preferred_element_type=F32) + counts_ref[...]
    rank1 = jnp.sum(oh1 * prior, axis=1, keepdims=True)
    rank2 = jnp.sum(oh2 * prior, axis=1, keepdims=True)
    counts_ref[...] += jnp.sum(picked, axis=0, keepdims=True)

    route = jnp.where(
        lane == 0, i1, jnp.where(lane == 1, i2, jnp.where(lane == 2, w1, jnp.where(
            lane == 3, w2, jnp.where(lane == 4, rank1, jnp.where(lane == 5, rank2, 0.0))))))
    route_ref[...] = route
    table_ref[...] = route.T[0:ROUTE_ROWS, :]


def _mixer_back(yg, ma, gb, x, w_glu_a, w_glu_b, w_o, norm_g, w_route, b_route, tm):
    t = x.shape[0]
    return pl.pallas_call(
        _back_kernel,
        grid=(t // tm,),
        in_specs=[
            _row_spec(tm, SSM_WIDTH),
            _row_spec(tm, D_MODEL),
            _row_spec(tm, D_MODEL),
            _row_spec(tm, D_MODEL),
            _const_spec((SSM_WIDTH, D_MODEL)),
            _const_spec((SSM_WIDTH, D_MODEL)),
            _const_spec((D_MODEL, D_MODEL)),
            _const_spec((1, D_MODEL)),
            _const_spec((D_MODEL, 2 * LANES)),
            _const_spec((1, LANES)),
        ],
        out_specs=[_row_spec(tm, D_MODEL), _row_spec(tm, PACKED), _row_spec(tm, LANES),
                   pl.BlockSpec((ROUTE_ROWS, tm), lambda i: (0, i)),
                   pl.BlockSpec((1, LANES), lambda i: (0, 0))],
        out_shape=[
            jax.ShapeDtypeStruct((t, D_MODEL), F32),
            jax.ShapeDtypeStruct((t, PACKED), I32),
            jax.ShapeDtypeStruct((t, LANES), F32),
            jax.ShapeDtypeStruct((ROUTE_ROWS, t), F32),
            jax.ShapeDtypeStruct((1, LANES), F32),
        ],
        compiler_params=_ARB,
        name="mixer_back",
    )(yg, ma, gb, x, w_glu_a, w_glu_b, w_o, norm_g, w_route, b_route)


_SC_MESH = dict(core_axis_name="c", subcore_axis_name="s")


def _sc_worker():
    return lax.axis_index("s") * SC_CORES + lax.axis_index("c")


def _sc_dispatch(rows, slot0, slot1, nslot):
    t, width = rows.shape
    per_w = t // (SC_CORES * SC_SUBCORES)
    assert per_w % SC_CHUNK == 0

    @functools.partial(
        pl.kernel,
        out_type=jax.ShapeDtypeStruct((nslot, width), rows.dtype),
        mesh=plsc.VectorSubcoreMesh(**_SC_MESH),
        scratch_types=[
            pltpu.VMEM((1, SC_CHUNK), I32),
            pltpu.VMEM((1, SC_CHUNK), I32),
            pltpu.VMEM((SC_CHUNK, width), rows.dtype),
        ],
        name="moe_dispatch",
    )
    def k(rows_hbm, s0_hbm, s1_hbm, out_hbm, i0_v, i1_v, rows_v):
        base = _sc_worker() * per_w

        @pl.loop(0, per_w // SC_CHUNK)
        def _(c):
            off = base + c * SC_CHUNK
            pltpu.sync_copy(s0_hbm.at[:, pl.ds(off, SC_CHUNK)], i0_v)
            pltpu.sync_copy(s1_hbm.at[:, pl.ds(off, SC_CHUNK)], i1_v)
            pltpu.sync_copy(rows_hbm.at[pl.ds(off, SC_CHUNK)], rows_v)
            pltpu.sync_copy(rows_v, out_hbm.at[i0_v.at[0]])
            pltpu.sync_copy(rows_v, out_hbm.at[i1_v.at[0]])

    return k(rows, slot0.reshape(1, t), slot1.reshape(1, t))


def _sc_gather(table, idx):
    n = idx.shape[0]
    width = table.shape[1]
    per_w = n // (SC_CORES * SC_SUBCORES)
    assert per_w % SC_CHUNK == 0

    @functools.partial(
        pl.kernel,
        out_type=jax.ShapeDtypeStruct((n, width), table.dtype),
        mesh=plsc.VectorSubcoreMesh(**_SC_MESH),
        scratch_types=[
            pltpu.VMEM((1, SC_CHUNK), I32),
            pltpu.VMEM((SC_CHUNK, width), table.dtype),
        ],
        name="moe_combine_gather",
    )
    def k(table_hbm, idx_hbm, out_hbm, idx_v, rows_v):
        base = _sc_worker() * per_w

        @pl.loop(0, per_w // SC_CHUNK)
        def _(c):
            off = base + c * SC_CHUNK
            pltpu.sync_copy(idx_hbm.at[:, pl.ds(off, SC_CHUNK)], idx_v)
            pltpu.sync_copy(table_hbm.at[idx_v.at[0]], rows_v)
            pltpu.sync_copy(rows_v, out_hbm.at[pl.ds(off, SC_CHUNK)])

    return k(table, idx.reshape(1, n))


def _moe_kernel(first_ref, last_ref, count_ref, total_ref,
                xs_hbm, wg_ref, wu_ref, wd_ref, out_hbm,
                xbuf0, xbuf1, obuf0, obuf1, wg_s, wu_s, wd_s, xsem, osem):
    e = pl.program_id(0)
    ch = MOE_TILE
    half = ch // 2
    first, last, count, total = first_ref[e], last_ref[e], count_ref[e], total_ref[0]
    xbufs = (xbuf0, xbuf1)
    obufs = (obuf0, obuf1)

    def x_copy(c, s):
        rows = pl.ds(pl.multiple_of(c * ch, ch), ch)
        return pltpu.make_async_copy(xs_hbm.at[rows], xbufs[s], xsem.at[s])

    def o_copy(c, s):
        rows = pl.ds(pl.multiple_of(c * ch, ch), ch)
        return pltpu.make_async_copy(obufs[s], out_hbm.at[rows], osem.at[s])

    @pl.when(e == 0)
    def _():
        x_copy(0, 0).start()

    @pl.when(last > first)
    def _():
        wg_s[...] = wg_ref[0].astype(BF16)
        wu_s[...] = wu_ref[0].astype(BF16)
        wd_s[...] = wd_ref[0].astype(BF16)

    def experts_rows(s, rows, valid):
        row = lax.broadcasted_iota(jnp.int32, (rows, PACKED), 0)
        words = jnp.where(row < valid, xbufs[s][0:rows, :], 0)
        lo, hi = _unpack_rows(words)
        lo = lo.astype(BF16)
        hi = hi.astype(BF16)
        gate = (jnp.dot(lo, wg_s[:PACKED, :], preferred_element_type=F32)
                + jnp.dot(hi, wg_s[PACKED:, :], preferred_element_type=F32))
        up = (jnp.dot(lo, wu_s[:PACKED, :], preferred_element_type=F32)
              + jnp.dot(hi, wu_s[PACKED:, :], preferred_element_type=F32))
        act = (gate * jax.nn.sigmoid(gate) * up).astype(BF16)
        obufs[s][0:rows, :] = _pack_rows(jnp.dot(act, wd_s[...], preferred_element_type=F32))

    def chunk_in_slot(c, s):
        x_copy(c, s).wait()

        @pl.when(c + 1 < total)
        def _():
            x_copy(c + 1, 1 - s).start()

        @pl.when(c >= 2)
        def _():
            o_copy(c - 2, s).wait()

        valid = count - (c - first) * ch

        @pl.when(valid > half)
        def _():
            experts_rows(s, ch, valid)

        @pl.when(valid <= half)
        def _():
            experts_rows(s, half, valid)
            obufs[s][half:, :] = jnp.zeros((ch - half, PACKED), I32)

        o_copy(c, s).start()

    def chunk(c, carry):
        for s in range(2):
            @pl.when(c % 2 == s)
            def _():
                chunk_in_slot(c, s)
        return carry

    lax.fori_loop(first, last, chunk, 0)

    @pl.when(e == pl.num_programs(0) - 1)
    def _():
        for s in range(2):
            @pl.when(((total - 1) % 2 == s) | (total >= 2))
            def _():
                o_copy(0, s).wait()


def _moe_experts(plan, xs, w_gate, w_up, w_down):
    nslot = xs.shape[0]
    any_spec = pl.BlockSpec(memory_space=pl.ANY)
    grid_spec = pltpu.PrefetchScalarGridSpec(
        num_scalar_prefetch=len(plan),
        grid=(N_EXPERTS,),
        in_specs=[
            any_spec,
            pl.BlockSpec((1, D_MODEL, EXPERT_FF), lambda e, *_: (e, 0, 0)),
            pl.BlockSpec((1, D_MODEL, EXPERT_FF), lambda e, *_: (e, 0, 0)),
            pl.BlockSpec((1, EXPERT_FF, D_MODEL), lambda e, *_: (e, 0, 0)),
        ],
        out_specs=any_spec,
        scratch_shapes=[
            pltpu.VMEM((MOE_TILE, PACKED), I32),
            pltpu.VMEM((MOE_TILE, PACKED), I32),
            pltpu.VMEM((MOE_TILE, PACKED), I32),
            pltpu.VMEM((MOE_TILE, PACKED), I32),
            pltpu.VMEM((D_MODEL, EXPERT_FF), BF16),
            pltpu.VMEM((D_MODEL, EXPERT_FF), BF16),
            pltpu.VMEM((EXPERT_FF, D_MODEL), BF16),
            pltpu.SemaphoreType.DMA((2,)),
            pltpu.SemaphoreType.DMA((2,)),
        ],
    )
    return pl.pallas_call(
        _moe_kernel,
        grid_spec=grid_spec,
        out_shape=jax.ShapeDtypeStruct((nslot, PACKED), I32),
        compiler_params=_ARB,
        name="moe_experts",
    )(*plan, xs, w_gate, w_up, w_down)


def _dispatch_plan(table, counts, tm):
    e_ids = table[0:2].astype(I32)
    ranks = table[4:6].astype(I32)
    counts = counts[0, :N_EXPERTS].astype(I32)
    padded = ((counts + tm - 1) // tm) * tm
    ends = jnp.cumsum(padded)
    starts = ends - padded
    experts = jnp.arange(N_EXPERTS, dtype=I32)
    slot = jnp.sum(jnp.where(e_ids[..., None] == experts, starts, 0), axis=-1) + ranks
    plan = (starts // tm, ends // tm, counts, ends[-1:] // tm)
    return slot, tuple(p.astype(I32) for p in plan)


def _final_kernel(x1_ref, y0_ref, y1_ref, route_ref, g_ref, *rest):
    out_ref = rest[-1]
    route = route_ref[...]
    w1 = route[:, 2:3]
    w2 = route[:, 3:4]
    a_lo, a_hi = _unpack_rows(y0_ref[...])
    b_lo, b_hi = _unpack_rows(y1_ref[...])
    moe = jnp.concatenate([w1 * a_lo + w2 * b_lo, w1 * a_hi + w2 * b_hi], axis=1)
    out_ref[...] = _rms(x1_ref[...] + moe, g_ref[...])


def _final(x1, ycat, route, norm_g, tm, chunk, prev):
    t = x1.shape[0]
    nblk = t // tm // FINAL_CHUNKS
    off = chunk * nblk
    in_specs = [
        pl.BlockSpec((tm, D_MODEL), lambda i: (i + off, 0)),
        pl.BlockSpec((tm, PACKED), lambda i: (i, 0)),
        pl.BlockSpec((tm, PACKED), lambda i: (i + nblk, 0)),
        pl.BlockSpec((tm, LANES), lambda i: (i + off, 0)),
        _const_spec((1, D_MODEL)),
    ]
    args = [x1, ycat, ycat, route, norm_g]
    aliases = {}
    if prev is not None:
        in_specs.append(pl.BlockSpec(memory_space=pl.ANY))
        args.append(prev)
        aliases = {len(args) - 1: 0}
    return pl.pallas_call(
        _final_kernel,
        grid=(nblk,),
        in_specs=in_specs,
        out_specs=pl.BlockSpec((tm, D_MODEL), lambda i: (i + off, 0)),
        out_shape=jax.ShapeDtypeStruct((t, D_MODEL), F32),
        input_output_aliases=aliases,
        compiler_params=_ARB,
        name="final_norm",
    )(*args)


def kernel(x, norm_mix, w_in, conv_w, conv_b, w_conv_out, ssm_a_re, ssm_a_im, ssm_log_dt,
           ssm_b_re, ssm_b_im, ssm_c_re, ssm_c_im, ssm_d, w_glu_a, w_glu_b, gate_bias, w_o,
           norm_ffn, w_route_group, b_route_group, w_route_expert, b_route_expert,
           w_gate, w_up, w_down, norm_final):
    bsz, length, d = x.shape
    assert d == D_MODEL and norm_mix.shape[0] == 1
    t = bsz * length
    tm = TOKEN_TILE
    assert bsz == 1 and t % (tm * FINAL_CHUNKS) == 0
    xt = x.reshape(t, d)
    row = lambda a: a.reshape(1, -1).astype(F32)

    s5_weights = _s5_weights(
        ssm_a_re[0], ssm_a_im[0], ssm_log_dt[0], ssm_b_re[0], ssm_b_im[0],
        ssm_c_re[0], ssm_c_im[0])
    ma, gb, yg = _mixer_front(
        xt, row(norm_mix[0]), w_in[0].astype(BF16), conv_w[0].reshape(3, CONV_WIDTH),
        row(conv_b[0]), w_conv_out[0].astype(BF16), gate_bias[0], s5_weights, row(ssm_d[0]), tm)

    pad = LANES - N_EXPERTS - N_EXPERT_GROUPS
    w_route = jnp.concatenate(
        [w_route_expert[0], w_route_group[0], jnp.zeros((d, pad), F32)], axis=1)
    w_route_hi = w_route.astype(BF16)
    w_route_lo = (w_route - w_route_hi.astype(F32)).astype(BF16)
    b_route = jnp.concatenate(
        [b_route_expert[0], b_route_group[0], jnp.zeros((pad,), F32)]).reshape(1, LANES)
    x1, h2p, route, table, counts = _mixer_back(
        yg, ma, gb, xt, w_glu_a[0].astype(BF16), w_glu_b[0].astype(BF16), w_o[0].astype(BF16),
        row(norm_ffn[0]), jnp.concatenate([w_route_hi, w_route_lo], axis=1), b_route, tm)

    slot, plan = _dispatch_plan(table, counts, MOE_TILE)
    nslot = 2 * t + N_EXPERTS * MOE_TILE
    xs = _sc_dispatch(h2p, slot[0], slot[1], nslot)
    ys = _moe_experts(plan, xs, w_gate[0], w_up[0], w_down[0])
    out = None
    tc = t // FINAL_CHUNKS
    for k in range(FINAL_CHUNKS):
        ycat = _sc_gather(ys, slot[:, k * tc:(k + 1) * tc].reshape(-1))
        out = _final(x1, ycat, route, row(norm_final), tm, k, out)
    return out.reshape(bsz, length, d)
```

```python
import functools

import jax
import jax.numpy as jnp
from jax import lax
from jax.experimental import pallas as pl
from jax.experimental.pallas import tpu as pltpu
from jax.experimental.pallas import tpu_sc as plsc

F32 = jnp.float32
BF16 = jnp.bfloat16
I32 = jnp.int32

D_MODEL = 1024
CONV_WIDTH = 1024
SSM_WIDTH = 512
SSM_GROUP = 16
SSM_GROUPS = 32
SSM_STATE = 64
N_EXPERT_GROUPS = 4
EXPERTS_PER_GROUP = 8
N_EXPERTS = 32
EXPERT_FF = 512
EPS = 1e-6

LANES = 128
MXU_DIM = 256
TOKEN_TILE = 512
MOE_TILE = 512
SCAN_PITCH = 36
STATE_ROWS = 2 * SSM_GROUPS * SSM_STATE // LANES
HALF_ROWS = STATE_ROWS // 2
ROUTE_GROUP_LANE = N_EXPERTS
ROUTE_ROWS = 8
FINAL_CHUNKS = 4
PACKED = D_MODEL // 2
HI_MASK = -65536
SC_CORES = 2
SC_SUBCORES = 16
SC_CHUNK = 128
VMEM_LIMIT = 56 * 1024 * 1024

_ARB = pltpu.CompilerParams(dimension_semantics=("arbitrary",), vmem_limit_bytes=VMEM_LIMIT)


def _const_spec(shape):
    nd = len(shape)
    return pl.BlockSpec(shape, lambda i, *_: (0,) * nd, pipeline_mode=pl.Buffered(1))


def _row_spec(tm, width):
    return pl.BlockSpec((tm, width), lambda i, *_: (i, 0))


def _rms(x, g):
    ms = jnp.mean(x * x, axis=-1, keepdims=True)
    return x * lax.rsqrt(ms + EPS) * g


def _pack_rows(v):
    bits = lax.bitcast_convert_type(v.astype(BF16).astype(F32), I32)
    lo = lax.shift_right_logical(bits[:, :PACKED], 16)
    hi = bits[:, PACKED:] & HI_MASK
    return lo | hi


def _unpack_rows(w):
    lo = lax.bitcast_convert_type(lax.shift_left(w, 16), F32)
    hi = lax.bitcast_convert_type(w & HI_MASK, F32)
    return lo, hi


def _front_kernel(x_ref, g_ref, win_ref, cw_ref, cb_ref, wco_ref, gbias_ref,
                  bre_ref, bim_ref, cw2_ref, are_ref, aim_ref, d_ref,
                  ma_ref, gb_ref, y_ref, carry_ref, u_ref, r_ref, state_ref):
    tm = x_ref.shape[0]
    c0, c1, c2 = CONV_WIDTH, 2 * CONV_WIDTH, 3 * CONV_WIDTH
    c3 = c2 + SSM_WIDTH

    @pl.when(pl.program_id(0) == 0)
    def _():
        carry_ref[...] = jnp.zeros_like(carry_ref)
        u_ref[...] = jnp.zeros_like(u_ref)
        state_ref[...] = jnp.zeros_like(state_ref)

    h = _rms(x_ref[...], g_ref[...]).astype(BF16)
    u_prev = u_ref[...]
    _s5_scan_tile(u_prev, bre_ref, bim_ref, are_ref, aim_ref, r_ref, state_ref)

    def proj(lo, hi):
        return jnp.dot(h, win_ref[:, lo:hi], preferred_element_type=F32)

    v = proj(c1, c2) * proj(0, c0)
    row = lax.broadcasted_iota(jnp.int32, v.shape, 0)
    prev1 = carry_ref[7:8, :]
    prev2 = carry_ref[6:7, :]
    v1 = jnp.where(row == 0, prev1, pltpu.roll(v, 1, 0))
    v2 = jnp.where(row == 0, prev2, jnp.where(row == 1, prev1, pltpu.roll(v, 2, 0)))
    carry_ref[...] = v[tm - 8:, :]
    y = cw_ref[0:1, :] * v2 + cw_ref[1:2, :] * v1 + cw_ref[2:3, :] * v + cb_ref[...]
    z = (proj(c0, c1) * y).astype(BF16)
    ya = jnp.dot(z, wco_ref[...], preferred_element_type=F32)
    ga = jax.nn.sigmoid(proj(c3, c3 + D_MODEL) + gbias_ref[0:1, :])
    ma_ref[...] = (ga * ya).astype(BF16)
    gb = jax.nn.sigmoid(proj(c3 + D_MODEL, c3 + 2 * D_MODEL) + gbias_ref[1:2, :])
    gb_ref[...] = gb.astype(BF16)
    u_new = proj(c2, c3).astype(BF16)
    _s5_readout(u_prev, cw2_ref, d_ref, y_ref, r_ref)
    u_ref[...] = u_new


def _mixer_front(x, norm_g, w_in, conv_w, conv_b, w_conv_out, gate_bias, s5_weights, d_skip, tm):
    bre, bim, cw, a_re, a_im = s5_weights
    t = x.shape[0]
    in_cols = w_in.shape[1]
    n = t // tm
    last = lambda i: (jnp.minimum(i, n - 1), 0)
    lag = lambda i: (jnp.maximum(i - 1, 0), 0)
    return pl.pallas_call(
        _front_kernel,
        grid=(n + 1,),
        in_specs=[
            pl.BlockSpec((tm, D_MODEL), last),
            _const_spec((1, D_MODEL)),
            _const_spec((D_MODEL, in_cols)),
            _const_spec((3, CONV_WIDTH)),
            _const_spec((1, CONV_WIDTH)),
            _const_spec((CONV_WIDTH, D_MODEL)),
            _const_spec((2, D_MODEL)),
            _const_spec(bre.shape),
            _const_spec(bim.shape),
            _const_spec(cw.shape),
            _const_spec(a_re.shape),
            _const_spec(a_im.shape),
            _const_spec((1, SSM_WIDTH)),
        ],
        out_specs=[_row_spec(tm, D_MODEL), _row_spec(tm, D_MODEL),
                   pl.BlockSpec((tm, SSM_WIDTH), lag)],
        out_shape=[
            jax.ShapeDtypeStruct((t + tm, D_MODEL), BF16),
            jax.ShapeDtypeStruct((t + tm, D_MODEL), BF16),
            jax.ShapeDtypeStruct((t, SSM_WIDTH), BF16),
        ],
        scratch_shapes=[
            pltpu.VMEM((8, CONV_WIDTH), F32),
            pltpu.VMEM((tm, SSM_WIDTH), BF16),
            pltpu.VMEM((tm * SCAN_PITCH, LANES), F32),
            pltpu.VMEM((STATE_ROWS, LANES), F32),
        ],
        compiler_params=_ARB,
        name="mixer_front",
    )(x, norm_g, w_in, conv_w, conv_b, w_conv_out, gate_bias, bre, bim, cw, a_re, a_im, d_skip)


def _s5_scan_tile(u, bre_ref, bim_ref, are_ref, aim_ref, r_ref, state_ref):
    tm = u.shape[0]
    tiles_per_half = MXU_DIM * 4 // LANES

    for k in range(2):
        uk = u[:, k * MXU_DIM:(k + 1) * MXU_DIM]
        re = jnp.dot(uk, bre_ref[k], preferred_element_type=F32)
        im = jnp.dot(uk, bim_ref[k], preferred_element_type=F32)
        for jj in range(tiles_per_half):
            j = k * tiles_per_half + jj
            sl = slice(jj * LANES, (jj + 1) * LANES)
            r_ref[pl.ds(j, tm, stride=SCAN_PITCH), :] = re[:, sl]
            r_ref[pl.ds(HALF_ROWS + j, tm, stride=SCAN_PITCH), :] = im[:, sl]

    a_re = are_ref[...]
    a_im = aim_ref[...]

    s_re = state_ref[0:HALF_ROWS, :]
    s_im = state_ref[HALF_ROWS:, :]
    for t in range(tm):
        base = t * SCAN_PITCH
        b_re = r_ref[pl.ds(base, HALF_ROWS), :]
        b_im = r_ref[pl.ds(base + HALF_ROWS, HALF_ROWS), :]
        s_re, s_im = (a_re * s_re - a_im * s_im + b_re,
                      a_re * s_im + a_im * s_re + b_im)
        r_ref[pl.ds(base, HALF_ROWS), :] = s_re
        r_ref[pl.ds(base + HALF_ROWS, HALF_ROWS), :] = s_im
    state_ref[0:HALF_ROWS, :] = s_re
    state_ref[HALF_ROWS:, :] = s_im


def _s5_readout(u, cw_ref, d_ref, y_ref, r_ref):
    tm = u.shape[0]
    tiles_per_half = MXU_DIM * 4 // LANES
    ys = []
    for k in range(2):
        cols = []
        for half in range(2):
            for jj in range(tiles_per_half):
                j = half * HALF_ROWS + k * tiles_per_half + jj
                cols.append(r_ref[pl.ds(j, tm, stride=SCAN_PITCH), :])
        s = jnp.concatenate(cols, axis=1).astype(BF16)
        ys.append(jnp.dot(s, cw_ref[k], preferred_element_type=F32))
    y = jnp.concatenate(ys, axis=1) + d_ref[...] * u.astype(F32)
    y_ref[...] = jax.nn.gelu(y).astype(BF16)


def _s5_weights(a_re, a_im, log_dt, b_re, b_im, c_re, c_im):
    g, n, h = SSM_GROUPS, SSM_STATE, SSM_GROUP
    gpt = MXU_DIM // h
    lam = lax.complex(a_re.astype(F32), a_im.astype(F32))
    dt = jnp.exp(log_dt.astype(F32))[:, None]
    a_bar = jnp.exp(lam * dt)
    b_bar = ((a_bar - 1.0) / lam)[..., None] * lax.complex(b_re.astype(F32), b_im.astype(F32))
    eye = jnp.eye(gpt, dtype=F32)

    def blk_in(w):
        w = w.reshape(g // gpt, gpt, n, h).transpose(0, 1, 3, 2)
        return jnp.einsum("kghn,gf->kghfn", w, eye).reshape(g // gpt, gpt * h, gpt * n)

    def blk_out(c):
        c = c.astype(F32).reshape(g // gpt, gpt, h, n).transpose(0, 1, 3, 2)
        return jnp.einsum("kgnh,gf->kgnfh", c, eye).reshape(g // gpt, gpt * n, gpt * h)

    bre = blk_in(jnp.real(b_bar)).astype(BF16)
    bim = blk_in(jnp.imag(b_bar)).astype(BF16)
    cw = jnp.concatenate([blk_out(c_re), -blk_out(c_im)], axis=1).astype(BF16)
    tile = (HALF_ROWS, LANES)
    return bre, bim, cw, jnp.real(a_bar).reshape(tile), jnp.imag(a_bar).reshape(tile)


def _back_kernel(yg_ref, ma_ref, gb_ref, x_ref, wa_ref, wb_ref, wo_ref, g2_ref, wr_ref, br_ref,
                 x1_ref, h2p_ref, route_ref, table_ref, counts_ref, h2_ref):
    tm = x_ref.shape[0]
    step = pl.program_id(0)

    @pl.when(step == 0)
    def _():
        counts_ref[...] = jnp.zeros_like(counts_ref)
        h2_ref[...] = jnp.zeros_like(h2_ref)

    h2_prev = h2_ref[...]
    h2_hi = h2_prev.astype(BF16)
    h2_lo = (h2_prev - h2_hi.astype(F32)).astype(BF16)
    both = jnp.dot(h2_hi, wr_ref[...], preferred_element_type=F32)
    logits = (both[:, :LANES]
              + (both[:, LANES:]
                 + jnp.dot(h2_lo, wr_ref[:, :LANES], preferred_element_type=F32))) + br_ref[...]
    lane = lax.broadcasted_iota(jnp.int32, logits.shape, 1)
    lane_f = lane.astype(F32)
    neg = jnp.float32(-jnp.inf)
    big = jnp.float32(LANES)
    is_grp = (lane >= ROUTE_GROUP_LANE) & (lane < ROUTE_GROUP_LANE + N_EXPERT_GROUPS)
    gl = jnp.where(is_grp, logits, neg)
    gmax = jnp.max(gl, axis=1, keepdims=True)
    gidx = jnp.min(jnp.where(gl == gmax, lane_f - ROUTE_GROUP_LANE, big), axis=1, keepdims=True)
    pg_top = 1.0 / jnp.sum(jnp.exp(gl - gmax), axis=1, keepdims=True)
    lane_grp = (lane // EXPERTS_PER_GROUP).astype(F32)
    el = jnp.where((lane < N_EXPERTS) & (lane_grp == gidx), logits, neg)
    l1 = jnp.max(el, axis=1, keepdims=True)
    i1 = jnp.min(jnp.where(el == l1, lane_f, big), axis=1, keepdims=True)
    el2 = jnp.where(lane_f == i1, neg, el)
    l2 = jnp.max(el2, axis=1, keepdims=True)
    i2 = jnp.min(jnp.where(el2 == l2, lane_f, big), axis=1, keepdims=True)
    r = jnp.exp(l2 - l1)
    w1 = pg_top / (1.0 + r)
    w2 = pg_top * r / (1.0 + r)
    oh1 = (lane_f == i1).astype(F32)
    oh2 = (lane_f == i2).astype(F32)
    picked = oh1 + oh2

    yg = yg_ref[...]
    yb = (jnp.dot(yg, wa_ref[...], preferred_element_type=F32)
          * jax.nn.sigmoid(jnp.dot(yg, wb_ref[...], preferred_element_type=F32)))
    mixed = (ma_ref[...].astype(F32) + gb_ref[...].astype(F32) * yb).astype(BF16)
    x1 = x_ref[...] + jnp.dot(mixed, wo_ref[...], preferred_element_type=F32)
    x1_ref[...] = x1
    h2 = _rms(x1, g2_ref[...])
    h2p_ref[...] = _pack_rows(h2)
    h2_ref[...] = h2

    rr = lax.broadcasted_iota(jnp.int32, (tm, tm), 0)
    cc = lax.broadcasted_iota(jnp.int32, (tm, tm), 1)
    before = (cc < rr).astype(BF16)
    prior = jnp.dot(before, picked.astype(BF16), preferred_element_type=F32) + counts_ref[...]
    rank1 = jnp.sum(oh1 * prior, axis=1, keepdims=True)
    rank2 = jnp.sum(oh2 * prior, axis=1, keepdims=True)
    counts_ref[...] += jnp.where(step > 0, jnp.sum(picked, axis=0, keepdims=True), 0.0)

    route = jnp.where(
        lane == 0, i1, jnp.where(lane == 1, i2, jnp.where(lane == 2, w1, jnp.where(
            lane == 3, w2, jnp.where(lane == 4, rank1, jnp.where(lane == 5, rank2, 0.0))))))
    route_ref[...] = route
    table_ref[...] = route.T[0:ROUTE_ROWS, :]


def _mixer_back(yg, ma, gb, x, w_glu_a, w_glu_b, w_o, norm_g, w_route, b_route, tm):
    t = x.shape[0]
    n = t // tm
    cur = lambda i: (jnp.minimum(i, n - 1), 0)
    lag = lambda i: (jnp.maximum(i - 1, 0), 0)
    return pl.pallas_call(
        _back_kernel,
        grid=(n + 1,),
        in_specs=[
            pl.BlockSpec((tm, SSM_WIDTH), cur),
            pl.BlockSpec((tm, D_MODEL), cur),
            pl.BlockSpec((tm, D_MODEL), cur),
            pl.BlockSpec((tm, D_MODEL), cur),
            _const_spec((SSM_WIDTH, D_MODEL)),
            _const_spec((SSM_WIDTH, D_MODEL)),
            _const_spec((D_MODEL, D_MODEL)),
            _const_spec((1, D_MODEL)),
            _const_spec((D_MODEL, 2 * LANES)),
            _const_spec((1, LANES)),
        ],
        out_specs=[pl.BlockSpec((tm, D_MODEL), cur), pl.BlockSpec((tm, PACKED), cur),
                   pl.BlockSpec((tm, LANES), lag),
                   pl.BlockSpec((ROUTE_ROWS, tm), lambda i: (0, jnp.maximum(i - 1, 0))),
                   pl.BlockSpec((1, LANES), lambda i: (0, 0))],
        out_shape=[
            jax.ShapeDtypeStruct((t, D_MODEL), F32),
            jax.ShapeDtypeStruct((t, PACKED), I32),
            jax.ShapeDtypeStruct((t, LANES), F32),
            jax.ShapeDtypeStruct((ROUTE_ROWS, t), F32),
            jax.ShapeDtypeStruct((1, LANES), F32),
        ],
        scratch_shapes=[pltpu.VMEM((tm, D_MODEL), F32)],
        compiler_params=_ARB,
        name="mixer_back",
    )(yg, ma, gb, x, w_glu_a, w_glu_b, w_o, norm_g, w_route, b_route)


_SC_MESH = dict(core_axis_name="c", subcore_axis_name="s")


def _sc_worker():
    return lax.axis_index("s") * SC_CORES + lax.axis_index("c")


def _sc_dispatch(rows, slot0, slot1, nslot):
    t, width = rows.shape
    per_w = t // (SC_CORES * SC_SUBCORES)
    assert per_w % SC_CHUNK == 0

    @functools.partial(
        pl.kernel,
        out_type=jax.ShapeDtypeStruct((nslot, width), rows.dtype),
        mesh=plsc.VectorSubcoreMesh(**_SC_MESH),
        scratch_types=[
            pltpu.VMEM((1, SC_CHUNK), I32),
            pltpu.VMEM((1, SC_CHUNK), I32),
            pltpu.VMEM((SC_CHUNK, width), rows.dtype),
        ],
        name="moe_dispatch",
    )
    def k(rows_hbm, s0_hbm, s1_hbm, out_hbm, i0_v, i1_v, rows_v):
        base = _sc_worker() * per_w

        @pl.loop(0, per_w // SC_CHUNK)
        def _(c):
            off = base + c * SC_CHUNK
            pltpu.sync_copy(s0_hbm.at[:, pl.ds(off, SC_CHUNK)], i0_v)
            pltpu.sync_copy(s1_hbm.at[:, pl.ds(off, SC_CHUNK)], i1_v)
            pltpu.sync_copy(rows_hbm.at[pl.ds(off, SC_CHUNK)], rows_v)
            pltpu.sync_copy(rows_v, out_hbm.at[i0_v.at[0]])
            pltpu.sync_copy(rows_v, out_hbm.at[i1_v.at[0]])

    return k(rows, slot0.reshape(1, t), slot1.reshape(1, t))


def _sc_gather(table, idx):
    n = idx.shape[0]
    width = table.shape[1]
    per_w = n // (SC_CORES * SC_SUBCORES)
    assert per_w % SC_CHUNK == 0

    @functools.partial(
        pl.kernel,
        out_type=jax.ShapeDtypeStruct((n, width), table.dtype),
        mesh=plsc.VectorSubcoreMesh(**_SC_MESH),
        scratch_types=[
            pltpu.VMEM((1, SC_CHUNK), I32),
            pltpu.VMEM((SC_CHUNK, width), table.dtype),
        ],
        name="moe_combine_gather",
    )
    def k(table_hbm, idx_hbm, out_hbm, idx_v, rows_v):
        base = _sc_worker() * per_w

        @pl.loop(0, per_w // SC_CHUNK)
        def _(c):
            off = base + c * SC_CHUNK
            pltpu.sync_copy(idx_hbm.at[:, pl.ds(off, SC_CHUNK)], idx_v)
            pltpu.sync_copy(table_hbm.at[idx_v.at[0]], rows_v)
            pltpu.sync_copy(rows_v, out_hbm.at[pl.ds(off, SC_CHUNK)])

    return k(table, idx.reshape(1, n))


def _moe_kernel(first_ref, last_ref, count_ref, total_ref,
                xs_hbm, wg_ref, wu_ref, wd_ref, out_hbm,
                xbuf0, xbuf1, obuf0, obuf1, wg_s, wu_s, wd_s, xsem, osem):
    e = pl.program_id(0)
    ch = MOE_TILE
    half = ch // 2
    first, last, count, total = first_ref[e], last_ref[e], count_ref[e], total_ref[0]
    xbufs = (xbuf0, xbuf1)
    obufs = (obuf0, obuf1)

    def x_copy(c, s):
        rows = pl.ds(pl.multiple_of(c * ch, ch), ch)
        return pltpu.make_async_copy(xs_hbm.at[rows], xbufs[s], xsem.at[s])

    def o_copy(c, s):
        rows = pl.ds(pl.multiple_of(c * ch, ch), ch)
        return pltpu.make_async_copy(obufs[s], out_hbm.at[rows], osem.at[s])

    @pl.when(e == 0)
    def _():
        x_copy(0, 0).start()

    @pl.when(last > first)
    def _():
        wg_s[...] = wg_ref[0].astype(BF16)
        wu_s[...] = wu_ref[0].astype(BF16)
        wd_s[...] = wd_ref[0].astype(BF16)

    def experts_rows(s, rows, valid):
        row = lax.broadcasted_iota(jnp.int32, (rows, PACKED), 0)
        words = jnp.where(row < valid, xbufs[s][0:rows, :], 0)
        lo, hi = _unpack_rows(words)
        lo = lo.astype(BF16)
        hi = hi.astype(BF16)
        gate = (jnp.dot(lo, wg_s[:PACKED, :], preferred_element_type=F32)
                + jnp.dot(hi, wg_s[PACKED:, :], preferred_element_type=F32))
        up = (jnp.dot(lo, wu_s[:PACKED, :], preferred_element_type=F32)
              + jnp.dot(hi, wu_s[PACKED:, :], preferred_element_type=F32))
        act = (gate * jax.nn.sigmoid(gate) * up).astype(BF16)
        obufs[s][0:rows, :] = _pack_rows(jnp.dot(act, wd_s[...], preferred_element_type=F32))

    def chunk_in_slot(c, s):
        x_copy(c, s).wait()

        @pl.when(c + 1 < total)
        def _():
            x_copy(c + 1, 1 - s).start()

        @pl.when(c >= 2)
        def _():
            o_copy(c - 2, s).wait()

        valid = count - (c - first) * ch

        @pl.when(valid > half)
        def _():
            experts_rows(s, ch, valid)

        @pl.when(valid <= half)
        def _():
            experts_rows(s, half, valid)
            obufs[s][half:, :] = jnp.zeros((ch - half, PACKED), I32)

        o_copy(c, s).start()

    def chunk(c, carry):
        for s in range(2):
            @pl.when(c % 2 == s)
            def _():
                chunk_in_slot(c, s)
        return carry

    lax.fori_loop(first, last, chunk, 0)

    @pl.when(e == pl.num_programs(0) - 1)
    def _():
        for s in range(2):
            @pl.when(((total - 1) % 2 == s) | (total >= 2))
            def _():
                o_copy(0, s).wait()


def _moe_experts(plan, xs, w_gate, w_up, w_down):
    nslot = xs.shape[0]
    any_spec = pl.BlockSpec(memory_space=pl.ANY)
    grid_spec = pltpu.PrefetchScalarGridSpec(
        num_scalar_prefetch=len(plan),
        grid=(N_EXPERTS,),
        in_specs=[
            any_spec,
            pl.BlockSpec((1, D_MODEL, EXPERT_FF), lambda e, *_: (e, 0, 0)),
            pl.BlockSpec((1, D_MODEL, EXPERT_FF), lambda e, *_: (e, 0, 0)),
            pl.BlockSpec((1, EXPERT_FF, D_MODEL), lambda e, *_: (e, 0, 0)),
        ],
        out_specs=any_spec,
        scratch_shapes=[
            pltpu.VMEM((MOE_TILE, PACKED), I32),
            pltpu.VMEM((MOE_TILE, PACKED), I32),
            pltpu.VMEM((MOE_TILE, PACKED), I32),
            pltpu.VMEM((MOE_TILE, PACKED), I32),
            pltpu.VMEM((D_MODEL, EXPERT_FF), BF16),
            pltpu.VMEM((D_MODEL, EXPERT_FF), BF16),
            pltpu.VMEM((EXPERT_FF, D_MODEL), BF16),
            pltpu.SemaphoreType.DMA((2,)),
            pltpu.SemaphoreType.DMA((2,)),
        ],
    )
    return pl.pallas_call(
        _moe_kernel,
        grid_spec=grid_spec,
        out_shape=jax.ShapeDtypeStruct((nslot, PACKED), I32),
        compiler_params=_ARB,
        name="moe_experts",
    )(*plan, xs, w_gate, w_up, w_down)


def _dispatch_plan(table, counts, tm):
    e_ids = table[0:2].astype(I32)
    ranks = table[4:6].astype(I32)
    counts = counts[0, :N_EXPERTS].astype(I32)
    padded = ((counts + tm - 1) // tm) * tm
    ends = jnp.cumsum(padded)
    starts = ends - padded
    experts = jnp.arange(N_EXPERTS, dtype=I32)
    slot = jnp.sum(jnp.where(e_ids[..., None] == experts, starts, 0), axis=-1) + ranks
    plan = (starts // tm, ends // tm, counts, ends[-1:] // tm)
    return slot, tuple(p.astype(I32) for p in plan)


def _final_kernel(x1_ref, y0_ref, y1_ref, route_ref, g_ref, *rest):
    out_ref = rest[-1]
    route = route_ref[...]
    w1 = route[:, 2:3]
    w2 = route[:, 3:4]
    a_lo, a_hi = _unpack_rows(y0_ref[...])
    b_lo, b_hi = _unpack_rows(y1_ref[...])
    moe = jnp.concatenate([w1 * a_lo + w2 * b_lo, w1 * a_hi + w2 * b_hi], axis=1)
    out_ref[...] = _rms(x1_ref[...] + moe, g_ref[...])


def _final(x1, ycat, route, norm_g, tm, chunk, prev):
    t = x1.shape[0]
    nblk = t // tm // FINAL_CHUNKS
    off = chunk * nblk
    in_specs = [
        pl.BlockSpec((tm, D_MODEL), lambda i: (i + off, 0)),
        pl.BlockSpec((tm, PACKED), lambda i: (i, 0)),
        pl.BlockSpec((tm, PACKED), lambda i: (i + nblk, 0)),
        pl.BlockSpec((tm, LANES), lambda i: (i + off, 0)),
        _const_spec((1, D_MODEL)),
    ]
    args = [x1, ycat, ycat, route, norm_g]
    aliases = {}
    if prev is not None:
        in_specs.append(pl.BlockSpec(memory_space=pl.ANY))
        args.append(prev)
        aliases = {len(args) - 1: 0}
    return pl.pallas_call(
        _final_kernel,
        grid=(nblk,),
        in_specs=in_specs,
        out_specs=pl.BlockSpec((tm, D_MODEL), lambda i: (i + off, 0)),
        out_shape=jax.ShapeDtypeStruct((t, D_MODEL), F32),
        input_output_aliases=aliases,
        compiler_params=_ARB,
        name="final_norm",
    )(*args)


def kernel(x, norm_mix, w_in, conv_w, conv_b, w_conv_out, ssm_a_re, ssm_a_im, ssm_log_dt,
           ssm_b_re, ssm_b_im, ssm_c_re, ssm_c_im, ssm_d, w_glu_a, w_glu_b, gate_bias, w_o,
           norm_ffn, w_route_group, b_route_group, w_route_expert, b_route_expert,
           w_gate, w_up, w_down, norm_final):
    bsz, length, d = x.shape
    assert d == D_MODEL and norm_mix.shape[0] == 1
    t = bsz * length
    tm = TOKEN_TILE
    assert bsz == 1 and t % (tm * FINAL_CHUNKS) == 0
    xt = x.reshape(t, d)
    row = lambda a: a.reshape(1, -1).astype(F32)

    s5_weights = _s5_weights(
        ssm_a_re[0], ssm_a_im[0], ssm_log_dt[0], ssm_b_re[0], ssm_b_im[0],
        ssm_c_re[0], ssm_c_im[0])
    ma, gb, yg = _mixer_front(
        xt, row(norm_mix[0]), w_in[0].astype(BF16), conv_w[0].reshape(3, CONV_WIDTH),
        row(conv_b[0]), w_conv_out[0].astype(BF16), gate_bias[0], s5_weights, row(ssm_d[0]), tm)

    pad = LANES - N_EXPERTS - N_EXPERT_GROUPS
    w_route = jnp.concatenate(
        [w_route_expert[0], w_route_group[0], jnp.zeros((d, pad), F32)], axis=1)
    w_route_hi = w_route.astype(BF16)
    w_route_lo = (w_route - w_route_hi.astype(F32)).astype(BF16)
    b_route = jnp.concatenate(
        [b_route_expert[0], b_route_group[0], jnp.zeros((pad,), F32)]).reshape(1, LANES)
    x1, h2p, route, table, counts = _mixer_back(
        yg, ma, gb, xt, w_glu_a[0].astype(BF16), w_glu_b[0].astype(BF16), w_o[0].astype(BF16),
        row(norm_ffn[0]), jnp.concatenate([w_route_hi, w_route_lo], axis=1), b_route, tm)

    slot, plan = _dispatch_plan(table, counts, MOE_TILE)
    nslot = 2 * t + N_EXPERTS * MOE_TILE
    xs = _sc_dispatch(h2p, slot[0], slot[1], nslot)
    ys = _moe_experts(plan, xs, w_gate[0], w_up[0], w_down[0])
    out = None
    tc = t // FINAL_CHUNKS
    for k in range(FINAL_CHUNKS):
        ycat = _sc_gather(ys, slot[:, k * tc:(k + 1) * tc].reshape(-1))
        out = _final(x1, ycat, route, row(norm_final), tm, k, out)
    return out.reshape(bsz, length, d)
```

```python
import functools

import jax
import jax.numpy as jnp
from jax import lax
from jax.experimental import pallas as pl
from jax.experimental.pallas import tpu as pltpu
from jax.experimental.pallas import tpu_sc as plsc

F32 = jnp.float32
BF16 = jnp.bfloat16
I32 = jnp.int32

D_MODEL = 1024
CONV_WIDTH = 1024
SSM_WIDTH = 512
SSM_GROUP = 16
SSM_GROUPS = 32
SSM_STATE = 64
N_EXPERT_GROUPS = 4
EXPERTS_PER_GROUP = 8
N_EXPERTS = 32
EXPERT_FF = 512
EPS = 1e-6

LANES = 128
MXU_DIM = 256
TOKEN_TILE = 512
FRONT_TILE = 256
MOE_TILE = 512
SCAN_PITCH = 36
STATE_ROWS = 2 * SSM_GROUPS * SSM_STATE // LANES
HALF_ROWS = STATE_ROWS // 2
ROUTE_GROUP_LANE = N_EXPERTS
ROUTE_ROWS = 8
FINAL_CHUNKS = 4
PACKED = D_MODEL // 2
HI_MASK = -65536
SC_CORES = 2
SC_SUBCORES = 16
SC_CHUNK = 128
VMEM_LIMIT = 56 * 1024 * 1024

_ARB = pltpu.CompilerParams(dimension_semantics=("arbitrary",), vmem_limit_bytes=VMEM_LIMIT)


def _const_spec(shape):
    nd = len(shape)
    return pl.BlockSpec(shape, lambda i, *_: (0,) * nd, pipeline_mode=pl.Buffered(1))


def _row_spec(tm, width):
    return pl.BlockSpec((tm, width), lambda i, *_: (i, 0))


def _rms(x, g):
    ms = jnp.mean(x * x, axis=-1, keepdims=True)
    return x * lax.rsqrt(ms + EPS) * g


def _pack_rows(v):
    bits = lax.bitcast_convert_type(v.astype(BF16).astype(F32), I32)
    lo = lax.shift_right_logical(bits[:, :PACKED], 16)
    hi = bits[:, PACKED:] & HI_MASK
    return lo | hi


def _unpack_rows(w):
    lo = lax.bitcast_convert_type(lax.shift_left(w, 16), F32)
    hi = lax.bitcast_convert_type(w & HI_MASK, F32)
    return lo, hi


def _front_kernel(x_ref, g_ref, win_ref, cw_ref, cb_ref, wco_ref, gbias_ref,
                  bre_ref, bim_ref, cw2_ref, are_ref, aim_ref, d_ref, wg_ref, wu_ref, wd_ref,
                  ma_ref, gb_ref, y_ref, wgb_ref, wub_ref, wdb_ref,
                  carry_ref, u_ref, r_ref, state_ref):
    tm = x_ref.shape[0]
    wgb_ref[...] = wg_ref[...].astype(BF16)
    wub_ref[...] = wu_ref[...].astype(BF16)
    wdb_ref[...] = wd_ref[...].astype(BF16)
    c0, c1, c2 = CONV_WIDTH, 2 * CONV_WIDTH, 3 * CONV_WIDTH
    c3 = c2 + SSM_WIDTH

    @pl.when(pl.program_id(0) == 0)
    def _():
        carry_ref[...] = jnp.zeros_like(carry_ref)
        u_ref[...] = jnp.zeros_like(u_ref)
        state_ref[...] = jnp.zeros_like(state_ref)

    h = _rms(x_ref[...], g_ref[...]).astype(BF16)
    u_prev = u_ref[...]
    _s5_scan_tile(u_prev, bre_ref, bim_ref, are_ref, aim_ref, r_ref, state_ref)

    def proj(lo, hi):
        return jnp.dot(h, win_ref[:, lo:hi], preferred_element_type=F32)

    v = proj(c1, c2) * proj(0, c0)
    row = lax.broadcasted_iota(jnp.int32, v.shape, 0)
    prev1 = carry_ref[7:8, :]
    prev2 = carry_ref[6:7, :]
    v1 = jnp.where(row == 0, prev1, pltpu.roll(v, 1, 0))
    v2 = jnp.where(row == 0, prev2, jnp.where(row == 1, prev1, pltpu.roll(v, 2, 0)))
    carry_ref[...] = v[tm - 8:, :]
    y = cw_ref[0:1, :] * v2 + cw_ref[1:2, :] * v1 + cw_ref[2:3, :] * v + cb_ref[...]
    z = (proj(c0, c1) * y).astype(BF16)
    ya = jnp.dot(z, wco_ref[...], preferred_element_type=F32)
    ga = jax.nn.sigmoid(proj(c3, c3 + D_MODEL) + gbias_ref[0:1, :])
    ma_ref[...] = (ga * ya).astype(BF16)
    gb = jax.nn.sigmoid(proj(c3 + D_MODEL, c3 + 2 * D_MODEL) + gbias_ref[1:2, :])
    gb_ref[...] = gb.astype(BF16)
    u_new = proj(c2, c3).astype(BF16)
    _s5_readout(u_prev, cw2_ref, d_ref, y_ref, r_ref)
    u_ref[...] = u_new


def _mixer_front(x, norm_g, w_in, conv_w, conv_b, w_conv_out, gate_bias, s5_weights, d_skip,
                 w_gate, w_up, w_down, tm):
    bre, bim, cw, a_re, a_im = s5_weights
    t = x.shape[0]
    in_cols = w_in.shape[1]
    n = t // tm
    last = lambda i: (jnp.minimum(i, n - 1), 0)
    lag = lambda i: (jnp.maximum(i - 1, 0), 0)
    split = max(k for k in (1, 2, 4, 8) if N_EXPERTS * k <= n + 1)
    piece = lambda i: (jnp.minimum(i, N_EXPERTS * split - 1) // split,
                       jnp.minimum(i, N_EXPERTS * split - 1) % split, 0)
    w_specs = [pl.BlockSpec((1, D_MODEL // split, EXPERT_FF), piece),
               pl.BlockSpec((1, D_MODEL // split, EXPERT_FF), piece),
               pl.BlockSpec((1, EXPERT_FF // split, D_MODEL), piece)]
    return pl.pallas_call(
        _front_kernel,
        grid=(n + 1,),
        in_specs=[
            pl.BlockSpec((tm, D_MODEL), last),
            _const_spec((1, D_MODEL)),
            _const_spec((D_MODEL, in_cols)),
            _const_spec((3, CONV_WIDTH)),
            _const_spec((1, CONV_WIDTH)),
            _const_spec((CONV_WIDTH, D_MODEL)),
            _const_spec((2, D_MODEL)),
            _const_spec(bre.shape),
            _const_spec(bim.shape),
            _const_spec(cw.shape),
            _const_spec(a_re.shape),
            _const_spec(a_im.shape),
            _const_spec((1, SSM_WIDTH)),
        ] + w_specs,
        out_specs=[_row_spec(tm, D_MODEL), _row_spec(tm, D_MODEL),
                   pl.BlockSpec((tm, SSM_WIDTH), lag)] + w_specs,
        out_shape=[
            jax.ShapeDtypeStruct((t + tm, D_MODEL), BF16),
            jax.ShapeDtypeStruct((t + tm, D_MODEL), BF16),
            jax.ShapeDtypeStruct((t, SSM_WIDTH), BF16),
            jax.ShapeDtypeStruct(w_gate.shape, BF16),
            jax.ShapeDtypeStruct(w_up.shape, BF16),
            jax.ShapeDtypeStruct(w_down.shape, BF16),
        ],
        scratch_shapes=[
            pltpu.VMEM((8, CONV_WIDTH), F32),
            pltpu.VMEM((tm, SSM_WIDTH), BF16),
            pltpu.VMEM((tm * SCAN_PITCH, LANES), F32),
            pltpu.VMEM((STATE_ROWS, LANES), F32),
        ],
        compiler_params=_ARB,
        name="mixer_front",
    )(x, norm_g, w_in, conv_w, conv_b, w_conv_out, gate_bias, bre, bim, cw, a_re, a_im, d_skip,
      w_gate, w_up, w_down)


def _s5_scan_tile(u, bre_ref, bim_ref, are_ref, aim_ref, r_ref, state_ref):
    tm = u.shape[0]
    tiles_per_half = MXU_DIM * 4 // LANES

    for k in range(2):
        uk = u[:, k * MXU_DIM:(k + 1) * MXU_DIM]
        re = jnp.dot(uk, bre_ref[k], preferred_element_type=F32)
        im = jnp.dot(uk, bim_ref[k], preferred_element_type=F32)
        for jj in range(tiles_per_half):
            j = k * tiles_per_half + jj
            sl = slice(jj * LANES, (jj + 1) * LANES)
            r_ref[pl.ds(j, tm, stride=SCAN_PITCH), :] = re[:, sl]
            r_ref[pl.ds(HALF_ROWS + j, tm, stride=SCAN_PITCH), :] = im[:, sl]

    a_re = are_ref[...]
    a_im = aim_ref[...]

    s_re = state_ref[0:HALF_ROWS, :]
    s_im = state_ref[HALF_ROWS:, :]
    for t in range(tm):
        base = t * SCAN_PITCH
        b_re = r_ref[pl.ds(base, HALF_ROWS), :]
        b_im = r_ref[pl.ds(base + HALF_ROWS, HALF_ROWS), :]
        s_re, s_im = (a_re * s_re - a_im * s_im + b_re,
                      a_re * s_im + a_im * s_re + b_im)
        r_ref[pl.ds(base, HALF_ROWS), :] = s_re
        r_ref[pl.ds(base + HALF_ROWS, HALF_ROWS), :] = s_im
    state_ref[0:HALF_ROWS, :] = s_re
    state_ref[HALF_ROWS:, :] = s_im


def _s5_readout(u, cw_ref, d_ref, y_ref, r_ref):
    tm = u.shape[0]
    tiles_per_half = MXU_DIM * 4 // LANES
    ys = []
    for k in range(2):
        cols = []
        for half in range(2):
            for jj in range(tiles_per_half):
                j = half * HALF_ROWS + k * tiles_per_half + jj
                cols.append(r_ref[pl.ds(j, tm, stride=SCAN_PITCH), :])
        s = jnp.concatenate(cols, axis=1).astype(BF16)
        ys.append(jnp.dot(s, cw_ref[k], preferred_element_type=F32))
    y = jnp.concatenate(ys, axis=1) + d_ref[...] * u.astype(F32)
    y_ref[...] = jax.nn.gelu(y).astype(BF16)


def _s5_weights(a_re, a_im, log_dt, b_re, b_im, c_re, c_im):
    g, n, h = SSM_GROUPS, SSM_STATE, SSM_GROUP
    gpt = MXU_DIM // h
    lam = lax.complex(a_re.astype(F32), a_im.astype(F32))
    dt = jnp.exp(log_dt.astype(F32))[:, None]
    a_bar = jnp.exp(lam * dt)
    b_bar = ((a_bar - 1.0) / lam)[..., None] * lax.complex(b_re.astype(F32), b_im.astype(F32))
    eye = jnp.eye(gpt, dtype=F32)

    def blk_in(w):
        w = w.reshape(g // gpt, gpt, n, h).transpose(0, 1, 3, 2)
        return jnp.einsum("kghn,gf->kghfn", w, eye).reshape(g // gpt, gpt * h, gpt * n)

    def blk_out(c):
        c = c.astype(F32).reshape(g // gpt, gpt, h, n).transpose(0, 1, 3, 2)
        return jnp.einsum("kgnh,gf->kgnfh", c, eye).reshape(g // gpt, gpt * n, gpt * h)

    bre = blk_in(jnp.real(b_bar)).astype(BF16)
    bim = blk_in(jnp.imag(b_bar)).astype(BF16)
    cw = jnp.concatenate([blk_out(c_re), -blk_out(c_im)], axis=1).astype(BF16)
    tile = (HALF_ROWS, LANES)
    return bre, bim, cw, jnp.real(a_bar).reshape(tile), jnp.imag(a_bar).reshape(tile)


def _back_kernel(yg_ref, ma_ref, gb_ref, x_ref, wa_ref, wb_ref, wo_ref, g2_ref, wr_ref, br_ref,
                 x1_ref, h2p_ref, route_ref, table_ref, counts_ref, h2_ref):
    tm = x_ref.shape[0]
    step = pl.program_id(0)

    @pl.when(step == 0)
    def _():
        counts_ref[...] = jnp.zeros_like(counts_ref)
        h2_ref[...] = jnp.zeros_like(h2_ref)

    h2_prev = h2_ref[...]
    h2_hi = h2_prev.astype(BF16)
    h2_lo = (h2_prev - h2_hi.astype(F32)).astype(BF16)
    both = jnp.dot(h2_hi, wr_ref[...], preferred_element_type=F32)
    logits = (both[:, :LANES]
              + (both[:, LANES:]
                 + jnp.dot(h2_lo, wr_ref[:, :LANES], preferred_element_type=F32))) + br_ref[...]
    lane = lax.broadcasted_iota(jnp.int32, logits.shape, 1)
    lane_f = lane.astype(F32)
    neg = jnp.float32(-jnp.inf)
    big = jnp.float32(LANES)
    is_grp = (lane >= ROUTE_GROUP_LANE) & (lane < ROUTE_GROUP_LANE + N_EXPERT_GROUPS)
    gl = jnp.where(is_grp, logits, neg)
    gmax = jnp.max(gl, axis=1, keepdims=True)
    gidx = jnp.min(jnp.where(gl == gmax, lane_f - ROUTE_GROUP_LANE, big), axis=1, keepdims=True)
    pg_top = 1.0 / jnp.sum(jnp.exp(gl - gmax), axis=1, keepdims=True)
    lane_grp = (lane // EXPERTS_PER_GROUP).astype(F32)
    el = jnp.where((lane < N_EXPERTS) & (lane_grp == gidx), logits, neg)
    l1 = jnp.max(el, axis=1, keepdims=True)
    i1 = jnp.min(jnp.where(el == l1, lane_f, big), axis=1, keepdims=True)
    el2 = jnp.where(lane_f == i1, neg, el)
    l2 = jnp.max(el2, axis=1, keepdims=True)
    i2 = jnp.min(jnp.where(el2 == l2, lane_f, big), axis=1, keepdims=True)
    r = jnp.exp(l2 - l1)
    w1 = pg_top / (1.0 + r)
    w2 = pg_top * r / (1.0 + r)
    oh1 = (lane_f == i1).astype(F32)
    oh2 = (lane_f == i2).astype(F32)
    picked = oh1 + oh2

    yg = yg_ref[...]
    yb = (jnp.dot(yg, wa_ref[...], preferred_element_type=F32)
          * jax.nn.sigmoid(jnp.dot(yg, wb_ref[...], preferred_element_type=F32)))
    mixed = (ma_ref[...].astype(F32) + gb_ref[...].astype(F32) * yb).astype(BF16)
    x1 = x_ref[...] + jnp.dot(mixed, wo_ref[...], preferred_element_type=F32)
    x1_ref[...] = x1
    h2 = _rms(x1, g2_ref[...])
    h2p_ref[...] = _pack_rows(h2)
    h2_ref[...] = h2

    rr = lax.broadcasted_iota(jnp.int32, (tm, tm), 0)
    cc = lax.broadcasted_iota(jnp.int32, (tm, tm), 1)
    before = (cc < rr).astype(BF16)
    prior = jnp.dot(before, picked.astype(BF16), preferred_element_type=F32) + counts_ref[...]
    rank1 = jnp.sum(oh1 * prior, axis=1, keepdims=True)
    rank2 = jnp.sum(oh2 * prior, axis=1, keepdims=True)
    counts_ref[...] += jnp.where(step > 0, jnp.sum(picked, axis=0, keepdims=True), 0.0)

    route = jnp.where(
        lane == 0, i1, jnp.where(lane == 1, i2, jnp.where(lane == 2, w1, jnp.where(
            lane == 3, w2, jnp.where(lane == 4, rank1, jnp.where(lane == 5, rank2, 0.0))))))
    route_ref[...] = route
    table_ref[...] = route.T[0:ROUTE_ROWS, :]


def _mixer_back(yg, ma, gb, x, w_glu_a, w_glu_b, w_o, norm_g, w_route, b_route, tm):
    t = x.shape[0]
    n = t // tm
    cur = lambda i: (jnp.minimum(i, n - 1), 0)
    lag = lambda i: (jnp.maximum(i - 1, 0), 0)
    return pl.pallas_call(
        _back_kernel,
        grid=(n + 1,),
        in_specs=[
            pl.BlockSpec((tm, SSM_WIDTH), cur),
            pl.BlockSpec((tm, D_MODEL), cur),
            pl.BlockSpec((tm, D_MODEL), cur),
            pl.BlockSpec((tm, D_MODEL), cur),
            _const_spec((SSM_WIDTH, D_MODEL)),
            _const_spec((SSM_WIDTH, D_MODEL)),
            _const_spec((D_MODEL, D_MODEL)),
            _const_spec((1, D_MODEL)),
            _const_spec((D_MODEL, 2 * LANES)),
            _const_spec((1, LANES)),
        ],
        out_specs=[pl.BlockSpec((tm, D_MODEL), cur), pl.BlockSpec((tm, PACKED), cur),
                   pl.BlockSpec((tm, LANES), lag),
                   pl.BlockSpec((ROUTE_ROWS, tm), lambda i: (0, jnp.maximum(i - 1, 0))),
                   pl.BlockSpec((1, LANES), lambda i: (0, 0))],
        out_shape=[
            jax.ShapeDtypeStruct((t, D_MODEL), F32),
            jax.ShapeDtypeStruct((t, PACKED), I32),
            jax.ShapeDtypeStruct((t, LANES), F32),
            jax.ShapeDtypeStruct((ROUTE_ROWS, t), F32),
            jax.ShapeDtypeStruct((1, LANES), F32),
        ],
        scratch_shapes=[pltpu.VMEM((tm, D_MODEL), F32)],
        compiler_params=_ARB,
        name="mixer_back",
    )(yg, ma, gb, x, w_glu_a, w_glu_b, w_o, norm_g, w_route, b_route)


_SC_MESH = dict(core_axis_name="c", subcore_axis_name="s")


def _sc_worker():
    return lax.axis_index("s") * SC_CORES + lax.axis_index("c")


def _sc_dispatch(rows, slot0, slot1, nslot):
    t, width = rows.shape
    per_w = t // (SC_CORES * SC_SUBCORES)
    assert per_w % SC_CHUNK == 0

    @functools.partial(
        pl.kernel,
        out_type=jax.ShapeDtypeStruct((nslot, width), rows.dtype),
        mesh=plsc.VectorSubcoreMesh(**_SC_MESH),
        scratch_types=[
            pltpu.VMEM((1, SC_CHUNK), I32),
            pltpu.VMEM((1, SC_CHUNK), I32),
            pltpu.VMEM((SC_CHUNK, width), rows.dtype),
        ],
        name="moe_dispatch",
    )
    def k(rows_hbm, s0_hbm, s1_hbm, out_hbm, i0_v, i1_v, rows_v):
        base = _sc_worker() * per_w

        @pl.loop(0, per_w // SC_CHUNK)
        def _(c):
            off = base + c * SC_CHUNK
            pltpu.sync_copy(s0_hbm.at[:, pl.ds(off, SC_CHUNK)], i0_v)
            pltpu.sync_copy(s1_hbm.at[:, pl.ds(off, SC_CHUNK)], i1_v)
            pltpu.sync_copy(rows_hbm.at[pl.ds(off, SC_CHUNK)], rows_v)
            pltpu.sync_copy(rows_v, out_hbm.at[i0_v.at[0]])
            pltpu.sync_copy(rows_v, out_hbm.at[i1_v.at[0]])

    return k(rows, slot0.reshape(1, t), slot1.reshape(1, t))


def _sc_gather(table, idx):
    n = idx.shape[0]
    width = table.shape[1]
    per_w = n // (SC_CORES * SC_SUBCORES)
    assert per_w % SC_CHUNK == 0

    @functools.partial(
        pl.kernel,
        out_type=jax.ShapeDtypeStruct((n, width), table.dtype),
        mesh=plsc.VectorSubcoreMesh(**_SC_MESH),
        scratch_types=[
            pltpu.VMEM((1, SC_CHUNK), I32),
            pltpu.VMEM((SC_CHUNK, width), table.dtype),
        ],
        name="moe_combine_gather",
    )
    def k(table_hbm, idx_hbm, out_hbm, idx_v, rows_v):
        base = _sc_worker() * per_w

        @pl.loop(0, per_w // SC_CHUNK)
        def _(c):
            off = base + c * SC_CHUNK
            pltpu.sync_copy(idx_hbm.at[:, pl.ds(off, SC_CHUNK)], idx_v)
            pltpu.sync_copy(table_hbm.at[idx_v.at[0]], rows_v)
            pltpu.sync_copy(rows_v, out_hbm.at[pl.ds(off, SC_CHUNK)])

    return k(table, idx.reshape(1, n))


def _moe_kernel(first_ref, last_ref, count_ref, total_ref,
                xs_hbm, wg_ref, wu_ref, wd_ref, out_hbm,
                xbuf0, xbuf1, obuf0, obuf1, xsem, osem):
    e = pl.program_id(0)
    ch = MOE_TILE
    half = ch // 2
    first, last, count, total = first_ref[e], last_ref[e], count_ref[e], total_ref[0]
    xbufs = (xbuf0, xbuf1)
    obufs = (obuf0, obuf1)

    def x_copy(c, s):
        rows = pl.ds(pl.multiple_of(c * ch, ch), ch)
        return pltpu.make_async_copy(xs_hbm.at[rows], xbufs[s], xsem.at[s])

    def o_copy(c, s):
        rows = pl.ds(pl.multiple_of(c * ch, ch), ch)
        return pltpu.make_async_copy(obufs[s], out_hbm.at[rows], osem.at[s])

    @pl.when(e == 0)
    def _():
        x_copy(0, 0).start()

    def experts_rows(s, rows, valid):
        row = lax.broadcasted_iota(jnp.int32, (rows, PACKED), 0)
        words = jnp.where(row < valid, xbufs[s][0:rows, :], 0)
        lo, hi = _unpack_rows(words)
        xs = jnp.concatenate([lo.astype(BF16), hi.astype(BF16)], axis=1)
        gate = jnp.dot(xs, wg_ref[0], preferred_element_type=F32)
        up = jnp.dot(xs, wu_ref[0], preferred_element_type=F32)
        act = (gate * jax.nn.sigmoid(gate) * up).astype(BF16)
        obufs[s][0:rows, :] = _pack_rows(jnp.dot(act, wd_ref[0], preferred_element_type=F32))

    def chunk_in_slot(c, s):
        x_copy(c, s).wait()

        @pl.when(c + 1 < total)
        def _():
            x_copy(c + 1, 1 - s).start()

        @pl.when(c >= 2)
        def _():
            o_copy(c - 2, s).wait()

        valid = count - (c - first) * ch

        @pl.when(valid > half)
        def _():
            experts_rows(s, ch, valid)

        @pl.when(valid <= half)
        def _():
            experts_rows(s, half, valid)
            obufs[s][half:, :] = jnp.zeros((ch - half, PACKED), I32)

        o_copy(c, s).start()

    def chunk(c, carry):
        for s in range(2):
            @pl.when(c % 2 == s)
            def _():
                chunk_in_slot(c, s)
        return carry

    lax.fori_loop(first, last, chunk, 0)

    @pl.when(e == pl.num_programs(0) - 1)
    def _():
        for s in range(2):
            @pl.when(((total - 1) % 2 == s) | (total >= 2))
            def _():
                o_copy(0, s).wait()


def _moe_experts(plan, xs, w_gate, w_up, w_down):
    nslot = xs.shape[0]
    any_spec = pl.BlockSpec(memory_space=pl.ANY)
    grid_spec = pltpu.PrefetchScalarGridSpec(
        num_scalar_prefetch=len(plan),
        grid=(N_EXPERTS,),
        in_specs=[
            any_spec,
            pl.BlockSpec((1, D_MODEL, EXPERT_FF), lambda e, *_: (e, 0, 0)),
            pl.BlockSpec((1, D_MODEL, EXPERT_FF), lambda e, *_: (e, 0, 0)),
            pl.BlockSpec((1, EXPERT_FF, D_MODEL), lambda e, *_: (e, 0, 0)),
        ],
        out_specs=any_spec,
        scratch_shapes=[
            pltpu.VMEM((MOE_TILE, PACKED), I32),
            pltpu.VMEM((MOE_TILE, PACKED), I32),
            pltpu.VMEM((MOE_TILE, PACKED), I32),
            pltpu.VMEM((MOE_TILE, PACKED), I32),
            pltpu.SemaphoreType.DMA((2,)),
            pltpu.SemaphoreType.DMA((2,)),
        ],
    )
    return pl.pallas_call(
        _moe_kernel,
        grid_spec=grid_spec,
        out_shape=jax.ShapeDtypeStruct((nslot, PACKED), I32),
        compiler_params=_ARB,
        name="moe_experts",
    )(*plan, xs, w_gate, w_up, w_down)


def _dispatch_plan(table, counts, tm):
    e_ids = table[0:2].astype(I32)
    ranks = table[4:6].astype(I32)
    counts = counts[0, :N_EXPERTS].astype(I32)
    padded = ((counts + tm - 1) // tm) * tm
    ends = jnp.cumsum(padded)
    starts = ends - padded
    experts = jnp.arange(N_EXPERTS, dtype=I32)
    slot = jnp.sum(jnp.where(e_ids[..., None] == experts, starts, 0), axis=-1) + ranks
    plan = (starts // tm, ends // tm, counts, ends[-1:] // tm)
    return slot, tuple(p.astype(I32) for p in plan)


def _final_kernel(x1_ref, y0_ref, y1_ref, route_ref, g_ref, *rest):
    out_ref = rest[-1]
    route = route_ref[...]
    w1 = route[:, 2:3]
    w2 = route[:, 3:4]
    a_lo, a_hi = _unpack_rows(y0_ref[...])
    b_lo, b_hi = _unpack_rows(y1_ref[...])
    moe = jnp.concatenate([w1 * a_lo + w2 * b_lo, w1 * a_hi + w2 * b_hi], axis=1)
    out_ref[...] = _rms(x1_ref[...] + moe, g_ref[...])


def _final(x1, ycat, route, norm_g, tm, chunk, prev):
    t = x1.shape[0]
    nblk = t // tm // FINAL_CHUNKS
    off = chunk * nblk
    in_specs = [
        pl.BlockSpec((tm, D_MODEL), lambda i: (i + off, 0)),
        pl.BlockSpec((tm, PACKED), lambda i: (i, 0)),
        pl.BlockSpec((tm, PACKED), lambda i: (i + nblk, 0)),
        pl.BlockSpec((tm, LANES), lambda i: (i + off, 0)),
        _const_spec((1, D_MODEL)),
    ]
    args = [x1, ycat, ycat, route, norm_g]
    aliases = {}
    if prev is not None:
        in_specs.append(pl.BlockSpec(memory_space=pl.ANY))
        args.append(prev)
        aliases = {len(args) - 1: 0}
    return pl.pallas_call(
        _final_kernel,
        grid=(nblk,),
        in_specs=in_specs,
        out_specs=pl.BlockSpec((tm, D_MODEL), lambda i: (i + off, 0)),
        out_shape=jax.ShapeDtypeStruct((t, D_MODEL), F32),
        input_output_aliases=aliases,
        compiler_params=_ARB,
        name="final_norm",
    )(*args)


def kernel(x, norm_mix, w_in, conv_w, conv_b, w_conv_out, ssm_a_re, ssm_a_im, ssm_log_dt,
           ssm_b_re, ssm_b_im, ssm_c_re, ssm_c_im, ssm_d, w_glu_a, w_glu_b, gate_bias, w_o,
           norm_ffn, w_route_group, b_route_group, w_route_expert, b_route_expert,
           w_gate, w_up, w_down, norm_final):
    bsz, length, d = x.shape
    assert d == D_MODEL and norm_mix.shape[0] == 1
    t = bsz * length
    tm = TOKEN_TILE
    assert bsz == 1 and t % (tm * FINAL_CHUNKS) == 0
    xt = x.reshape(t, d)
    row = lambda a: a.reshape(1, -1).astype(F32)

    s5_weights = _s5_weights(
        ssm_a_re[0], ssm_a_im[0], ssm_log_dt[0], ssm_b_re[0], ssm_b_im[0],
        ssm_c_re[0], ssm_c_im[0])
    ma, gb, yg, wg_bf, wu_bf, wd_bf = _mixer_front(
        xt, row(norm_mix[0]), w_in[0].astype(BF16), conv_w[0].reshape(3, CONV_WIDTH),
        row(conv_b[0]), w_conv_out[0].astype(BF16), gate_bias[0], s5_weights, row(ssm_d[0]),
        w_gate[0], w_up[0], w_down[0], FRONT_TILE)

    pad = LANES - N_EXPERTS - N_EXPERT_GROUPS
    w_route = jnp.concatenate(
        [w_route_expert[0], w_route_group[0], jnp.zeros((d, pad), F32)], axis=1)
    w_route_hi = w_route.astype(BF16)
    w_route_lo = (w_route - w_route_hi.astype(F32)).astype(BF16)
    b_route = jnp.concatenate(
        [b_route_expert[0], b_route_group[0], jnp.zeros((pad,), F32)]).reshape(1, LANES)
    x1, h2p, route, table, counts = _mixer_back(
        yg, ma, gb, xt, w_glu_a[0].astype(BF16), w_glu_b[0].astype(BF16), w_o[0].astype(BF16),
        row(norm_ffn[0]), jnp.concatenate([w_route_hi, w_route_lo], axis=1), b_route, tm)

    slot, plan = _dispatch_plan(table, counts, MOE_TILE)
    nslot = 2 * t + N_EXPERTS * MOE_TILE
    xs = _sc_dispatch(h2p, slot[0], slot[1], nslot)
    ys = _moe_experts(plan, xs, wg_bf, wu_bf, wd_bf)
    out = None
    tc = t // FINAL_CHUNKS
    for k in range(FINAL_CHUNKS):
        ycat = _sc_gather(ys, slot[:, k * tc:(k + 1) * tc].reshape(-1))
        out = _final(x1, ycat, route, row(norm_final), tm, k, out)
    return out.reshape(bsz, length, d)
```

```python
import functools

import jax
import jax.numpy as jnp
from jax import lax
from jax.experimental import pallas as pl
from jax.experimental.pallas import tpu as pltpu
from jax.experimental.pallas import tpu_sc as plsc

F32 = jnp.float32
BF16 = jnp.bfloat16
I32 = jnp.int32

D_MODEL = 1024
CONV_WIDTH = 1024
SSM_WIDTH = 512
SSM_GROUP = 16
SSM_GROUPS = 32
SSM_STATE = 64
N_EXPERT_GROUPS = 4
EXPERTS_PER_GROUP = 8
N_EXPERTS = 32
EXPERT_FF = 512
EPS = 1e-6

LANES = 128
MXU_DIM = 256
TOKEN_TILE = 512
FRONT_TILE = 256
MOE_TILE = 512
SCAN_PITCH = 36
STATE_ROWS = 2 * SSM_GROUPS * SSM_STATE // LANES
HALF_ROWS = STATE_ROWS // 2
ROUTE_GROUP_LANE = N_EXPERTS
ROUTE_ROWS = 8
FINAL_CHUNKS = 4
PACKED = D_MODEL // 2
HI_MASK = -65536
SC_CORES = 2
SC_SUBCORES = 16
SC_CHUNK = 128
VMEM_LIMIT = 56 * 1024 * 1024

_ARB = pltpu.CompilerParams(dimension_semantics=("arbitrary",), vmem_limit_bytes=VMEM_LIMIT)


def _const_spec(shape):
    nd = len(shape)
    return pl.BlockSpec(shape, lambda i, *_: (0,) * nd, pipeline_mode=pl.Buffered(1))


def _row_spec(tm, width):
    return pl.BlockSpec((tm, width), lambda i, *_: (i, 0))


def _rms(x, g):
    ms = jnp.mean(x * x, axis=-1, keepdims=True)
    return x * lax.rsqrt(ms + EPS) * g


def _pack_rows(v):
    bits = lax.bitcast_convert_type(v.astype(BF16).astype(F32), I32)
    lo = lax.shift_right_logical(bits[:, :PACKED], 16)
    hi = bits[:, PACKED:] & HI_MASK
    return lo | hi


def _unpack_rows(w):
    lo = lax.bitcast_convert_type(lax.shift_left(w, 16), F32)
    hi = lax.bitcast_convert_type(w & HI_MASK, F32)
    return lo, hi


def _front_kernel(x_ref, g_ref, win_ref, cw_ref, cb_ref, wco_ref, gbias_ref,
                  bre_ref, bim_ref, cw2_ref, are_ref, aim_ref, d_ref, wg_ref, wu_ref, wd_ref,
                  ma_ref, gb_ref, y_ref, wgb_ref, wub_ref, wdb_ref,
                  carry_ref, u_ref, r_ref, state_ref):
    tm = x_ref.shape[0]
    wgb_ref[...] = wg_ref[...].astype(BF16)
    wub_ref[...] = wu_ref[...].astype(BF16)
    wdb_ref[...] = wd_ref[...].astype(BF16)
    c0, c1, c2 = CONV_WIDTH, 2 * CONV_WIDTH, 3 * CONV_WIDTH
    c3 = c2 + SSM_WIDTH

    @pl.when(pl.program_id(0) == 0)
    def _():
        carry_ref[...] = jnp.zeros_like(carry_ref)
        u_ref[...] = jnp.zeros_like(u_ref)
        state_ref[...] = jnp.zeros_like(state_ref)

    h = _rms(x_ref[...], g_ref[...]).astype(BF16)
    u_prev = u_ref[...]
    _s5_scan_tile(u_prev, bre_ref, bim_ref, are_ref, aim_ref, r_ref, state_ref)

    def proj(lo, hi):
        return jnp.dot(h, win_ref[:, lo:hi], preferred_element_type=F32)

    v = proj(c1, c2) * proj(0, c0)
    row = lax.broadcasted_iota(jnp.int32, v.shape, 0)
    prev1 = carry_ref[7:8, :]
    prev2 = carry_ref[6:7, :]
    v1 = jnp.where(row == 0, prev1, pltpu.roll(v, 1, 0))
    v2 = jnp.where(row == 0, prev2, jnp.where(row == 1, prev1, pltpu.roll(v, 2, 0)))
    carry_ref[...] = v[tm - 8:, :]
    y = cw_ref[0:1, :] * v2 + cw_ref[1:2, :] * v1 + cw_ref[2:3, :] * v + cb_ref[...]
    z = (proj(c0, c1) * y).astype(BF16)
    ya = jnp.dot(z, wco_ref[...], preferred_element_type=F32)
    ga = jax.nn.sigmoid(proj(c3, c3 + D_MODEL) + gbias_ref[0:1, :])
    ma_ref[...] = (ga * ya).astype(BF16)
    gb = jax.nn.sigmoid(proj(c3 + D_MODEL, c3 + 2 * D_MODEL) + gbias_ref[1:2, :])
    gb_ref[...] = gb.astype(BF16)
    u_new = proj(c2, c3).astype(BF16)
    _s5_readout(u_prev, cw2_ref, d_ref, y_ref, r_ref)
    u_ref[...] = u_new


def _mixer_front(x, norm_g, w_in, conv_w, conv_b, w_conv_out, gate_bias, s5_weights, d_skip,
                 w_gate, w_up, w_down, tm):
    bre, bim, cw, a_re, a_im = s5_weights
    t = x.shape[0]
    in_cols = w_in.shape[1]
    n = t // tm
    last = lambda i: (jnp.minimum(i, n - 1), 0)
    lag = lambda i: (jnp.maximum(i - 1, 0), 0)
    split = max(k for k in (1, 2, 4, 8) if N_EXPERTS * k <= n + 1)
    piece = lambda i: (jnp.minimum(i, N_EXPERTS * split - 1) // split,
                       jnp.minimum(i, N_EXPERTS * split - 1) % split, 0)
    w_specs = [pl.BlockSpec((1, D_MODEL // split, EXPERT_FF), piece),
               pl.BlockSpec((1, D_MODEL // split, EXPERT_FF), piece),
               pl.BlockSpec((1, EXPERT_FF // split, D_MODEL), piece)]
    return pl.pallas_call(
        _front_kernel,
        grid=(n + 1,),
        in_specs=[
            pl.BlockSpec((tm, D_MODEL), last),
            _const_spec((1, D_MODEL)),
            _const_spec((D_MODEL, in_cols)),
            _const_spec((3, CONV_WIDTH)),
            _const_spec((1, CONV_WIDTH)),
            _const_spec((CONV_WIDTH, D_MODEL)),
            _const_spec((2, D_MODEL)),
            _const_spec(bre.shape),
            _const_spec(bim.shape),
            _const_spec(cw.shape),
            _const_spec(a_re.shape),
            _const_spec(a_im.shape),
            _const_spec((1, SSM_WIDTH)),
        ] + w_specs,
        out_specs=[_row_spec(tm, D_MODEL), _row_spec(tm, D_MODEL),
                   pl.BlockSpec((tm, SSM_WIDTH), lag)] + w_specs,
        out_shape=[
            jax.ShapeDtypeStruct((t + tm, D_MODEL), BF16),
            jax.ShapeDtypeStruct((t + tm, D_MODEL), BF16),
            jax.ShapeDtypeStruct((t, SSM_WIDTH), BF16),
            jax.ShapeDtypeStruct(w_gate.shape, BF16),
            jax.ShapeDtypeStruct(w_up.shape, BF16),
            jax.ShapeDtypeStruct(w_down.shape, BF16),
        ],
        scratch_shapes=[
            pltpu.VMEM((8, CONV_WIDTH), F32),
            pltpu.VMEM((tm, SSM_WIDTH), BF16),
            pltpu.VMEM((tm * SCAN_PITCH, LANES), F32),
            pltpu.VMEM((STATE_ROWS, LANES), F32),
        ],
        compiler_params=_ARB,
        name="mixer_front",
    )(x, norm_g, w_in, conv_w, conv_b, w_conv_out, gate_bias, bre, bim, cw, a_re, a_im, d_skip,
      w_gate, w_up, w_down)


def _s5_scan_tile(u, bre_ref, bim_ref, are_ref, aim_ref, r_ref, state_ref):
    tm = u.shape[0]
    tiles_per_half = MXU_DIM * 4 // LANES

    for k in range(2):
        uk = u[:, k * MXU_DIM:(k + 1) * MXU_DIM]
        re = jnp.dot(uk, bre_ref[k], preferred_element_type=F32)
        im = jnp.dot(uk, bim_ref[k], preferred_element_type=F32)
        for jj in range(tiles_per_half):
            j = k * tiles_per_half + jj
            sl = slice(jj * LANES, (jj + 1) * LANES)
            r_ref[pl.ds(j, tm, stride=SCAN_PITCH), :] = re[:, sl]
            r_ref[pl.ds(HALF_ROWS + j, tm, stride=SCAN_PITCH), :] = im[:, sl]

    a_re = are_ref[...]
    a_im = aim_ref[...]

    s_re = state_ref[0:HALF_ROWS, :]
    s_im = state_ref[HALF_ROWS:, :]
    for t in range(tm):
        base = t * SCAN_PITCH
        b_re = r_ref[pl.ds(base, HALF_ROWS), :]
        b_im = r_ref[pl.ds(base + HALF_ROWS, HALF_ROWS), :]
        s_re, s_im = (a_re * s_re - a_im * s_im + b_re,
                      a_re * s_im + a_im * s_re + b_im)
        r_ref[pl.ds(base, HALF_ROWS), :] = s_re
        r_ref[pl.ds(base + HALF_ROWS, HALF_ROWS), :] = s_im
    state_ref[0:HALF_ROWS, :] = s_re
    state_ref[HALF_ROWS:, :] = s_im


def _s5_readout(u, cw_ref, d_ref, y_ref, r_ref):
    tm = u.shape[0]
    tiles_per_half = MXU_DIM * 4 // LANES
    ys = []
    for k in range(2):
        cols = []
        for half in range(2):
            for jj in range(tiles_per_half):
                j = half * HALF_ROWS + k * tiles_per_half + jj
                cols.append(r_ref[pl.ds(j, tm, stride=SCAN_PITCH), :])
        s = jnp.concatenate(cols, axis=1).astype(BF16)
        ys.append(jnp.dot(s, cw_ref[k], preferred_element_type=F32))
    y = jnp.concatenate(ys, axis=1) + d_ref[...] * u.astype(F32)
    y_ref[...] = jax.nn.gelu(y).astype(BF16)


def _s5_weights(a_re, a_im, log_dt, b_re, b_im, c_re, c_im):
    g, n, h = SSM_GROUPS, SSM_STATE, SSM_GROUP
    gpt = MXU_DIM // h
    lam = lax.complex(a_re.astype(F32), a_im.astype(F32))
    dt = jnp.exp(log_dt.astype(F32))[:, None]
    a_bar = jnp.exp(lam * dt)
    b_bar = ((a_bar - 1.0) / lam)[..., None] * lax.complex(b_re.astype(F32), b_im.astype(F32))
    eye = jnp.eye(gpt, dtype=F32)

    def blk_in(w):
        w = w.reshape(g // gpt, gpt, n, h).transpose(0, 1, 3, 2)
        return jnp.einsum("kghn,gf->kghfn", w, eye).reshape(g // gpt, gpt * h, gpt * n)

    def blk_out(c):
        c = c.astype(F32).reshape(g // gpt, gpt, h, n).transpose(0, 1, 3, 2)
        return jnp.einsum("kgnh,gf->kgnfh", c, eye).reshape(g // gpt, gpt * n, gpt * h)

    bre = blk_in(jnp.real(b_bar)).astype(BF16)
    bim = blk_in(jnp.imag(b_bar)).astype(BF16)
    cw = jnp.concatenate([blk_out(c_re), -blk_out(c_im)], axis=1).astype(BF16)
    tile = (HALF_ROWS, LANES)
    return bre, bim, cw, jnp.real(a_bar).reshape(tile), jnp.imag(a_bar).reshape(tile)


def _back_kernel(yg_ref, ma_ref, gb_ref, x_ref, wa_ref, wb_ref, wo_ref, g2_ref, wr_ref, br_ref,
                 x1_ref, h2p_ref, route_ref, table_ref, counts_ref, h2_ref):
    tm = x_ref.shape[0]
    step = pl.program_id(0)

    @pl.when(step == 0)
    def _():
        counts_ref[...] = jnp.zeros_like(counts_ref)
        h2_ref[...] = jnp.zeros_like(h2_ref)

    h2_prev = h2_ref[...]
    h2_hi = h2_prev.astype(BF16)
    h2_lo = (h2_prev - h2_hi.astype(F32)).astype(BF16)
    both = jnp.dot(h2_hi, wr_ref[...], preferred_element_type=F32)
    logits = (both[:, :LANES]
              + (both[:, LANES:]
                 + jnp.dot(h2_lo, wr_ref[:, :LANES], preferred_element_type=F32))) + br_ref[...]
    lane = lax.broadcasted_iota(jnp.int32, logits.shape, 1)
    lane_f = lane.astype(F32)
    neg = jnp.float32(-jnp.inf)
    big = jnp.float32(LANES)
    is_grp = (lane >= ROUTE_GROUP_LANE) & (lane < ROUTE_GROUP_LANE + N_EXPERT_GROUPS)
    gl = jnp.where(is_grp, logits, neg)
    gmax = jnp.max(gl, axis=1, keepdims=True)
    gidx = jnp.min(jnp.where(gl == gmax, lane_f - ROUTE_GROUP_LANE, big), axis=1, keepdims=True)
    pg_top = 1.0 / jnp.sum(jnp.exp(gl - gmax), axis=1, keepdims=True)
    lane_grp = (lane // EXPERTS_PER_GROUP).astype(F32)
    el = jnp.where((lane < N_EXPERTS) & (lane_grp == gidx), logits, neg)
    l1 = jnp.max(el, axis=1, keepdims=True)
    i1 = jnp.min(jnp.where(el == l1, lane_f, big), axis=1, keepdims=True)
    el2 = jnp.where(lane_f == i1, neg, el)
    l2 = jnp.max(el2, axis=1, keepdims=True)
    i2 = jnp.min(jnp.where(el2 == l2, lane_f, big), axis=1, keepdims=True)
    r = jnp.exp(l2 - l1)
    w1 = pg_top / (1.0 + r)
    w2 = pg_top * r / (1.0 + r)
    oh1 = (lane_f == i1).astype(F32)
    oh2 = (lane_f == i2).astype(F32)
    picked = oh1 + oh2

    yg = yg_ref[...]
    yb = (jnp.dot(yg, wa_ref[...], preferred_element_type=F32)
          * jax.nn.sigmoid(jnp.dot(yg, wb_ref[...], preferred_element_type=F32)))
    mixed = (ma_ref[...].astype(F32) + gb_ref[...].astype(F32) * yb).astype(BF16)
    x1 = x_ref[...] + jnp.dot(mixed, wo_ref[...], preferred_element_type=F32)
    x1_ref[...] = _pack_rows(x1)
    h2 = _rms(x1, g2_ref[...])
    h2p_ref[...] = _pack_rows(h2)
    h2_ref[...] = h2

    rr = lax.broadcasted_iota(jnp.int32, (tm, tm), 0)
    cc = lax.broadcasted_iota(jnp.int32, (tm, tm), 1)
    before = (cc < rr).astype(BF16)
    prior = jnp.dot(before, picked.astype(BF16), preferred_element_type=F32) + counts_ref[...]
    rank1 = jnp.sum(oh1 * prior, axis=1, keepdims=True)
    rank2 = jnp.sum(oh2 * prior, axis=1, keepdims=True)
    counts_ref[...] += jnp.where(step > 0, jnp.sum(picked, axis=0, keepdims=True), 0.0)

    route = jnp.where(
        lane == 0, i1, jnp.where(lane == 1, i2, jnp.where(lane == 2, w1, jnp.where(
            lane == 3, w2, jnp.where(lane == 4, rank1, jnp.where(lane == 5, rank2, 0.0))))))
    route_ref[...] = route
    table_ref[...] = route.T[0:ROUTE_ROWS, :]


def _mixer_back(yg, ma, gb, x, w_glu_a, w_glu_b, w_o, norm_g, w_route, b_route, tm):
    t = x.shape[0]
    n = t // tm
    cur = lambda i: (jnp.minimum(i, n - 1), 0)
    lag = lambda i: (jnp.maximum(i - 1, 0), 0)
    return pl.pallas_call(
        _back_kernel,
        grid=(n + 1,),
        in_specs=[
            pl.BlockSpec((tm, SSM_WIDTH), cur),
            pl.BlockSpec((tm, D_MODEL), cur),
            pl.BlockSpec((tm, D_MODEL), cur),
            pl.BlockSpec((tm, D_MODEL), cur),
            _const_spec((SSM_WIDTH, D_MODEL)),
            _const_spec((SSM_WIDTH, D_MODEL)),
            _const_spec((D_MODEL, D_MODEL)),
            _const_spec((1, D_MODEL)),
            _const_spec((D_MODEL, 2 * LANES)),
            _const_spec((1, LANES)),
        ],
        out_specs=[pl.BlockSpec((tm, PACKED), cur), pl.BlockSpec((tm, PACKED), cur),
                   pl.BlockSpec((tm, LANES), lag),
                   pl.BlockSpec((ROUTE_ROWS, tm), lambda i: (0, jnp.maximum(i - 1, 0))),
                   pl.BlockSpec((1, LANES), lambda i: (0, 0))],
        out_shape=[
            jax.ShapeDtypeStruct((t, PACKED), I32),
            jax.ShapeDtypeStruct((t, PACKED), I32),
            jax.ShapeDtypeStruct((t, LANES), F32),
            jax.ShapeDtypeStruct((ROUTE_ROWS, t), F32),
            jax.ShapeDtypeStruct((1, LANES), F32),
        ],
        scratch_shapes=[pltpu.VMEM((tm, D_MODEL), F32)],
        compiler_params=_ARB,
        name="mixer_back",
    )(yg, ma, gb, x, w_glu_a, w_glu_b, w_o, norm_g, w_route, b_route)


_SC_MESH = dict(core_axis_name="c", subcore_axis_name="s")


def _sc_worker():
    return lax.axis_index("s") * SC_CORES + lax.axis_index("c")


def _sc_dispatch(rows, slot0, slot1, nslot):
    t, width = rows.shape
    workers = SC_CORES * SC_SUBCORES
    per_w = t // workers
    assert per_w % SC_CHUNK == 0
    nchunk = per_w // SC_CHUNK

    @functools.partial(
        pl.kernel,
        out_type=jax.ShapeDtypeStruct((nslot, width), rows.dtype),
        mesh=plsc.VectorSubcoreMesh(**_SC_MESH),
        scratch_types=[
            pltpu.VMEM((nchunk, SC_CHUNK), I32),
            pltpu.VMEM((nchunk, SC_CHUNK), I32),
            pltpu.VMEM((SC_CHUNK, width), rows.dtype),
            pltpu.SemaphoreType.DMA,
            pltpu.SemaphoreType.DMA,
        ],
        name="moe_dispatch",
    )
    def k(rows_hbm, s0_hbm, s1_hbm, out_hbm, i0_v, i1_v, rows_v, sem0, sem1):
        w = _sc_worker()
        pltpu.sync_copy(s0_hbm.at[w], i0_v)
        pltpu.sync_copy(s1_hbm.at[w], i1_v)

        @pl.loop(0, nchunk)
        def _(c):
            pltpu.sync_copy(rows_hbm.at[pl.ds(w * per_w + c * SC_CHUNK, SC_CHUNK)], rows_v)
            first = pltpu.async_copy(rows_v, out_hbm.at[i0_v.at[c]], sem0)
            second = pltpu.async_copy(rows_v, out_hbm.at[i1_v.at[c]], sem1)
            first.wait()
            second.wait()

    shape = (workers, nchunk, SC_CHUNK)
    return k(rows, slot0.reshape(shape), slot1.reshape(shape))


def _sc_gather(table, idx):
    n = idx.shape[0]
    width = table.shape[1]
    per_w = n // (SC_CORES * SC_SUBCORES)
    assert per_w % SC_CHUNK == 0

    @functools.partial(
        pl.kernel,
        out_type=jax.ShapeDtypeStruct((n, width), table.dtype),
        mesh=plsc.VectorSubcoreMesh(**_SC_MESH),
        scratch_types=[
            pltpu.VMEM((1, SC_CHUNK), I32),
            pltpu.VMEM((SC_CHUNK, width), table.dtype),
        ],
        name="moe_combine_gather",
    )
    def k(table_hbm, idx_hbm, out_hbm, idx_v, rows_v):
        base = _sc_worker() * per_w

        @pl.loop(0, per_w // SC_CHUNK)
        def _(c):
            off = base + c * SC_CHUNK
            pltpu.sync_copy(idx_hbm.at[:, pl.ds(off, SC_CHUNK)], idx_v)
            pltpu.sync_copy(table_hbm.at[idx_v.at[0]], rows_v)
            pltpu.sync_copy(rows_v, out_hbm.at[pl.ds(off, SC_CHUNK)])

    return k(table, idx.reshape(1, n))


def _moe_kernel(first_ref, last_ref, count_ref, total_ref,
                xs_hbm, wg_ref, wu_ref, wd_ref, out_hbm,
                xbuf0, xbuf1, obuf0, obuf1, xsem, osem):
    e = pl.program_id(0)
    ch = MOE_TILE
    half = ch // 2
    first, last, count, total = first_ref[e], last_ref[e], count_ref[e], total_ref[0]
    xbufs = (xbuf0, xbuf1)
    obufs = (obuf0, obuf1)

    def x_copy(c, s):
        rows = pl.ds(pl.multiple_of(c * ch, ch), ch)
        return pltpu.make_async_copy(xs_hbm.at[rows], xbufs[s], xsem.at[s])

    def o_copy(c, s):
        rows = pl.ds(pl.multiple_of(c * ch, ch), ch)
        return pltpu.make_async_copy(obufs[s], out_hbm.at[rows], osem.at[s])

    @pl.when(e == 0)
    def _():
        x_copy(0, 0).start()

    def experts_rows(s, rows, valid):
        row = lax.broadcasted_iota(jnp.int32, (rows, PACKED), 0)
        words = jnp.where(row < valid, xbufs[s][0:rows, :], 0)
        lo, hi = _unpack_rows(words)
        xs = jnp.concatenate([lo.astype(BF16), hi.astype(BF16)], axis=1)
        gate = jnp.dot(xs, wg_ref[0], preferred_element_type=F32)
        up = jnp.dot(xs, wu_ref[0], preferred_element_type=F32)
        act = (gate * jax.nn.sigmoid(gate) * up).astype(BF16)
        obufs[s][0:rows, :] = _pack_rows(jnp.dot(act, wd_ref[0], preferred_element_type=F32))

    def chunk_in_slot(c, s):
        x_copy(c, s).wait()

        @pl.when(c + 1 < total)
        def _():
            x_copy(c + 1, 1 - s).start()

        @pl.when(c >= 2)
        def _():
            o_copy(c - 2, s).wait()

        valid = count - (c - first) * ch

        @pl.when(valid > half)
        def _():
            experts_rows(s, ch, valid)

        @pl.when(valid <= half)
        def _():
            experts_rows(s, half, valid)
            obufs[s][half:, :] = jnp.zeros((ch - half, PACKED), I32)

        o_copy(c, s).start()

    def chunk(c, carry):
        for s in range(2):
            @pl.when(c % 2 == s)
            def _():
                chunk_in_slot(c, s)
        return carry

    lax.fori_loop(first, last, chunk, 0)

    @pl.when(e == pl.num_programs(0) - 1)
    def _():
        for s in range(2):
            @pl.when(((total - 1) % 2 == s) | (total >= 2))
            def _():
                o_copy(0, s).wait()


def _moe_experts(plan, xs, w_gate, w_up, w_down):
    nslot = xs.shape[0]
    any_spec = pl.BlockSpec(memory_space=pl.ANY)
    grid_spec = pltpu.PrefetchScalarGridSpec(
        num_scalar_prefetch=len(plan),
        grid=(N_EXPERTS,),
        in_specs=[
            any_spec,
            pl.BlockSpec((1, D_MODEL, EXPERT_FF), lambda e, *_: (e, 0, 0)),
            pl.BlockSpec((1, D_MODEL, EXPERT_FF), lambda e, *_: (e, 0, 0)),
            pl.BlockSpec((1, EXPERT_FF, D_MODEL), lambda e, *_: (e, 0, 0)),
        ],
        out_specs=any_spec,
        scratch_shapes=[
            pltpu.VMEM((MOE_TILE, PACKED), I32),
            pltpu.VMEM((MOE_TILE, PACKED), I32),
            pltpu.VMEM((MOE_TILE, PACKED), I32),
            pltpu.VMEM((MOE_TILE, PACKED), I32),
            pltpu.SemaphoreType.DMA((2,)),
            pltpu.SemaphoreType.DMA((2,)),
        ],
    )
    return pl.pallas_call(
        _moe_kernel,
        grid_spec=grid_spec,
        out_shape=jax.ShapeDtypeStruct((nslot, PACKED), I32),
        compiler_params=_ARB,
        name="moe_experts",
    )(*plan, xs, w_gate, w_up, w_down)


def _dispatch_plan(table, counts, tm):
    e_ids = table[0:2].astype(I32)
    ranks = table[4:6].astype(I32)
    counts = counts[0, :N_EXPERTS].astype(I32)
    padded = ((counts + tm - 1) // tm) * tm
    ends = jnp.cumsum(padded)
    starts = ends - padded
    experts = jnp.arange(N_EXPERTS, dtype=I32)
    slot = jnp.sum(jnp.where(e_ids[..., None] == experts, starts, 0), axis=-1) + ranks
    plan = (starts // tm, ends // tm, counts, ends[-1:] // tm)
    return slot, tuple(p.astype(I32) for p in plan)


def _final_kernel(x1_ref, y0_ref, y1_ref, route_ref, g_ref, *rest):
    out_ref = rest[-1]
    route = route_ref[...]
    w1 = route[:, 2:3]
    w2 = route[:, 3:4]
    x_lo, x_hi = _unpack_rows(x1_ref[...])
    a_lo, a_hi = _unpack_rows(y0_ref[...])
    b_lo, b_hi = _unpack_rows(y1_ref[...])
    x2 = jnp.concatenate([x_lo + (w1 * a_lo + w2 * b_lo), x_hi + (w1 * a_hi + w2 * b_hi)], axis=1)
    out_ref[...] = _rms(x2, g_ref[...])


def _final(x1, ycat, route, norm_g, tm, chunk, prev):
    t = x1.shape[0]
    nblk = t // tm // FINAL_CHUNKS
    off = chunk * nblk
    in_specs = [
        pl.BlockSpec((tm, PACKED), lambda i: (i + off, 0)),
        pl.BlockSpec((tm, PACKED), lambda i: (i, 0)),
        pl.BlockSpec((tm, PACKED), lambda i: (i + nblk, 0)),
        pl.BlockSpec((tm, LANES), lambda i: (i + off, 0)),
        _const_spec((1, D_MODEL)),
    ]
    args = [x1, ycat, ycat, route, norm_g]
    aliases = {}
    if prev is not None:
        in_specs.append(pl.BlockSpec(memory_space=pl.ANY))
        args.append(prev)
        aliases = {len(args) - 1: 0}
    return pl.pallas_call(
        _final_kernel,
        grid=(nblk,),
        in_specs=in_specs,
        out_specs=pl.BlockSpec((tm, D_MODEL), lambda i: (i + off, 0)),
        out_shape=jax.ShapeDtypeStruct((t, D_MODEL), F32),
        input_output_aliases=aliases,
        compiler_params=_ARB,
        name="final_norm",
    )(*args)


def kernel(x, norm_mix, w_in, conv_w, conv_b, w_conv_out, ssm_a_re, ssm_a_im, ssm_log_dt,
           ssm_b_re, ssm_b_im, ssm_c_re, ssm_c_im, ssm_d, w_glu_a, w_glu_b, gate_bias, w_o,
           norm_ffn, w_route_group, b_route_group, w_route_expert, b_route_expert,
           w_gate, w_up, w_down, norm_final):
    bsz, length, d = x.shape
    assert d == D_MODEL and norm_mix.shape[0] == 1
    t = bsz * length
    tm = TOKEN_TILE
    assert bsz == 1 and t % (tm * FINAL_CHUNKS) == 0
    xt = x.reshape(t, d)
    row = lambda a: a.reshape(1, -1).astype(F32)

    s5_weights = _s5_weights(
        ssm_a_re[0], ssm_a_im[0], ssm_log_dt[0], ssm_b_re[0], ssm_b_im[0],
        ssm_c_re[0], ssm_c_im[0])
    ma, gb, yg, wg_bf, wu_bf, wd_bf = _mixer_front(
        xt, row(norm_mix[0]), w_in[0].astype(BF16), conv_w[0].reshape(3, CONV_WIDTH),
        row(conv_b[0]), w_conv_out[0].astype(BF16), gate_bias[0], s5_weights, row(ssm_d[0]),
        w_gate[0], w_up[0], w_down[0], FRONT_TILE)

    pad = LANES - N_EXPERTS - N_EXPERT_GROUPS
    w_route = jnp.concatenate(
        [w_route_expert[0], w_route_group[0], jnp.zeros((d, pad), F32)], axis=1)
    w_route_hi = w_route.astype(BF16)
    w_route_lo = (w_route - w_route_hi.astype(F32)).astype(BF16)
    b_route = jnp.concatenate(
        [b_route_expert[0], b_route_group[0], jnp.zeros((pad,), F32)]).reshape(1, LANES)
    x1, h2p, route, table, counts = _mixer_back(
        yg, ma, gb, xt, w_glu_a[0].astype(BF16), w_glu_b[0].astype(BF16), w_o[0].astype(BF16),
        row(norm_ffn[0]), jnp.concatenate([w_route_hi, w_route_lo], axis=1), b_route, tm)

    slot, plan = _dispatch_plan(table, counts, MOE_TILE)
    nslot = 2 * t + N_EXPERTS * MOE_TILE
    xs = _sc_dispatch(h2p, slot[0], slot[1], nslot)
    ys = _moe_experts(plan, xs, wg_bf, wu_bf, wd_bf)
    out = None
    tc = t // FINAL_CHUNKS
    for k in range(FINAL_CHUNKS):
        ycat = _sc_gather(ys, slot[:, k * tc:(k + 1) * tc].reshape(-1))
        out = _final(x1, ycat, route, row(norm_final), tm, k, out)
    return out.reshape(bsz, length, d)
```

```python
import functools

import jax
import jax.numpy as jnp
from jax import lax
from jax.experimental import pallas as pl
from jax.experimental.pallas import tpu as pltpu
from jax.experimental.pallas import tpu_sc as plsc

F32 = jnp.float32
BF16 = jnp.bfloat16
I32 = jnp.int32

D_MODEL = 1024
CONV_WIDTH = 1024
SSM_WIDTH = 512
SSM_GROUP = 16
SSM_GROUPS = 32
SSM_STATE = 64
N_EXPERT_GROUPS = 4
EXPERTS_PER_GROUP = 8
N_EXPERTS = 32
EXPERT_FF = 512
EPS = 1e-6

LANES = 128
MXU_DIM = 256
TOKEN_TILE = 512
FINAL_TILE = 1024
FRONT_TILE = 256
MOE_TILE = 512
SCAN_PITCH = 36
STATE_ROWS = 2 * SSM_GROUPS * SSM_STATE // LANES
HALF_ROWS = STATE_ROWS // 2
ROUTE_GROUP_LANE = N_EXPERTS
ROUTE_ROWS = 8
FINAL_CHUNKS = 4
PACKED = D_MODEL // 2
HI_MASK = -65536
SC_CORES = 2
SC_SUBCORES = 16
SC_CHUNK = 128
VMEM_LIMIT = 56 * 1024 * 1024

_ARB = pltpu.CompilerParams(dimension_semantics=("arbitrary",), vmem_limit_bytes=VMEM_LIMIT)


def _const_spec(shape):
    nd = len(shape)
    return pl.BlockSpec(shape, lambda i, *_: (0,) * nd, pipeline_mode=pl.Buffered(1))


def _row_spec(tm, width):
    return pl.BlockSpec((tm, width), lambda i, *_: (i, 0))


def _rms(x, g):
    ms = jnp.mean(x * x, axis=-1, keepdims=True)
    return x * lax.rsqrt(ms + EPS) * g


def _pack_rows(v):
    bits = lax.bitcast_convert_type(v.astype(BF16).astype(F32), I32)
    lo = lax.shift_right_logical(bits[:, :PACKED], 16)
    hi = bits[:, PACKED:] & HI_MASK
    return lo | hi


def _unpack_rows(w):
    lo = lax.bitcast_convert_type(lax.shift_left(w, 16), F32)
    hi = lax.bitcast_convert_type(w & HI_MASK, F32)
    return lo, hi


def _front_kernel(x_ref, g_ref, win_ref, cw_ref, cb_ref, wco_ref, gbias_ref,
                  bre_ref, bim_ref, cw2_ref, are_ref, aim_ref, d_ref, wg_ref, wu_ref, wd_ref,
                  ma_ref, gb_ref, y_ref, wgb_ref, wub_ref, wdb_ref,
                  carry_ref, u_ref, r_ref, state_ref):
    tm = x_ref.shape[0]
    wgb_ref[...] = wg_ref[...].astype(BF16)
    wub_ref[...] = wu_ref[...].astype(BF16)
    wdb_ref[...] = wd_ref[...].astype(BF16)
    c0, c1, c2 = CONV_WIDTH, 2 * CONV_WIDTH, 3 * CONV_WIDTH
    c3 = c2 + SSM_WIDTH

    @pl.when(pl.program_id(0) == 0)
    def _():
        carry_ref[...] = jnp.zeros_like(carry_ref)
        u_ref[...] = jnp.zeros_like(u_ref)
        state_ref[...] = jnp.zeros_like(state_ref)

    h = _rms(x_ref[...], g_ref[...]).astype(BF16)
    u_prev = u_ref[...]
    _s5_scan_tile(u_prev, bre_ref, bim_ref, are_ref, aim_ref, r_ref, state_ref)

    def proj(lo, hi):
        return jnp.dot(h, win_ref[:, lo:hi], preferred_element_type=F32)

    v = proj(c1, c2) * proj(0, c0)
    row = lax.broadcasted_iota(jnp.int32, v.shape, 0)
    prev1 = carry_ref[7:8, :]
    prev2 = carry_ref[6:7, :]
    v1 = jnp.where(row == 0, prev1, pltpu.roll(v, 1, 0))
    v2 = jnp.where(row == 0, prev2, jnp.where(row == 1, prev1, pltpu.roll(v, 2, 0)))
    carry_ref[...] = v[tm - 8:, :]
    y = cw_ref[0:1, :] * v2 + cw_ref[1:2, :] * v1 + cw_ref[2:3, :] * v + cb_ref[...]
    z = (proj(c0, c1) * y).astype(BF16)
    ya = jnp.dot(z, wco_ref[...], preferred_element_type=F32)
    ga = jax.nn.sigmoid(proj(c3, c3 + D_MODEL) + gbias_ref[0:1, :])
    ma_ref[...] = (ga * ya).astype(BF16)
    gb = jax.nn.sigmoid(proj(c3 + D_MODEL, c3 + 2 * D_MODEL) + gbias_ref[1:2, :])
    gb_ref[...] = gb.astype(BF16)
    u_new = proj(c2, c3).astype(BF16)
    _s5_readout(u_prev, cw2_ref, d_ref, y_ref, r_ref)
    u_ref[...] = u_new


def _mixer_front(x, norm_g, w_in, conv_w, conv_b, w_conv_out, gate_bias, s5_weights, d_skip,
                 w_gate, w_up, w_down, tm):
    bre, bim, cw, a_re, a_im = s5_weights
    t = x.shape[0]
    in_cols = w_in.shape[1]
    n = t // tm
    last = lambda i: (jnp.minimum(i, n - 1), 0)
    lag = lambda i: (jnp.maximum(i - 1, 0), 0)
    split = max(k for k in (1, 2, 4, 8) if N_EXPERTS * k <= n + 1)
    piece = lambda i: (jnp.minimum(i, N_EXPERTS * split - 1) // split,
                       jnp.minimum(i, N_EXPERTS * split - 1) % split, 0)
    w_specs = [pl.BlockSpec((1, D_MODEL // split, EXPERT_FF), piece),
               pl.BlockSpec((1, D_MODEL // split, EXPERT_FF), piece),
               pl.BlockSpec((1, EXPERT_FF // split, D_MODEL), piece)]
    return pl.pallas_call(
        _front_kernel,
        grid=(n + 1,),
        in_specs=[
            pl.BlockSpec((tm, D_MODEL), last),
            _const_spec((1, D_MODEL)),
            _const_spec((D_MODEL, in_cols)),
            _const_spec((3, CONV_WIDTH)),
            _const_spec((1, CONV_WIDTH)),
            _const_spec((CONV_WIDTH, D_MODEL)),
            _const_spec((2, D_MODEL)),
            _const_spec(bre.shape),
            _const_spec(bim.shape),
            _const_spec(cw.shape),
            _const_spec(a_re.shape),
            _const_spec(a_im.shape),
            _const_spec((1, SSM_WIDTH)),
        ] + w_specs,
        out_specs=[_row_spec(tm, D_MODEL), _row_spec(tm, D_MODEL),
                   pl.BlockSpec((tm, SSM_WIDTH), lag)] + w_specs,
        out_shape=[
            jax.ShapeDtypeStruct((t + tm, D_MODEL), BF16),
            jax.ShapeDtypeStruct((t + tm, D_MODEL), BF16),
            jax.ShapeDtypeStruct((t, SSM_WIDTH), BF16),
            jax.ShapeDtypeStruct(w_gate.shape, BF16),
            jax.ShapeDtypeStruct(w_up.shape, BF16),
            jax.ShapeDtypeStruct(w_down.shape, BF16),
        ],
        scratch_shapes=[
            pltpu.VMEM((8, CONV_WIDTH), F32),
            pltpu.VMEM((tm, SSM_WIDTH), BF16),
            pltpu.VMEM((tm * SCAN_PITCH, LANES), F32),
            pltpu.VMEM((STATE_ROWS, LANES), F32),
        ],
        compiler_params=_ARB,
        name="mixer_front",
    )(x, norm_g, w_in, conv_w, conv_b, w_conv_out, gate_bias, bre, bim, cw, a_re, a_im, d_skip,
      w_gate, w_up, w_down)


def _s5_scan_tile(u, bre_ref, bim_ref, are_ref, aim_ref, r_ref, state_ref):
    tm = u.shape[0]
    tiles_per_half = MXU_DIM * 4 // LANES

    for k in range(2):
        uk = u[:, k * MXU_DIM:(k + 1) * MXU_DIM]
        re = jnp.dot(uk, bre_ref[k], preferred_element_type=F32)
        im = jnp.dot(uk, bim_ref[k], preferred_element_type=F32)
        for jj in range(tiles_per_half):
            j = k * tiles_per_half + jj
            sl = slice(jj * LANES, (jj + 1) * LANES)
            r_ref[pl.ds(j, tm, stride=SCAN_PITCH), :] = re[:, sl]
            r_ref[pl.ds(HALF_ROWS + j, tm, stride=SCAN_PITCH), :] = im[:, sl]

    a_re = are_ref[...]
    a_im = aim_ref[...]

    s_re = state_ref[0:HALF_ROWS, :]
    s_im = state_ref[HALF_ROWS:, :]
    for t in range(tm):
        base = t * SCAN_PITCH
        b_re = r_ref[pl.ds(base, HALF_ROWS), :]
        b_im = r_ref[pl.ds(base + HALF_ROWS, HALF_ROWS), :]
        s_re, s_im = (a_re * s_re - a_im * s_im + b_re,
                      a_re * s_im + a_im * s_re + b_im)
        r_ref[pl.ds(base, HALF_ROWS), :] = s_re
        r_ref[pl.ds(base + HALF_ROWS, HALF_ROWS), :] = s_im
    state_ref[0:HALF_ROWS, :] = s_re
    state_ref[HALF_ROWS:, :] = s_im


def _s5_readout(u, cw_ref, d_ref, y_ref, r_ref):
    tm = u.shape[0]
    tiles_per_half = MXU_DIM * 4 // LANES
    ys = []
    for k in range(2):
        cols = []
        for half in range(2):
            for jj in range(tiles_per_half):
                j = half * HALF_ROWS + k * tiles_per_half + jj
                cols.append(r_ref[pl.ds(j, tm, stride=SCAN_PITCH), :])
        s = jnp.concatenate(cols, axis=1).astype(BF16)
        ys.append(jnp.dot(s, cw_ref[k], preferred_element_type=F32))
    y = jnp.concatenate(ys, axis=1) + d_ref[...] * u.astype(F32)
    y_ref[...] = jax.nn.gelu(y).astype(BF16)


def _s5_weights(a_re, a_im, log_dt, b_re, b_im, c_re, c_im):
    g, n, h = SSM_GROUPS, SSM_STATE, SSM_GROUP
    gpt = MXU_DIM // h
    lam = lax.complex(a_re.astype(F32), a_im.astype(F32))
    dt = jnp.exp(log_dt.astype(F32))[:, None]
    a_bar = jnp.exp(lam * dt)
    b_bar = ((a_bar - 1.0) / lam)[..., None] * lax.complex(b_re.astype(F32), b_im.astype(F32))
    eye = jnp.eye(gpt, dtype=F32)

    def blk_in(w):
        w = w.reshape(g // gpt, gpt, n, h).transpose(0, 1, 3, 2)
        return jnp.einsum("kghn,gf->kghfn", w, eye).reshape(g // gpt, gpt * h, gpt * n)

    def blk_out(c):
        c = c.astype(F32).reshape(g // gpt, gpt, h, n).transpose(0, 1, 3, 2)
        return jnp.einsum("kgnh,gf->kgnfh", c, eye).reshape(g // gpt, gpt * n, gpt * h)

    bre = blk_in(jnp.real(b_bar)).astype(BF16)
    bim = blk_in(jnp.imag(b_bar)).astype(BF16)
    cw = jnp.concatenate([blk_out(c_re), -blk_out(c_im)], axis=1).astype(BF16)
    tile = (HALF_ROWS, LANES)
    return bre, bim, cw, jnp.real(a_bar).reshape(tile), jnp.imag(a_bar).reshape(tile)


def _back_kernel(yg_ref, ma_ref, gb_ref, x_ref, wa_ref, wb_ref, wo_ref, g2_ref, wr_ref, br_ref,
                 x1_ref, h2p_ref, route_ref, table_ref, counts_ref, h2_ref):
    tm = x_ref.shape[0]
    step = pl.program_id(0)

    @pl.when(step == 0)
    def _():
        counts_ref[...] = jnp.zeros_like(counts_ref)
        h2_ref[...] = jnp.zeros_like(h2_ref)

    h2_prev = h2_ref[...]
    h2_hi = h2_prev.astype(BF16)
    h2_lo = (h2_prev - h2_hi.astype(F32)).astype(BF16)
    both = jnp.dot(h2_hi, wr_ref[...], preferred_element_type=F32)
    logits = (both[:, :LANES]
              + (both[:, LANES:]
                 + jnp.dot(h2_lo, wr_ref[:, :LANES], preferred_element_type=F32))) + br_ref[...]
    lane = lax.broadcasted_iota(jnp.int32, logits.shape, 1)
    lane_f = lane.astype(F32)
    neg = jnp.float32(-jnp.inf)
    big = jnp.float32(LANES)
    is_grp = (lane >= ROUTE_GROUP_LANE) & (lane < ROUTE_GROUP_LANE + N_EXPERT_GROUPS)
    gl = jnp.where(is_grp, logits, neg)
    gmax = jnp.max(gl, axis=1, keepdims=True)
    gidx = jnp.min(jnp.where(gl == gmax, lane_f - ROUTE_GROUP_LANE, big), axis=1, keepdims=True)
    pg_top = 1.0 / jnp.sum(jnp.exp(gl - gmax), axis=1, keepdims=True)
    lane_grp = (lane // EXPERTS_PER_GROUP).astype(F32)
    el = jnp.where((lane < N_EXPERTS) & (lane_grp == gidx), logits, neg)
    l1 = jnp.max(el, axis=1, keepdims=True)
    i1 = jnp.min(jnp.where(el == l1, lane_f, big), axis=1, keepdims=True)
    el2 = jnp.where(lane_f == i1, neg, el)
    l2 = jnp.max(el2, axis=1, keepdims=True)
    i2 = jnp.min(jnp.where(el2 == l2, lane_f, big), axis=1, keepdims=True)
    r = jnp.exp(l2 - l1)
    w1 = pg_top / (1.0 + r)
    w2 = pg_top * r / (1.0 + r)
    oh1 = (lane_f == i1).astype(F32)
    oh2 = (lane_f == i2).astype(F32)
    picked = oh1 + oh2

    yg = yg_ref[...]
    yb = (jnp.dot(yg, wa_ref[...], preferred_element_type=F32)
          * jax.nn.sigmoid(jnp.dot(yg, wb_ref[...], preferred_element_type=F32)))
    mixed = (ma_ref[...].astype(F32) + gb_ref[...].astype(F32) * yb).astype(BF16)
    x1 = x_ref[...] + jnp.dot(mixed, wo_ref[...], preferred_element_type=F32)
    x1_ref[...] = _pack_rows(x1)
    h2 = _rms(x1, g2_ref[...])
    h2p_ref[...] = _pack_rows(h2)
    h2_ref[...] = h2

    rr = lax.broadcasted_iota(jnp.int32, (tm, tm), 0)
    cc = lax.broadcasted_iota(jnp.int32, (tm, tm), 1)
    before = (cc < rr).astype(BF16)
    prior = jnp.dot(before, picked.astype(BF16), preferred_element_type=F32) + counts_ref[...]
    rank1 = jnp.sum(oh1 * prior, axis=1, keepdims=True)
    rank2 = jnp.sum(oh2 * prior, axis=1, keepdims=True)
    counts_ref[...] += jnp.where(step > 0, jnp.sum(picked, axis=0, keepdims=True), 0.0)

    route = jnp.where(
        lane == 0, i1, jnp.where(lane == 1, i2, jnp.where(lane == 2, w1, jnp.where(
            lane == 3, w2, jnp.where(lane == 4, rank1, jnp.where(lane == 5, rank2, 0.0))))))
    route_ref[...] = route
    table_ref[...] = route.T[0:ROUTE_ROWS, :]


def _mixer_back(yg, ma, gb, x, w_glu_a, w_glu_b, w_o, norm_g, w_route, b_route, tm):
    t = x.shape[0]
    n = t // tm
    cur = lambda i: (jnp.minimum(i, n - 1), 0)
    lag = lambda i: (jnp.maximum(i - 1, 0), 0)
    return pl.pallas_call(
        _back_kernel,
        grid=(n + 1,),
        in_specs=[
            pl.BlockSpec((tm, SSM_WIDTH), cur),
            pl.BlockSpec((tm, D_MODEL), cur),
            pl.BlockSpec((tm, D_MODEL), cur),
            pl.BlockSpec((tm, D_MODEL), cur),
            _const_spec((SSM_WIDTH, D_MODEL)),
            _const_spec((SSM_WIDTH, D_MODEL)),
            _const_spec((D_MODEL, D_MODEL)),
            _const_spec((1, D_MODEL)),
            _const_spec((D_MODEL, 2 * LANES)),
            _const_spec((1, LANES)),
        ],
        out_specs=[pl.BlockSpec((tm, PACKED), cur), pl.BlockSpec((tm, PACKED), cur),
                   pl.BlockSpec((tm, LANES), lag),
                   pl.BlockSpec((ROUTE_ROWS, tm), lambda i: (0, jnp.maximum(i - 1, 0))),
                   pl.BlockSpec((1, LANES), lambda i: (0, 0))],
        out_shape=[
            jax.ShapeDtypeStruct((t, PACKED), I32),
            jax.ShapeDtypeStruct((t, PACKED), I32),
            jax.ShapeDtypeStruct((t, LANES), F32),
            jax.ShapeDtypeStruct((ROUTE_ROWS, t), F32),
            jax.ShapeDtypeStruct((1, LANES), F32),
        ],
        scratch_shapes=[pltpu.VMEM((tm, D_MODEL), F32)],
        compiler_params=_ARB,
        name="mixer_back",
    )(yg, ma, gb, x, w_glu_a, w_glu_b, w_o, norm_g, w_route, b_route)


_SC_MESH = dict(core_axis_name="c", subcore_axis_name="s")


def _sc_worker():
    return lax.axis_index("s") * SC_CORES + lax.axis_index("c")


def _sc_dispatch(rows, slot0, slot1, nslot):
    t, width = rows.shape
    workers = SC_CORES * SC_SUBCORES
    per_w = t // workers
    assert per_w % SC_CHUNK == 0
    nchunk = per_w // SC_CHUNK

    @functools.partial(
        pl.kernel,
        out_type=jax.ShapeDtypeStruct((nslot, width), rows.dtype),
        mesh=plsc.VectorSubcoreMesh(**_SC_MESH),
        scratch_types=[
            pltpu.VMEM((nchunk, SC_CHUNK), I32),
            pltpu.VMEM((nchunk, SC_CHUNK), I32),
            pltpu.VMEM((SC_CHUNK, width), rows.dtype),
            pltpu.SemaphoreType.DMA,
            pltpu.SemaphoreType.DMA,
        ],
        name="moe_dispatch",
    )
    def k(rows_hbm, s0_hbm, s1_hbm, out_hbm, i0_v, i1_v, rows_v, sem0, sem1):
        w = _sc_worker()
        pltpu.sync_copy(s0_hbm.at[w], i0_v)
        pltpu.sync_copy(s1_hbm.at[w], i1_v)

        @pl.loop(0, nchunk)
        def _(c):
            pltpu.sync_copy(rows_hbm.at[pl.ds(w * per_w + c * SC_CHUNK, SC_CHUNK)], rows_v)
            first = pltpu.async_copy(rows_v, out_hbm.at[i0_v.at[c]], sem0)
            second = pltpu.async_copy(rows_v, out_hbm.at[i1_v.at[c]], sem1)
            first.wait()
            second.wait()

    shape = (workers, nchunk, SC_CHUNK)
    return k(rows, slot0.reshape(shape), slot1.reshape(shape))


def _sc_gather(table, idx):
    n = idx.shape[0]
    width = table.shape[1]
    per_w = n // (SC_CORES * SC_SUBCORES)
    assert per_w % SC_CHUNK == 0

    @functools.partial(
        pl.kernel,
        out_type=jax.ShapeDtypeStruct((n, width), table.dtype),
        mesh=plsc.VectorSubcoreMesh(**_SC_MESH),
        scratch_types=[
            pltpu.VMEM((1, SC_CHUNK), I32),
            pltpu.VMEM((SC_CHUNK, width), table.dtype),
        ],
        name="moe_combine_gather",
    )
    def k(table_hbm, idx_hbm, out_hbm, idx_v, rows_v):
        base = _sc_worker() * per_w

        @pl.loop(0, per_w // SC_CHUNK)
        def _(c):
            off = base + c * SC_CHUNK
            pltpu.sync_copy(idx_hbm.at[:, pl.ds(off, SC_CHUNK)], idx_v)
            pltpu.sync_copy(table_hbm.at[idx_v.at[0]], rows_v)
            pltpu.sync_copy(rows_v, out_hbm.at[pl.ds(off, SC_CHUNK)])

    return k(table, idx.reshape(1, n))


def _moe_kernel(first_ref, last_ref, count_ref, total_ref,
                xs_hbm, wg_ref, wu_ref, wd_ref, out_hbm,
                xbuf0, xbuf1, obuf0, obuf1, xsem, osem):
    e = pl.program_id(0)
    ch = MOE_TILE
    half = ch // 2
    first, last, count, total = first_ref[e], last_ref[e], count_ref[e], total_ref[0]
    xbufs = (xbuf0, xbuf1)
    obufs = (obuf0, obuf1)

    def x_copy(c, s):
        rows = pl.ds(pl.multiple_of(c * ch, ch), ch)
        return pltpu.make_async_copy(xs_hbm.at[rows], xbufs[s], xsem.at[s])

    def o_copy(c, s):
        rows = pl.ds(pl.multiple_of(c * ch, ch), ch)
        return pltpu.make_async_copy(obufs[s], out_hbm.at[rows], osem.at[s])

    @pl.when(e == 0)
    def _():
        x_copy(0, 0).start()

    def experts_rows(s, rows, valid):
        row = lax.broadcasted_iota(jnp.int32, (rows, PACKED), 0)
        words = jnp.where(row < valid, xbufs[s][0:rows, :], 0)
        lo, hi = _unpack_rows(words)
        xs = jnp.concatenate([lo.astype(BF16), hi.astype(BF16)], axis=1)
        gate = jnp.dot(xs, wg_ref[0], preferred_element_type=F32)
        up = jnp.dot(xs, wu_ref[0], preferred_element_type=F32)
        act = (gate * jax.nn.sigmoid(gate) * up).astype(BF16)
        obufs[s][0:rows, :] = _pack_rows(jnp.dot(act, wd_ref[0], preferred_element_type=F32))

    def chunk_in_slot(c, s):
        x_copy(c, s).wait()

        @pl.when(c + 1 < total)
        def _():
            x_copy(c + 1, 1 - s).start()

        @pl.when(c >= 2)
        def _():
            o_copy(c - 2, s).wait()

        valid = count - (c - first) * ch

        sizes = (ch // 4, half, ch)
        for lo, rows in zip((0,) + sizes[:-1], sizes):
            @pl.when((valid > lo) & ((valid <= rows) | (rows == ch)))
            def _():
                experts_rows(s, rows, valid)
                if rows < ch:
                    obufs[s][rows:, :] = jnp.zeros((ch - rows, PACKED), I32)

        o_copy(c, s).start()

    def chunk(c, carry):
        for s in range(2):
            @pl.when(c % 2 == s)
            def _():
                chunk_in_slot(c, s)
        return carry

    lax.fori_loop(first, last, chunk, 0)

    @pl.when(e == pl.num_programs(0) - 1)
    def _():
        for s in range(2):
            @pl.when(((total - 1) % 2 == s) | (total >= 2))
            def _():
                o_copy(0, s).wait()


def _moe_experts(plan, xs, w_gate, w_up, w_down):
    nslot = xs.shape[0]
    any_spec = pl.BlockSpec(memory_space=pl.ANY)
    grid_spec = pltpu.PrefetchScalarGridSpec(
        num_scalar_prefetch=len(plan),
        grid=(N_EXPERTS,),
        in_specs=[
            any_spec,
            pl.BlockSpec((1, D_MODEL, EXPERT_FF), lambda e, *_: (e, 0, 0)),
            pl.BlockSpec((1, D_MODEL, EXPERT_FF), lambda e, *_: (e, 0, 0)),
            pl.BlockSpec((1, EXPERT_FF, D_MODEL), lambda e, *_: (e, 0, 0)),
        ],
        out_specs=any_spec,
        scratch_shapes=[
            pltpu.VMEM((MOE_TILE, PACKED), I32),
            pltpu.VMEM((MOE_TILE, PACKED), I32),
            pltpu.VMEM((MOE_TILE, PACKED), I32),
            pltpu.VMEM((MOE_TILE, PACKED), I32),
            pltpu.SemaphoreType.DMA((2,)),
            pltpu.SemaphoreType.DMA((2,)),
        ],
    )
    return pl.pallas_call(
        _moe_kernel,
        grid_spec=grid_spec,
        out_shape=jax.ShapeDtypeStruct((nslot, PACKED), I32),
        compiler_params=_ARB,
        name="moe_experts",
    )(*plan, xs, w_gate, w_up, w_down)


def _dispatch_plan(table, counts, tm):
    e_ids = table[0:2].astype(I32)
    ranks = table[4:6].astype(I32)
    counts = counts[0, :N_EXPERTS].astype(I32)
    padded = ((counts + tm - 1) // tm) * tm
    ends = jnp.cumsum(padded)
    starts = ends - padded
    experts = jnp.arange(N_EXPERTS, dtype=I32)
    slot = jnp.sum(jnp.where(e_ids[..., None] == experts, starts, 0), axis=-1) + ranks
    plan = (starts // tm, ends // tm, counts, ends[-1:] // tm)
    return slot, tuple(p.astype(I32) for p in plan)


def _final_kernel(x1_ref, y0_ref, y1_ref, route_ref, g_ref, *rest):
    out_ref = rest[-1]
    route = route_ref[...]
    w1 = route[:, 2:3]
    w2 = route[:, 3:4]
    x_lo, x_hi = _unpack_rows(x1_ref[...])
    a_lo, a_hi = _unpack_rows(y0_ref[...])
    b_lo, b_hi = _unpack_rows(y1_ref[...])
    x2 = jnp.concatenate([x_lo + (w1 * a_lo + w2 * b_lo), x_hi + (w1 * a_hi + w2 * b_hi)], axis=1)
    out_ref[...] = _rms(x2, g_ref[...])


def _final(x1, ycat, route, norm_g, tm, chunk, prev):
    t = x1.shape[0]
    nblk = t // tm // FINAL_CHUNKS
    off = chunk * nblk
    in_specs = [
        pl.BlockSpec((tm, PACKED), lambda i: (i + off, 0)),
        pl.BlockSpec((tm, PACKED), lambda i: (i, 0)),
        pl.BlockSpec((tm, PACKED), lambda i: (i + nblk, 0)),
        pl.BlockSpec((tm, LANES), lambda i: (i + off, 0)),
        _const_spec((1, D_MODEL)),
    ]
    args = [x1, ycat, ycat, route, norm_g]
    aliases = {}
    if prev is not None:
        in_specs.append(pl.BlockSpec(memory_space=pl.ANY))
        args.append(prev)
        aliases = {len(args) - 1: 0}
    return pl.pallas_call(
        _final_kernel,
        grid=(nblk,),
        in_specs=in_specs,
        out_specs=pl.BlockSpec((tm, D_MODEL), lambda i: (i + off, 0)),
        out_shape=jax.ShapeDtypeStruct((t, D_MODEL), F32),
        input_output_aliases=aliases,
        compiler_params=_ARB,
        name="final_norm",
    )(*args)


def kernel(x, norm_mix, w_in, conv_w, conv_b, w_conv_out, ssm_a_re, ssm_a_im, ssm_log_dt,
           ssm_b_re, ssm_b_im, ssm_c_re, ssm_c_im, ssm_d, w_glu_a, w_glu_b, gate_bias, w_o,
           norm_ffn, w_route_group, b_route_group, w_route_expert, b_route_expert,
           w_gate, w_up, w_down, norm_final):
    bsz, length, d = x.shape
    assert d == D_MODEL and norm_mix.shape[0] == 1
    t = bsz * length
    tm = TOKEN_TILE
    assert bsz == 1 and t % (FINAL_TILE * FINAL_CHUNKS) == 0
    xt = x.reshape(t, d)
    row = lambda a: a.reshape(1, -1).astype(F32)

    s5_weights = _s5_weights(
        ssm_a_re[0], ssm_a_im[0], ssm_log_dt[0], ssm_b_re[0], ssm_b_im[0],
        ssm_c_re[0], ssm_c_im[0])
    ma, gb, yg, wg_bf, wu_bf, wd_bf = _mixer_front(
        xt, row(norm_mix[0]), w_in[0].astype(BF16), conv_w[0].reshape(3, CONV_WIDTH),
        row(conv_b[0]), w_conv_out[0].astype(BF16), gate_bias[0], s5_weights, row(ssm_d[0]),
        w_gate[0], w_up[0], w_down[0], FRONT_TILE)

    pad = LANES - N_EXPERTS - N_EXPERT_GROUPS
    w_route = jnp.concatenate(
        [w_route_expert[0], w_route_group[0], jnp.zeros((d, pad), F32)], axis=1)
    w_route_hi = w_route.astype(BF16)
    w_route_lo = (w_route - w_route_hi.astype(F32)).astype(BF16)
    b_route = jnp.concatenate(
        [b_route_expert[0], b_route_group[0], jnp.zeros((pad,), F32)]).reshape(1, LANES)
    x1, h2p, route, table, counts = _mixer_back(
        yg, ma, gb, xt, w_glu_a[0].astype(BF16), w_glu_b[0].astype(BF16), w_o[0].astype(BF16),
        row(norm_ffn[0]), jnp.concatenate([w_route_hi, w_route_lo], axis=1), b_route, tm)

    slot, plan = _dispatch_plan(table, counts, MOE_TILE)
    nslot = 2 * t + N_EXPERTS * MOE_TILE
    xs = _sc_dispatch(h2p, slot[0], slot[1], nslot)
    ys = _moe_experts(plan, xs, wg_bf, wu_bf, wd_bf)
    out = None
    tc = t // FINAL_CHUNKS
    for k in range(FINAL_CHUNKS):
        ycat = _sc_gather(ys, slot[:, k * tc:(k + 1) * tc].reshape(-1))
        out = _final(x1, ycat, route, row(norm_final), FINAL_TILE, k, out)
    return out.reshape(bsz, length, d)
```

```python
import functools

import jax
import jax.numpy as jnp
from jax import lax
from jax.experimental import pallas as pl
from jax.experimental.pallas import tpu as pltpu
from jax.experimental.pallas import tpu_sc as plsc

F32 = jnp.float32
BF16 = jnp.bfloat16
I32 = jnp.int32

D_MODEL = 1024
CONV_WIDTH = 1024
SSM_WIDTH = 512
SSM_GROUP = 16
SSM_GROUPS = 32
SSM_STATE = 64
N_EXPERT_GROUPS = 4
EXPERTS_PER_GROUP = 8
N_EXPERTS = 32
EXPERT_FF = 512
EPS = 1e-6

LANES = 128
MXU_DIM = 256
TOKEN_TILE = 512
FINAL_TILE = 1024
FRONT_TILE = 256
MOE_TILE = 512
SCAN_PITCH = 36
STATE_ROWS = 2 * SSM_GROUPS * SSM_STATE // LANES
HALF_ROWS = STATE_ROWS // 2
ROUTE_GROUP_LANE = N_EXPERTS
ROUTE_ROWS = 8
FINAL_CHUNKS = 4
PACKED = D_MODEL // 2
HI_MASK = -65536
SC_CORES = 2
SC_SUBCORES = 16
SC_CHUNK = 128
VMEM_LIMIT = 56 * 1024 * 1024

_ARB = pltpu.CompilerParams(dimension_semantics=("arbitrary",), vmem_limit_bytes=VMEM_LIMIT)


def _const_spec(shape):
    nd = len(shape)
    return pl.BlockSpec(shape, lambda i, *_: (0,) * nd, pipeline_mode=pl.Buffered(1))


def _row_spec(tm, width):
    return pl.BlockSpec((tm, width), lambda i, *_: (i, 0))


def _rms(x, g):
    ms = jnp.mean(x * x, axis=-1, keepdims=True)
    return x * lax.rsqrt(ms + EPS) * g


def _pack_rows(v):
    bits = lax.bitcast_convert_type(v.astype(BF16).astype(F32), I32)
    lo = lax.shift_right_logical(bits[:, :PACKED], 16)
    hi = bits[:, PACKED:] & HI_MASK
    return lo | hi


def _unpack_rows(w):
    lo = lax.bitcast_convert_type(lax.shift_left(w, 16), F32)
    hi = lax.bitcast_convert_type(w & HI_MASK, F32)
    return lo, hi


def _front_kernel(x_ref, g_ref, win_ref, cw_ref, cb_ref, wco_ref, gbias_ref,
                  bt_ref, ct_ref, are_ref, aim_ref, d_ref, wg_ref, wu_ref, wd_ref,
                  ma_ref, gb_ref, y_ref, wgb_ref, wub_ref, wdb_ref,
                  carry_ref, u_ref, r_ref, state_ref, bre_ref, bim_ref, cw2_ref):
    tm = x_ref.shape[0]
    wgb_ref[...] = wg_ref[...].astype(BF16)
    wub_ref[...] = wu_ref[...].astype(BF16)
    wdb_ref[...] = wd_ref[...].astype(BF16)
    c0, c1, c2 = CONV_WIDTH, 2 * CONV_WIDTH, 3 * CONV_WIDTH
    c3 = c2 + SSM_WIDTH

    @pl.when(pl.program_id(0) == 0)
    def _():
        carry_ref[...] = jnp.zeros_like(carry_ref)
        u_ref[...] = jnp.zeros_like(u_ref)
        state_ref[...] = jnp.zeros_like(state_ref)
        _s5_expand_weights(bt_ref, ct_ref, bre_ref, bim_ref, cw2_ref)

    h = _rms(x_ref[...], g_ref[...]).astype(BF16)
    u_prev = u_ref[...]
    _s5_scan_tile(u_prev, bre_ref, bim_ref, are_ref, aim_ref, r_ref, state_ref)

    def proj(lo, hi):
        return jnp.dot(h, win_ref[:, lo:hi], preferred_element_type=F32)

    v = proj(c1, c2) * proj(0, c0)
    row = lax.broadcasted_iota(jnp.int32, v.shape, 0)
    prev1 = carry_ref[7:8, :]
    prev2 = carry_ref[6:7, :]
    v1 = jnp.where(row == 0, prev1, pltpu.roll(v, 1, 0))
    v2 = jnp.where(row == 0, prev2, jnp.where(row == 1, prev1, pltpu.roll(v, 2, 0)))
    carry_ref[...] = v[tm - 8:, :]
    y = cw_ref[0:1, :] * v2 + cw_ref[1:2, :] * v1 + cw_ref[2:3, :] * v + cb_ref[...]
    z = (proj(c0, c1) * y).astype(BF16)
    ya = jnp.dot(z, wco_ref[...], preferred_element_type=F32)
    ga = jax.nn.sigmoid(proj(c3, c3 + D_MODEL) + gbias_ref[0:1, :])
    ma_ref[...] = (ga * ya).astype(BF16)
    gb = jax.nn.sigmoid(proj(c3 + D_MODEL, c3 + 2 * D_MODEL) + gbias_ref[1:2, :])
    gb_ref[...] = gb.astype(BF16)
    u_new = proj(c2, c3).astype(BF16)
    _s5_readout(u_prev, cw2_ref, d_ref, y_ref, r_ref)
    u_ref[...] = u_new


def _mixer_front(x, norm_g, w_in, conv_w, conv_b, w_conv_out, gate_bias, s5_weights, d_skip,
                 w_gate, w_up, w_down, tm):
    bt, ct, a_re, a_im = s5_weights
    halves = SSM_GROUPS * SSM_GROUP // MXU_DIM
    t = x.shape[0]
    in_cols = w_in.shape[1]
    n = t // tm
    last = lambda i: (jnp.minimum(i, n - 1), 0)
    lag = lambda i: (jnp.maximum(i - 1, 0), 0)
    split = max(k for k in (1, 2, 4, 8) if N_EXPERTS * k <= n + 1)
    piece = lambda i: (jnp.minimum(i, N_EXPERTS * split - 1) // split,
                       jnp.minimum(i, N_EXPERTS * split - 1) % split, 0)
    w_specs = [pl.BlockSpec((1, D_MODEL // split, EXPERT_FF), piece),
               pl.BlockSpec((1, D_MODEL // split, EXPERT_FF), piece),
               pl.BlockSpec((1, EXPERT_FF // split, D_MODEL), piece)]
    return pl.pallas_call(
        _front_kernel,
        grid=(n + 1,),
        in_specs=[
            pl.BlockSpec((tm, D_MODEL), last),
            _const_spec((1, D_MODEL)),
            _const_spec((D_MODEL, in_cols)),
            _const_spec((3, CONV_WIDTH)),
            _const_spec((1, CONV_WIDTH)),
            _const_spec((CONV_WIDTH, D_MODEL)),
            _const_spec((2, D_MODEL)),
            _const_spec(bt.shape),
            _const_spec(ct.shape),
            _const_spec(a_re.shape),
            _const_spec(a_im.shape),
            _const_spec((1, SSM_WIDTH)),
        ] + w_specs,
        out_specs=[_row_spec(tm, D_MODEL), _row_spec(tm, D_MODEL),
                   pl.BlockSpec((tm, SSM_WIDTH), lag)] + w_specs,
        out_shape=[
            jax.ShapeDtypeStruct((t + tm, D_MODEL), BF16),
            jax.ShapeDtypeStruct((t + tm, D_MODEL), BF16),
            jax.ShapeDtypeStruct((t, SSM_WIDTH), BF16),
            jax.ShapeDtypeStruct(w_gate.shape, BF16),
            jax.ShapeDtypeStruct(w_up.shape, BF16),
            jax.ShapeDtypeStruct(w_down.shape, BF16),
        ],
        scratch_shapes=[
            pltpu.VMEM((8, CONV_WIDTH), F32),
            pltpu.VMEM((tm, SSM_WIDTH), BF16),
            pltpu.VMEM((tm * SCAN_PITCH, LANES), F32),
            pltpu.VMEM((STATE_ROWS, LANES), F32),
            pltpu.VMEM((halves, MXU_DIM, SSM_GROUPS * SSM_STATE // halves), BF16),
            pltpu.VMEM((halves, MXU_DIM, SSM_GROUPS * SSM_STATE // halves), BF16),
            pltpu.VMEM((halves, 2 * SSM_GROUPS * SSM_STATE // halves, MXU_DIM), BF16),
        ],
        compiler_params=_ARB,
        name="mixer_front",
    )(x, norm_g, w_in, conv_w, conv_b, w_conv_out, gate_bias, bt, ct, a_re, a_im, d_skip,
      w_gate, w_up, w_down)


def _s5_scan_tile(u, bre_ref, bim_ref, are_ref, aim_ref, r_ref, state_ref):
    tm = u.shape[0]
    tiles_per_half = MXU_DIM * 4 // LANES

    for k in range(2):
        uk = u[:, k * MXU_DIM:(k + 1) * MXU_DIM]
        re = jnp.dot(uk, bre_ref[k], preferred_element_type=F32)
        im = jnp.dot(uk, bim_ref[k], preferred_element_type=F32)
        for jj in range(tiles_per_half):
            j = k * tiles_per_half + jj
            sl = slice(jj * LANES, (jj + 1) * LANES)
            r_ref[pl.ds(j, tm, stride=SCAN_PITCH), :] = re[:, sl]
            r_ref[pl.ds(HALF_ROWS + j, tm, stride=SCAN_PITCH), :] = im[:, sl]

    a_re = are_ref[...]
    a_im = aim_ref[...]

    s_re = state_ref[0:HALF_ROWS, :]
    s_im = state_ref[HALF_ROWS:, :]
    for t in range(tm):
        base = t * SCAN_PITCH
        b_re = r_ref[pl.ds(base, HALF_ROWS), :]
        b_im = r_ref[pl.ds(base + HALF_ROWS, HALF_ROWS), :]
        s_re, s_im = (a_re * s_re - a_im * s_im + b_re,
                      a_re * s_im + a_im * s_re + b_im)
        r_ref[pl.ds(base, HALF_ROWS), :] = s_re
        r_ref[pl.ds(base + HALF_ROWS, HALF_ROWS), :] = s_im
    state_ref[0:HALF_ROWS, :] = s_re
    state_ref[HALF_ROWS:, :] = s_im


def _s5_readout(u, cw_ref, d_ref, y_ref, r_ref):
    tm = u.shape[0]
    tiles_per_half = MXU_DIM * 4 // LANES
    ys = []
    for k in range(2):
        cols = []
        for half in range(2):
            for jj in range(tiles_per_half):
                j = half * HALF_ROWS + k * tiles_per_half + jj
                cols.append(r_ref[pl.ds(j, tm, stride=SCAN_PITCH), :])
        s = jnp.concatenate(cols, axis=1).astype(BF16)
        ys.append(jnp.dot(s, cw_ref[k], preferred_element_type=F32))
    y = jnp.concatenate(ys, axis=1) + d_ref[...] * u.astype(F32)
    y_ref[...] = jax.nn.gelu(y).astype(BF16)


def _s5_expand_weights(bt_ref, ct_ref, bre_ref, bim_ref, cw_ref):
    n, h = SSM_STATE, SSM_GROUP
    kb, nb = bre_ref.shape[1], bre_ref.shape[2]
    kc, nc = cw_ref.shape[1] // 2, cw_ref.shape[2]
    iota = lambda shape, d: lax.broadcasted_iota(jnp.int32, shape, d)
    tile_b = (iota((n, nb), 1) % n == iota((n, nb), 0)).astype(BF16)
    keep_b = iota((kb, nb), 0) // h == iota((kb, nb), 1) // n
    tile_c = (iota((h, nc), 1) % h == iota((h, nc), 0)).astype(BF16)
    keep_c = iota((kc, nc), 0) // n == iota((kc, nc), 1) // h
    for k in range(bre_ref.shape[0]):
        for part, dst in enumerate((bre_ref, bim_ref)):
            spread = jnp.dot(bt_ref[part, k].astype(BF16), tile_b, preferred_element_type=F32)
            dst[k] = jnp.where(keep_b, spread, 0.0).astype(BF16)
        for part in range(2):
            spread = jnp.dot(ct_ref[part, k].astype(BF16), tile_c, preferred_element_type=F32)
            cw_ref[k, part * kc:(part + 1) * kc, :] = jnp.where(keep_c, spread, 0.0).astype(BF16)


def _s5_weights(a_re, a_im, log_dt, b_re, b_im, c_re, c_im):
    g, n, h = SSM_GROUPS, SSM_STATE, SSM_GROUP
    halves = g * h // MXU_DIM
    lam = lax.complex(a_re.astype(F32), a_im.astype(F32))
    dt = jnp.exp(log_dt.astype(F32))[:, None]
    a_bar = jnp.exp(lam * dt)
    b_bar = ((a_bar - 1.0) / lam)[..., None] * lax.complex(b_re.astype(F32), b_im.astype(F32))
    bt = jnp.stack([jnp.real(b_bar), jnp.imag(b_bar)])
    bt = bt.transpose(0, 1, 3, 2).reshape(2, halves, g * h // halves, n)
    ct = jnp.stack([c_re.astype(F32), -c_im.astype(F32)])
    ct = ct.transpose(0, 1, 3, 2).reshape(2, halves, g * n // halves, h)
    tile = (HALF_ROWS, LANES)
    return bt, ct, jnp.real(a_bar).reshape(tile), jnp.imag(a_bar).reshape(tile)


def _back_kernel(yg_ref, ma_ref, gb_ref, x_ref, wa_ref, wb_ref, wo_ref, g2_ref, wr_ref, br_ref,
                 x1_ref, h2p_ref, route_ref, table_ref, counts_ref, h2_ref):
    tm = x_ref.shape[0]
    step = pl.program_id(0)

    @pl.when(step == 0)
    def _():
        counts_ref[...] = jnp.zeros_like(counts_ref)
        h2_ref[...] = jnp.zeros_like(h2_ref)

    h2_prev = h2_ref[...]
    h2_hi = h2_prev.astype(BF16)
    h2_lo = (h2_prev - h2_hi.astype(F32)).astype(BF16)
    both = jnp.dot(h2_hi, wr_ref[...], preferred_element_type=F32)
    logits = (both[:, :LANES]
              + (both[:, LANES:]
                 + jnp.dot(h2_lo, wr_ref[:, :LANES], preferred_element_type=F32))) + br_ref[...]
    lane = lax.broadcasted_iota(jnp.int32, logits.shape, 1)
    lane_f = lane.astype(F32)
    neg = jnp.float32(-jnp.inf)
    big = jnp.float32(LANES)
    is_grp = (lane >= ROUTE_GROUP_LANE) & (lane < ROUTE_GROUP_LANE + N_EXPERT_GROUPS)
    gl = jnp.where(is_grp, logits, neg)
    gmax = jnp.max(gl, axis=1, keepdims=True)
    gidx = jnp.min(jnp.where(gl == gmax, lane_f - ROUTE_GROUP_LANE, big), axis=1, keepdims=True)
    pg_top = 1.0 / jnp.sum(jnp.exp(gl - gmax), axis=1, keepdims=True)
    lane_grp = (lane // EXPERTS_PER_GROUP).astype(F32)
    el = jnp.where((lane < N_EXPERTS) & (lane_grp == gidx), logits, neg)
    l1 = jnp.max(el, axis=1, keepdims=True)
    i1 = jnp.min(jnp.where(el == l1, lane_f, big), axis=1, keepdims=True)
    el2 = jnp.where(lane_f == i1, neg, el)
    l2 = jnp.max(el2, axis=1, keepdims=True)
    i2 = jnp.min(jnp.where(el2 == l2, lane_f, big), axis=1, keepdims=True)
    r = jnp.exp(l2 - l1)
    w1 = pg_top / (1.0 + r)
    w2 = pg_top * r / (1.0 + r)
    oh1 = (lane_f == i1).astype(F32)
    oh2 = (lane_f == i2).astype(F32)
    picked = oh1 + oh2

    yg = yg_ref[...]
    yb = (jnp.dot(yg, wa_ref[...], preferred_element_type=F32)
          * jax.nn.sigmoid(jnp.dot(yg, wb_ref[...], preferred_element_type=F32)))
    mixed = (ma_ref[...].astype(F32) + gb_ref[...].astype(F32) * yb).astype(BF16)
    x1 = x_ref[...] + jnp.dot(mixed, wo_ref[...], preferred_element_type=F32)
    x1_ref[...] = _pack_rows(x1)
    h2 = _rms(x1, g2_ref[...])
    h2p_ref[...] = _pack_rows(h2)
    h2_ref[...] = h2

    rr = lax.broadcasted_iota(jnp.int32, (tm, tm), 0)
    cc = lax.broadcasted_iota(jnp.int32, (tm, tm), 1)
    before = (cc < rr).astype(BF16)
    prior = jnp.dot(before, picked.astype(BF16), preferred_element_type=F32) + counts_ref[...]
    rank1 = jnp.sum(oh1 * prior, axis=1, keepdims=True)
    rank2 = jnp.sum(oh2 * prior, axis=1, keepdims=True)
    counts_ref[...] += jnp.where(step > 0, jnp.sum(picked, axis=0, keepdims=True), 0.0)

    route = jnp.where(
        lane == 0, i1, jnp.where(lane == 1, i2, jnp.where(lane == 2, w1, jnp.where(
            lane == 3, w2, jnp.where(lane == 4, rank1, jnp.where(lane == 5, rank2, 0.0))))))
    route_ref[...] = route
    table_ref[...] = route.T[0:ROUTE_ROWS, :]


def _mixer_back(yg, ma, gb, x, w_glu_a, w_glu_b, w_o, norm_g, w_route, b_route, tm):
    t = x.shape[0]
    n = t // tm
    cur = lambda i: (jnp.minimum(i, n - 1), 0)
    lag = lambda i: (jnp.maximum(i - 1, 0), 0)
    return pl.pallas_call(
        _back_kernel,
        grid=(n + 1,),
        in_specs=[
            pl.BlockSpec((tm, SSM_WIDTH), cur),
            pl.BlockSpec((tm, D_MODEL), cur),
            pl.BlockSpec((tm, D_MODEL), cur),
            pl.BlockSpec((tm, D_MODEL), cur),
            _const_spec((SSM_WIDTH, D_MODEL)),
            _const_spec((SSM_WIDTH, D_MODEL)),
            _const_spec((D_MODEL, D_MODEL)),
            _const_spec((1, D_MODEL)),
            _const_spec((D_MODEL, 2 * LANES)),
            _const_spec((1, LANES)),
        ],
        out_specs=[pl.BlockSpec((tm, PACKED), cur), pl.BlockSpec((tm, PACKED), cur),
                   pl.BlockSpec((tm, LANES), lag),
                   pl.BlockSpec((ROUTE_ROWS, tm), lambda i: (0, jnp.maximum(i - 1, 0))),
                   pl.BlockSpec((1, LANES), lambda i: (0, 0))],
        out_shape=[
            jax.ShapeDtypeStruct((t, PACKED), I32),
            jax.ShapeDtypeStruct((t, PACKED), I32),
            jax.ShapeDtypeStruct((t, LANES), F32),
            jax.ShapeDtypeStruct((ROUTE_ROWS, t), F32),
            jax.ShapeDtypeStruct((1, LANES), F32),
        ],
        scratch_shapes=[pltpu.VMEM((tm, D_MODEL), F32)],
        compiler_params=_ARB,
        name="mixer_back",
    )(yg, ma, gb, x, w_glu_a, w_glu_b, w_o, norm_g, w_route, b_route)


_SC_MESH = dict(core_axis_name="c", subcore_axis_name="s")


def _sc_worker():
    return lax.axis_index("s") * SC_CORES + lax.axis_index("c")


def _sc_dispatch(rows, slot0, slot1, nslot):
    t, width = rows.shape
    workers = SC_CORES * SC_SUBCORES
    per_w = t // workers
    assert per_w % SC_CHUNK == 0
    nchunk = per_w // SC_CHUNK

    @functools.partial(
        pl.kernel,
        out_type=jax.ShapeDtypeStruct((nslot, width), rows.dtype),
        mesh=plsc.VectorSubcoreMesh(**_SC_MESH),
        scratch_types=[
            pltpu.VMEM((nchunk, SC_CHUNK), I32),
            pltpu.VMEM((nchunk, SC_CHUNK), I32),
            pltpu.VMEM((SC_CHUNK, width), rows.dtype),
            pltpu.SemaphoreType.DMA,
            pltpu.SemaphoreType.DMA,
        ],
        name="moe_dispatch",
    )
    def k(rows_hbm, s0_hbm, s1_hbm, out_hbm, i0_v, i1_v, rows_v, sem0, sem1):
        w = _sc_worker()
        pltpu.sync_copy(s0_hbm.at[w], i0_v)
        pltpu.sync_copy(s1_hbm.at[w], i1_v)

        @pl.loop(0, nchunk)
        def _(c):
            pltpu.sync_copy(rows_hbm.at[pl.ds(w * per_w + c * SC_CHUNK, SC_CHUNK)], rows_v)
            first = pltpu.async_copy(rows_v, out_hbm.at[i0_v.at[c]], sem0)
            second = pltpu.async_copy(rows_v, out_hbm.at[i1_v.at[c]], sem1)
            first.wait()
            second.wait()

    shape = (workers, nchunk, SC_CHUNK)
    return k(rows, slot0.reshape(shape), slot1.reshape(shape))


def _sc_gather(table, idx):
    n = idx.shape[0]
    width = table.shape[1]
    per_w = n // (SC_CORES * SC_SUBCORES)
    assert per_w % SC_CHUNK == 0

    @functools.partial(
        pl.kernel,
        out_type=jax.ShapeDtypeStruct((n, width), table.dtype),
        mesh=plsc.VectorSubcoreMesh(**_SC_MESH),
        scratch_types=[
            pltpu.VMEM((1, SC_CHUNK), I32),
            pltpu.VMEM((SC_CHUNK, width), table.dtype),
        ],
        name="moe_combine_gather",
    )
    def k(table_hbm, idx_hbm, out_hbm, idx_v, rows_v):
        base = _sc_worker() * per_w

        @pl.loop(0, per_w // SC_CHUNK)
        def _(c):
            off = base + c * SC_CHUNK
            pltpu.sync_copy(idx_hbm.at[:, pl.ds(off, SC_CHUNK)], idx_v)
            pltpu.sync_copy(table_hbm.at[idx_v.at[0]], rows_v)
            pltpu.sync_copy(rows_v, out_hbm.at[pl.ds(off, SC_CHUNK)])

    return k(table, idx.reshape(1, n))


def _moe_kernel(first_ref, last_ref, count_ref, total_ref,
                xs_hbm, wg_ref, wu_ref, wd_ref, out_hbm,
                xbuf0, xbuf1, obuf0, obuf1, xsem, osem):
    e = pl.program_id(0)
    ch = MOE_TILE
    half = ch // 2
    first, last, count, total = first_ref[e], last_ref[e], count_ref[e], total_ref[0]
    xbufs = (xbuf0, xbuf1)
    obufs = (obuf0, obuf1)

    def x_copy(c, s):
        rows = pl.ds(pl.multiple_of(c * ch, ch), ch)
        return pltpu.make_async_copy(xs_hbm.at[rows], xbufs[s], xsem.at[s])

    def o_copy(c, s):
        rows = pl.ds(pl.multiple_of(c * ch, ch), ch)
        return pltpu.make_async_copy(obufs[s], out_hbm.at[rows], osem.at[s])

    @pl.when(e == 0)
    def _():
        x_copy(0, 0).start()

    def experts_rows(s, rows, valid):
        row = lax.broadcasted_iota(jnp.int32, (rows, PACKED), 0)
        words = jnp.where(row < valid, xbufs[s][0:rows, :], 0)
        lo, hi = _unpack_rows(words)
        xs = jnp.concatenate([lo.astype(BF16), hi.astype(BF16)], axis=1)
        gate = jnp.dot(xs, wg_ref[0], preferred_element_type=F32)
        up = jnp.dot(xs, wu_ref[0], preferred_element_type=F32)
        act = (gate * jax.nn.sigmoid(gate) * up).astype(BF16)
        obufs[s][0:rows, :] = _pack_rows(jnp.dot(act, wd_ref[0], preferred_element_type=F32))

    def chunk_in_slot(c, s):
        x_copy(c, s).wait()

        @pl.when(c + 1 < total)
        def _():
            x_copy(c + 1, 1 - s).start()

        @pl.when(c >= 2)
        def _():
            o_copy(c - 2, s).wait()

        valid = count - (c - first) * ch

        sizes = (ch // 4, half, ch)
        for lo, rows in zip((0,) + sizes[:-1], sizes):
            @pl.when((valid > lo) & ((valid <= rows) | (rows == ch)))
            def _():
                experts_rows(s, rows, valid)
                if rows < ch:
                    obufs[s][rows:, :] = jnp.zeros((ch - rows, PACKED), I32)

        o_copy(c, s).start()

    def chunk(c, carry):
        for s in range(2):
            @pl.when(c % 2 == s)
            def _():
                chunk_in_slot(c, s)
        return carry

    lax.fori_loop(first, last, chunk, 0)

    @pl.when(e == pl.num_programs(0) - 1)
    def _():
        for s in range(2):
            @pl.when(((total - 1) % 2 == s) | (total >= 2))
            def _():
                o_copy(0, s).wait()


def _moe_experts(plan, xs, w_gate, w_up, w_down):
    nslot = xs.shape[0]
    any_spec = pl.BlockSpec(memory_space=pl.ANY)
    grid_spec = pltpu.PrefetchScalarGridSpec(
        num_scalar_prefetch=len(plan),
        grid=(N_EXPERTS,),
        in_specs=[
            any_spec,
            pl.BlockSpec((1, D_MODEL, EXPERT_FF), lambda e, *_: (e, 0, 0)),
            pl.BlockSpec((1, D_MODEL, EXPERT_FF), lambda e, *_: (e, 0, 0)),
            pl.BlockSpec((1, EXPERT_FF, D_MODEL), lambda e, *_: (e, 0, 0)),
        ],
        out_specs=any_spec,
        scratch_shapes=[
            pltpu.VMEM((MOE_TILE, PACKED), I32),
            pltpu.VMEM((MOE_TILE, PACKED), I32),
            pltpu.VMEM((MOE_TILE, PACKED), I32),
            pltpu.VMEM((MOE_TILE, PACKED), I32),
            pltpu.SemaphoreType.DMA((2,)),
            pltpu.SemaphoreType.DMA((2,)),
        ],
    )
    return pl.pallas_call(
        _moe_kernel,
        grid_spec=grid_spec,
        out_shape=jax.ShapeDtypeStruct((nslot, PACKED), I32),
        compiler_params=_ARB,
        name="moe_experts",
    )(*plan, xs, w_gate, w_up, w_down)


def _dispatch_plan(table, counts, tm):
    e_ids = table[0:2].astype(I32)
    ranks = table[4:6].astype(I32)
    counts = counts[0, :N_EXPERTS].astype(I32)
    padded = ((counts + tm - 1) // tm) * tm
    ends = jnp.cumsum(padded)
    starts = ends - padded
    experts = jnp.arange(N_EXPERTS, dtype=I32)
    slot = jnp.sum(jnp.where(e_ids[..., None] == experts, starts, 0), axis=-1) + ranks
    plan = (starts // tm, ends // tm, counts, ends[-1:] // tm)
    return slot, tuple(p.astype(I32) for p in plan)


def _final_kernel(x1_ref, y0_ref, y1_ref, route_ref, g_ref, *rest):
    out_ref = rest[-1]
    route = route_ref[...]
    w1 = route[:, 2:3]
    w2 = route[:, 3:4]
    x_lo, x_hi = _unpack_rows(x1_ref[...])
    a_lo, a_hi = _unpack_rows(y0_ref[...])
    b_lo, b_hi = _unpack_rows(y1_ref[...])
    x2 = jnp.concatenate([x_lo + (w1 * a_lo + w2 * b_lo), x_hi + (w1 * a_hi + w2 * b_hi)], axis=1)
    out_ref[...] = _rms(x2, g_ref[...])


def _final(x1, ycat, route, norm_g, tm, chunk, prev):
    t = x1.shape[0]
    nblk = t // tm // FINAL_CHUNKS
    off = chunk * nblk
    in_specs = [
        pl.BlockSpec((tm, PACKED), lambda i: (i + off, 0)),
        pl.BlockSpec((tm, PACKED), lambda i: (i, 0)),
        pl.BlockSpec((tm, PACKED), lambda i: (i + nblk, 0)),
        pl.BlockSpec((tm, LANES), lambda i: (i + off, 0)),
        _const_spec((1, D_MODEL)),
    ]
    args = [x1, ycat, ycat, route, norm_g]
    aliases = {}
    if prev is not None:
        in_specs.append(pl.BlockSpec(memory_space=pl.ANY))
        args.append(prev)
        aliases = {len(args) - 1: 0}
    return pl.pallas_call(
        _final_kernel,
        grid=(nblk,),
        in_specs=in_specs,
        out_specs=pl.BlockSpec((tm, D_MODEL), lambda i: (i + off, 0)),
        out_shape=jax.ShapeDtypeStruct((t, D_MODEL), F32),
        input_output_aliases=aliases,
        compiler_params=_ARB,
        name="final_norm",
    )(*args)


def kernel(x, norm_mix, w_in, conv_w, conv_b, w_conv_out, ssm_a_re, ssm_a_im, ssm_log_dt,
           ssm_b_re, ssm_b_im, ssm_c_re, ssm_c_im, ssm_d, w_glu_a, w_glu_b, gate_bias, w_o,
           norm_ffn, w_route_group, b_route_group, w_route_expert, b_route_expert,
           w_gate, w_up, w_down, norm_final):
    bsz, length, d = x.shape
    assert d == D_MODEL and norm_mix.shape[0] == 1
    t = bsz * length
    tm = TOKEN_TILE
    assert bsz == 1 and t % (FINAL_TILE * FINAL_CHUNKS) == 0
    xt = x.reshape(t, d)
    row = lambda a: a.reshape(1, -1).astype(F32)

    s5_weights = _s5_weights(
        ssm_a_re[0], ssm_a_im[0], ssm_log_dt[0], ssm_b_re[0], ssm_b_im[0],
        ssm_c_re[0], ssm_c_im[0])
    ma, gb, yg, wg_bf, wu_bf, wd_bf = _mixer_front(
        xt, row(norm_mix[0]), w_in[0].astype(BF16), conv_w[0].reshape(3, CONV_WIDTH),
        row(conv_b[0]), w_conv_out[0].astype(BF16), gate_bias[0], s5_weights, row(ssm_d[0]),
        w_gate[0], w_up[0], w_down[0], FRONT_TILE)

    pad = LANES - N_EXPERTS - N_EXPERT_GROUPS
    w_route = jnp.concatenate(
        [w_route_expert[0], w_route_group[0], jnp.zeros((d, pad), F32)], axis=1)
    w_route_hi = w_route.astype(BF16)
    w_route_lo = (w_route - w_route_hi.astype(F32)).astype(BF16)
    b_route = jnp.concatenate(
        [b_route_expert[0], b_route_group[0], jnp.zeros((pad,), F32)]).reshape(1, LANES)
    x1, h2p, route, table, counts = _mixer_back(
        yg, ma, gb, xt, w_glu_a[0].astype(BF16), w_glu_b[0].astype(BF16), w_o[0].astype(BF16),
        row(norm_ffn[0]), jnp.concatenate([w_route_hi, w_route_lo], axis=1), b_route, tm)

    slot, plan = _dispatch_plan(table, counts, MOE_TILE)
    nslot = 2 * t + N_EXPERTS * MOE_TILE
    xs = _sc_dispatch(h2p, slot[0], slot[1], nslot)
    ys = _moe_experts(plan, xs, wg_bf, wu_bf, wd_bf)
    out = None
    tc = t // FINAL_CHUNKS
    for k in range(FINAL_CHUNKS):
        ycat = _sc_gather(ys, slot[:, k * tc:(k + 1) * tc].reshape(-1))
        out = _final(x1, ycat, route, row(norm_final), FINAL_TILE, k, out)
    return out.reshape(bsz, length, d)
```

```python
import functools

import jax
import jax.numpy as jnp
from jax import lax
from jax.experimental import pallas as pl
from jax.experimental.pallas import tpu as pltpu
from jax.experimental.pallas import tpu_sc as plsc

F32 = jnp.float32
BF16 = jnp.bfloat16
I32 = jnp.int32

D_MODEL = 1024
CONV_WIDTH = 1024
SSM_WIDTH = 512
SSM_GROUP = 16
SSM_GROUPS = 32
SSM_STATE = 64
N_EXPERT_GROUPS = 4
EXPERTS_PER_GROUP = 8
N_EXPERTS = 32
EXPERT_FF = 512
EPS = 1e-6

LANES = 128
MXU_DIM = 256
TOKEN_TILE = 512
FINAL_TILE = 1024
FRONT_TILE = 256
MOE_TILE = 512
SCAN_PITCH = 36
STATE_ROWS = 2 * SSM_GROUPS * SSM_STATE // LANES
HALF_ROWS = STATE_ROWS // 2
ROUTE_GROUP_LANE = N_EXPERTS
ROUTE_ROWS = 8
FINAL_CHUNKS = 4
PACKED = D_MODEL // 2
HI_MASK = -65536
SC_CORES = 2
SC_SUBCORES = 16
SC_CHUNK = 128
VMEM_LIMIT = 56 * 1024 * 1024

_ARB = pltpu.CompilerParams(dimension_semantics=("arbitrary",), vmem_limit_bytes=VMEM_LIMIT)


def _const_spec(shape):
    nd = len(shape)
    return pl.BlockSpec(shape, lambda i, *_: (0,) * nd, pipeline_mode=pl.Buffered(1))


def _row_spec(tm, width):
    return pl.BlockSpec((tm, width), lambda i, *_: (i, 0))


def _rms(x, g):
    ms = jnp.mean(x * x, axis=-1, keepdims=True)
    return x * lax.rsqrt(ms + EPS) * g


def _pack_rows(v):
    bits = lax.bitcast_convert_type(v.astype(BF16).astype(F32), I32)
    lo = lax.shift_right_logical(bits[:, :PACKED], 16)
    hi = bits[:, PACKED:] & HI_MASK
    return lo | hi


def _unpack_rows(w):
    lo = lax.bitcast_convert_type(lax.shift_left(w, 16), F32)
    hi = lax.bitcast_convert_type(w & HI_MASK, F32)
    return lo, hi


def _front_kernel(x_ref, g_ref, win_ref, cw_ref, cb_ref, wco32_ref, gbias_ref,
                  bt_ref, ct_ref, are_ref, aim_ref, d_ref, wg_ref, wu_ref, wd_ref,
                  ma_ref, gb_ref, y_ref, wgb_ref, wub_ref, wdb_ref,
                  carry_ref, u_ref, r_ref, state_ref, bre_ref, bim_ref, cw2_ref, wco_ref):
    tm = x_ref.shape[0]
    wgb_ref[...] = wg_ref[...].astype(BF16)
    wub_ref[...] = wu_ref[...].astype(BF16)
    wdb_ref[...] = wd_ref[...].astype(BF16)
    c0, c1, c2 = CONV_WIDTH, 2 * CONV_WIDTH, 3 * CONV_WIDTH
    c3 = c2 + SSM_WIDTH

    @pl.when(pl.program_id(0) == 0)
    def _():
        carry_ref[...] = jnp.zeros_like(carry_ref)
        u_ref[...] = jnp.zeros_like(u_ref)
        state_ref[...] = jnp.zeros_like(state_ref)
        _s5_expand_weights(bt_ref, ct_ref, bre_ref, bim_ref, cw2_ref)
        wco_ref[...] = wco32_ref[...].astype(BF16)

    h = _rms(x_ref[...], g_ref[...]).astype(BF16)
    u_prev = u_ref[...]
    _s5_scan_tile(u_prev, bre_ref, bim_ref, are_ref, aim_ref, r_ref, state_ref)

    def proj(lo, hi):
        return jnp.dot(h, win_ref[:, lo:hi], preferred_element_type=F32)

    v = proj(c1, c2) * proj(0, c0)
    row = lax.broadcasted_iota(jnp.int32, v.shape, 0)
    prev1 = carry_ref[7:8, :]
    prev2 = carry_ref[6:7, :]
    v1 = jnp.where(row == 0, prev1, pltpu.roll(v, 1, 0))
    v2 = jnp.where(row == 0, prev2, jnp.where(row == 1, prev1, pltpu.roll(v, 2, 0)))
    carry_ref[...] = v[tm - 8:, :]
    y = cw_ref[0:1, :] * v2 + cw_ref[1:2, :] * v1 + cw_ref[2:3, :] * v + cb_ref[...]
    z = (proj(c0, c1) * y).astype(BF16)
    ya = jnp.dot(z, wco_ref[...], preferred_element_type=F32)
    ga = jax.nn.sigmoid(proj(c3, c3 + D_MODEL) + gbias_ref[0:1, :])
    ma_ref[...] = (ga * ya).astype(BF16)
    gb = jax.nn.sigmoid(proj(c3 + D_MODEL, c3 + 2 * D_MODEL) + gbias_ref[1:2, :])
    gb_ref[...] = gb.astype(BF16)
    u_new = proj(c2, c3).astype(BF16)
    _s5_readout(u_prev, cw2_ref, d_ref, y_ref, r_ref)
    u_ref[...] = u_new


def _mixer_front(x, norm_g, w_in, conv_w, conv_b, w_conv_out, gate_bias, s5_weights, d_skip,
                 w_gate, w_up, w_down, tm):
    bt, ct, a_re, a_im = s5_weights
    halves = SSM_GROUPS * SSM_GROUP // MXU_DIM
    t = x.shape[0]
    in_cols = w_in.shape[1]
    n = t // tm
    last = lambda i: (jnp.minimum(i, n - 1), 0)
    lag = lambda i: (jnp.maximum(i - 1, 0), 0)
    split = max(k for k in (1, 2, 4, 8) if N_EXPERTS * k <= n + 1)
    piece = lambda i: (jnp.minimum(i, N_EXPERTS * split - 1) // split,
                       jnp.minimum(i, N_EXPERTS * split - 1) % split, 0)
    w_specs = [pl.BlockSpec((1, D_MODEL // split, EXPERT_FF), piece),
               pl.BlockSpec((1, D_MODEL // split, EXPERT_FF), piece),
               pl.BlockSpec((1, EXPERT_FF // split, D_MODEL), piece)]
    return pl.pallas_call(
        _front_kernel,
        grid=(n + 1,),
        in_specs=[
            pl.BlockSpec((tm, D_MODEL), last),
            _const_spec((1, D_MODEL)),
            _const_spec((D_MODEL, in_cols)),
            _const_spec((3, CONV_WIDTH)),
            _const_spec((1, CONV_WIDTH)),
            _const_spec((CONV_WIDTH, D_MODEL)),
            _const_spec((2, D_MODEL)),
            _const_spec(bt.shape),
            _const_spec(ct.shape),
            _const_spec(a_re.shape),
            _const_spec(a_im.shape),
            _const_spec((1, SSM_WIDTH)),
        ] + w_specs,
        out_specs=[_row_spec(tm, D_MODEL), _row_spec(tm, D_MODEL),
                   pl.BlockSpec((tm, SSM_WIDTH), lag)] + w_specs,
        out_shape=[
            jax.ShapeDtypeStruct((t + tm, D_MODEL), BF16),
            jax.ShapeDtypeStruct((t + tm, D_MODEL), BF16),
            jax.ShapeDtypeStruct((t, SSM_WIDTH), BF16),
            jax.ShapeDtypeStruct(w_gate.shape, BF16),
            jax.ShapeDtypeStruct(w_up.shape, BF16),
            jax.ShapeDtypeStruct(w_down.shape, BF16),
        ],
        scratch_shapes=[
            pltpu.VMEM((8, CONV_WIDTH), F32),
            pltpu.VMEM((tm, SSM_WIDTH), BF16),
            pltpu.VMEM((tm * SCAN_PITCH, LANES), F32),
            pltpu.VMEM((STATE_ROWS, LANES), F32),
            pltpu.VMEM((halves, MXU_DIM, SSM_GROUPS * SSM_STATE // halves), BF16),
            pltpu.VMEM((halves, MXU_DIM, SSM_GROUPS * SSM_STATE // halves), BF16),
            pltpu.VMEM((halves, 2 * SSM_GROUPS * SSM_STATE // halves, MXU_DIM), BF16),
            pltpu.VMEM((CONV_WIDTH, D_MODEL), BF16),
        ],
        compiler_params=_ARB,
        name="mixer_front",
    )(x, norm_g, w_in, conv_w, conv_b, w_conv_out, gate_bias, bt, ct, a_re, a_im, d_skip,
      w_gate, w_up, w_down)


def _s5_scan_tile(u, bre_ref, bim_ref, are_ref, aim_ref, r_ref, state_ref):
    tm = u.shape[0]
    tiles_per_half = MXU_DIM * 4 // LANES

    for k in range(2):
        uk = u[:, k * MXU_DIM:(k + 1) * MXU_DIM]
        re = jnp.dot(uk, bre_ref[k], preferred_element_type=F32)
        im = jnp.dot(uk, bim_ref[k], preferred_element_type=F32)
        for jj in range(tiles_per_half):
            j = k * tiles_per_half + jj
            sl = slice(jj * LANES, (jj + 1) * LANES)
            r_ref[pl.ds(j, tm, stride=SCAN_PITCH), :] = re[:, sl]
            r_ref[pl.ds(HALF_ROWS + j, tm, stride=SCAN_PITCH), :] = im[:, sl]

    a_re = are_ref[...]
    a_im = aim_ref[...]

    s_re = state_ref[0:HALF_ROWS, :]
    s_im = state_ref[HALF_ROWS:, :]
    for t in range(tm):
        base = t * SCAN_PITCH
        b_re = r_ref[pl.ds(base, HALF_ROWS), :]
        b_im = r_ref[pl.ds(base + HALF_ROWS, HALF_ROWS), :]
        s_re, s_im = (a_re * s_re - a_im * s_im + b_re,
                      a_re * s_im + a_im * s_re + b_im)
        r_ref[pl.ds(base, HALF_ROWS), :] = s_re
        r_ref[pl.ds(base + HALF_ROWS, HALF_ROWS), :] = s_im
    state_ref[0:HALF_ROWS, :] = s_re
    state_ref[HALF_ROWS:, :] = s_im


def _s5_readout(u, cw_ref, d_ref, y_ref, r_ref):
    tm = u.shape[0]
    tiles_per_half = MXU_DIM * 4 // LANES
    ys = []
    for k in range(2):
        cols = []
        for half in range(2):
            for jj in range(tiles_per_half):
                j = half * HALF_ROWS + k * tiles_per_half + jj
                cols.append(r_ref[pl.ds(j, tm, stride=SCAN_PITCH), :])
        s = jnp.concatenate(cols, axis=1).astype(BF16)
        ys.append(jnp.dot(s, cw_ref[k], preferred_element_type=F32))
    y = jnp.concatenate(ys, axis=1) + d_ref[...] * u.astype(F32)
    y_ref[...] = jax.nn.gelu(y).astype(BF16)


def _s5_expand_weights(bt_ref, ct_ref, bre_ref, bim_ref, cw_ref):
    n, h = SSM_STATE, SSM_GROUP
    kb, nb = bre_ref.shape[1], bre_ref.shape[2]
    kc, nc = cw_ref.shape[1] // 2, cw_ref.shape[2]
    iota = lambda shape, d: lax.broadcasted_iota(jnp.int32, shape, d)
    tile_b = (iota((n, nb), 1) % n == iota((n, nb), 0)).astype(BF16)
    keep_b = iota((kb, nb), 0) // h == iota((kb, nb), 1) // n
    tile_c = (iota((h, nc), 1) % h == iota((h, nc), 0)).astype(BF16)
    keep_c = iota((kc, nc), 0) // n == iota((kc, nc), 1) // h
    for k in range(bre_ref.shape[0]):
        for part, dst in enumerate((bre_ref, bim_ref)):
            spread = jnp.dot(bt_ref[part, k].astype(BF16), tile_b, preferred_element_type=F32)
            dst[k] = jnp.where(keep_b, spread, 0.0).astype(BF16)
        for part in range(2):
            spread = jnp.dot(ct_ref[part, k].astype(BF16), tile_c, preferred_element_type=F32)
            cw_ref[k, part * kc:(part + 1) * kc, :] = jnp.where(keep_c, spread, 0.0).astype(BF16)


def _s5_weights(a_re, a_im, log_dt, b_re, b_im, c_re, c_im):
    g, n, h = SSM_GROUPS, SSM_STATE, SSM_GROUP
    halves = g * h // MXU_DIM
    lam = lax.complex(a_re.astype(F32), a_im.astype(F32))
    dt = jnp.exp(log_dt.astype(F32))[:, None]
    a_bar = jnp.exp(lam * dt)
    b_bar = ((a_bar - 1.0) / lam)[..., None] * lax.complex(b_re.astype(F32), b_im.astype(F32))
    bt = jnp.stack([jnp.real(b_bar), jnp.imag(b_bar)])
    bt = bt.transpose(0, 1, 3, 2).reshape(2, halves, g * h // halves, n)
    ct = jnp.stack([c_re.astype(F32), -c_im.astype(F32)])
    ct = ct.transpose(0, 1, 3, 2).reshape(2, halves, g * n // halves, h)
    tile = (HALF_ROWS, LANES)
    return bt, ct, jnp.real(a_bar).reshape(tile), jnp.imag(a_bar).reshape(tile)


def _back_kernel(yg_ref, ma_ref, gb_ref, x_ref, wa32_ref, wb32_ref, wo32_ref, g2_ref, wr32_ref,
                 br_ref, x1_ref, h2p_ref, route_ref, table_ref, counts_ref,
                 h2_ref, wa_ref, wb_ref, wo_ref, wr_ref):
    tm = x_ref.shape[0]
    step = pl.program_id(0)

    @pl.when(step == 0)
    def _():
        counts_ref[...] = jnp.zeros_like(counts_ref)
        h2_ref[...] = jnp.zeros_like(h2_ref)
        wa_ref[...] = wa32_ref[...].astype(BF16)
        wb_ref[...] = wb32_ref[...].astype(BF16)
        wo_ref[...] = wo32_ref[...].astype(BF16)
        wr32 = wr32_ref[...]
        wr_hi = wr32.astype(BF16)
        wr_ref[:, :LANES] = wr_hi
        wr_ref[:, LANES:] = (wr32 - wr_hi.astype(F32)).astype(BF16)

    h2_prev = h2_ref[...]
    h2_hi = h2_prev.astype(BF16)
    h2_lo = (h2_prev - h2_hi.astype(F32)).astype(BF16)
    both = jnp.dot(h2_hi, wr_ref[...], preferred_element_type=F32)
    logits = (both[:, :LANES]
              + (both[:, LANES:]
                 + jnp.dot(h2_lo, wr_ref[:, :LANES], preferred_element_type=F32))) + br_ref[...]
    lane = lax.broadcasted_iota(jnp.int32, logits.shape, 1)
    lane_f = lane.astype(F32)
    neg = jnp.float32(-jnp.inf)
    big = jnp.float32(LANES)
    is_grp = (lane >= ROUTE_GROUP_LANE) & (lane < ROUTE_GROUP_LANE + N_EXPERT_GROUPS)
    gl = jnp.where(is_grp, logits, neg)
    gmax = jnp.max(gl, axis=1, keepdims=True)
    gidx = jnp.min(jnp.where(gl == gmax, lane_f - ROUTE_GROUP_LANE, big), axis=1, keepdims=True)
    pg_top = 1.0 / jnp.sum(jnp.exp(gl - gmax), axis=1, keepdims=True)
    lane_grp = (lane // EXPERTS_PER_GROUP).astype(F32)
    el = jnp.where((lane < N_EXPERTS) & (lane_grp == gidx), logits, neg)
    l1 = jnp.max(el, axis=1, keepdims=True)
    i1 = jnp.min(jnp.where(el == l1, lane_f, big), axis=1, keepdims=True)
    el2 = jnp.where(lane_f == i1, neg, el)
    l2 = jnp.max(el2, axis=1, keepdims=True)
    i2 = jnp.min(jnp.where(el2 == l2, lane_f, big), axis=1, keepdims=True)
    r = jnp.exp(l2 - l1)
    w1 = pg_top / (1.0 + r)
    w2 = pg_top * r / (1.0 + r)
    oh1 = (lane_f == i1).astype(F32)
    oh2 = (lane_f == i2).astype(F32)
    picked = oh1 + oh2

    yg = yg_ref[...]
    yb = (jnp.dot(yg, wa_ref[...], preferred_element_type=F32)
          * jax.nn.sigmoid(jnp.dot(yg, wb_ref[...], preferred_element_type=F32)))
    mixed = (ma_ref[...].astype(F32) + gb_ref[...].astype(F32) * yb).astype(BF16)
    x1 = x_ref[...] + jnp.dot(mixed, wo_ref[...], preferred_element_type=F32)
    x1_ref[...] = _pack_rows(x1)
    h2 = _rms(x1, g2_ref[...])
    h2p_ref[...] = _pack_rows(h2)
    h2_ref[...] = h2

    rr = lax.broadcasted_iota(jnp.int32, (tm, tm), 0)
    cc = lax.broadcasted_iota(jnp.int32, (tm, tm), 1)
    before = (cc < rr).astype(BF16)
    prior = jnp.dot(before, picked.astype(BF16), preferred_element_type=F32) + counts_ref[...]
    rank1 = jnp.sum(oh1 * prior, axis=1, keepdims=True)
    rank2 = jnp.sum(oh2 * prior, axis=1, keepdims=True)
    counts_ref[...] += jnp.where(step > 0, jnp.sum(picked, axis=0, keepdims=True), 0.0)

    route = jnp.where(
        lane == 0, i1, jnp.where(lane == 1, i2, jnp.where(lane == 2, w1, jnp.where(
            lane == 3, w2, jnp.where(lane == 4, rank1, jnp.where(lane == 5, rank2, 0.0))))))
    route_ref[...] = route
    table_ref[...] = route.T[0:ROUTE_ROWS, :]


def _mixer_back(yg, ma, gb, x, w_glu_a, w_glu_b, w_o, norm_g, w_route, b_route, tm):
    t = x.shape[0]
    n = t // tm
    cur = lambda i: (jnp.minimum(i, n - 1), 0)
    lag = lambda i: (jnp.maximum(i - 1, 0), 0)
    return pl.pallas_call(
        _back_kernel,
        grid=(n + 1,),
        in_specs=[
            pl.BlockSpec((tm, SSM_WIDTH), cur),
            pl.BlockSpec((tm, D_MODEL), cur),
            pl.BlockSpec((tm, D_MODEL), cur),
            pl.BlockSpec((tm, D_MODEL), cur),
            _const_spec((SSM_WIDTH, D_MODEL)),
            _const_spec((SSM_WIDTH, D_MODEL)),
            _const_spec((D_MODEL, D_MODEL)),
            _const_spec((1, D_MODEL)),
            _const_spec((D_MODEL, LANES)),
            _const_spec((1, LANES)),
        ],
        out_specs=[pl.BlockSpec((tm, PACKED), cur), pl.BlockSpec((tm, PACKED), cur),
                   pl.BlockSpec((tm, LANES), lag),
                   pl.BlockSpec((ROUTE_ROWS, tm), lambda i: (0, jnp.maximum(i - 1, 0))),
                   pl.BlockSpec((1, LANES), lambda i: (0, 0))],
        out_shape=[
            jax.ShapeDtypeStruct((t, PACKED), I32),
            jax.ShapeDtypeStruct((t, PACKED), I32),
            jax.ShapeDtypeStruct((t, LANES), F32),
            jax.ShapeDtypeStruct((ROUTE_ROWS, t), F32),
            jax.ShapeDtypeStruct((1, LANES), F32),
        ],
        scratch_shapes=[
            pltpu.VMEM((tm, D_MODEL), F32),
            pltpu.VMEM((SSM_WIDTH, D_MODEL), BF16),
            pltpu.VMEM((SSM_WIDTH, D_MODEL), BF16),
            pltpu.VMEM((D_MODEL, D_MODEL), BF16),
            pltpu.VMEM((D_MODEL, 2 * LANES), BF16),
        ],
        compiler_params=_ARB,
        name="mixer_back",
    )(yg, ma, gb, x, w_glu_a, w_glu_b, w_o, norm_g, w_route, b_route)


_SC_MESH = dict(core_axis_name="c", subcore_axis_name="s")


def _sc_worker():
    return lax.axis_index("s") * SC_CORES + lax.axis_index("c")


def _sc_dispatch(rows, slot0, slot1, nslot):
    t, width = rows.shape
    workers = SC_CORES * SC_SUBCORES
    per_w = t // workers
    assert per_w % SC_CHUNK == 0
    nchunk = per_w // SC_CHUNK

    @functools.partial(
        pl.kernel,
        out_type=jax.ShapeDtypeStruct((nslot, width), rows.dtype),
        mesh=plsc.VectorSubcoreMesh(**_SC_MESH),
        scratch_types=[
            pltpu.VMEM((nchunk, SC_CHUNK), I32),
            pltpu.VMEM((nchunk, SC_CHUNK), I32),
            pltpu.VMEM((SC_CHUNK, width), rows.dtype),
            pltpu.SemaphoreType.DMA,
            pltpu.SemaphoreType.DMA,
        ],
        name="moe_dispatch",
    )
    def k(rows_hbm, s0_hbm, s1_hbm, out_hbm, i0_v, i1_v, rows_v, sem0, sem1):
        w = _sc_worker()
        pltpu.sync_copy(s0_hbm.at[w], i0_v)
        pltpu.sync_copy(s1_hbm.at[w], i1_v)

        @pl.loop(0, nchunk)
        def _(c):
            pltpu.sync_copy(rows_hbm.at[pl.ds(w * per_w + c * SC_CHUNK, SC_CHUNK)], rows_v)
            first = pltpu.async_copy(rows_v, out_hbm.at[i0_v.at[c]], sem0)
            second = pltpu.async_copy(rows_v, out_hbm.at[i1_v.at[c]], sem1)
            first.wait()
            second.wait()

    shape = (workers, nchunk, SC_CHUNK)
    return k(rows, slot0.reshape(shape), slot1.reshape(shape))


def _sc_gather(table, idx):
    n = idx.shape[0]
    width = table.shape[1]
    per_w = n // (SC_CORES * SC_SUBCORES)
    assert per_w % SC_CHUNK == 0

    @functools.partial(
        pl.kernel,
        out_type=jax.ShapeDtypeStruct((n, width), table.dtype),
        mesh=plsc.VectorSubcoreMesh(**_SC_MESH),
        scratch_types=[
            pltpu.VMEM((1, SC_CHUNK), I32),
            pltpu.VMEM((SC_CHUNK, width), table.dtype),
        ],
        name="moe_combine_gather",
    )
    def k(table_hbm, idx_hbm, out_hbm, idx_v, rows_v):
        base = _sc_worker() * per_w

        @pl.loop(0, per_w // SC_CHUNK)
        def _(c):
            off = base + c * SC_CHUNK
            pltpu.sync_copy(idx_hbm.at[:, pl.ds(off, SC_CHUNK)], idx_v)
            pltpu.sync_copy(table_hbm.at[idx_v.at[0]], rows_v)
            pltpu.sync_copy(rows_v, out_hbm.at[pl.ds(off, SC_CHUNK)])

    return k(table, idx.reshape(1, n))


def _moe_kernel(first_ref, last_ref, count_ref, total_ref,
                xs_hbm, wg_ref, wu_ref, wd_ref, out_hbm,
                xbuf0, xbuf1, obuf0, obuf1, xsem, osem):
    e = pl.program_id(0)
    ch = MOE_TILE
    half = ch // 2
    first, last, count, total = first_ref[e], last_ref[e], count_ref[e], total_ref[0]
    xbufs = (xbuf0, xbuf1)
    obufs = (obuf0, obuf1)

    def x_copy(c, s):
        rows = pl.ds(pl.multiple_of(c * ch, ch), ch)
        return pltpu.make_async_copy(xs_hbm.at[rows], xbufs[s], xsem.at[s])

    def o_copy(c, s):
        rows = pl.ds(pl.multiple_of(c * ch, ch), ch)
        return pltpu.make_async_copy(obufs[s], out_hbm.at[rows], osem.at[s])

    @pl.when(e == 0)
    def _():
        x_copy(0, 0).start()

    def experts_rows(s, rows, valid):
        row = lax.broadcasted_iota(jnp.int32, (rows, PACKED), 0)
        words = jnp.where(row < valid, xbufs[s][0:rows, :], 0)
        lo, hi = _unpack_rows(words)
        xs = jnp.concatenate([lo.astype(BF16), hi.astype(BF16)], axis=1)
        gate = jnp.dot(xs, wg_ref[0], preferred_element_type=F32)
        up = jnp.dot(xs, wu_ref[0], preferred_element_type=F32)
        act = (gate * jax.nn.sigmoid(gate) * up).astype(BF16)
        obufs[s][0:rows, :] = _pack_rows(jnp.dot(act, wd_ref[0], preferred_element_type=F32))

    def chunk_in_slot(c, s):
        x_copy(c, s).wait()

        @pl.when(c + 1 < total)
        def _():
            x_copy(c + 1, 1 - s).start()

        @pl.when(c >= 2)
        def _():
            o_copy(c - 2, s).wait()

        valid = count - (c - first) * ch

        sizes = (ch // 4, half, ch)
        for lo, rows in zip((0,) + sizes[:-1], sizes):
            @pl.when((valid > lo) & ((valid <= rows) | (rows == ch)))
            def _():
                experts_rows(s, rows, valid)
                if rows < ch:
                    obufs[s][rows:, :] = jnp.zeros((ch - rows, PACKED), I32)

        o_copy(c, s).start()

    def chunk(c, carry):
        for s in range(2):
            @pl.when(c % 2 == s)
            def _():
                chunk_in_slot(c, s)
        return carry

    lax.fori_loop(first, last, chunk, 0)

    @pl.when(e == pl.num_programs(0) - 1)
    def _():
        for s in range(2):
            @pl.when(((total - 1) % 2 == s) | (total >= 2))
            def _():
                o_copy(0, s).wait()


def _moe_experts(plan, xs, w_gate, w_up, w_down):
    nslot = xs.shape[0]
    any_spec = pl.BlockSpec(memory_space=pl.ANY)
    grid_spec = pltpu.PrefetchScalarGridSpec(
        num_scalar_prefetch=len(plan),
        grid=(N_EXPERTS,),
        in_specs=[
            any_spec,
            pl.BlockSpec((1, D_MODEL, EXPERT_FF), lambda e, *_: (e, 0, 0)),
            pl.BlockSpec((1, D_MODEL, EXPERT_FF), lambda e, *_: (e, 0, 0)),
            pl.BlockSpec((1, EXPERT_FF, D_MODEL), lambda e, *_: (e, 0, 0)),
        ],
        out_specs=any_spec,
        scratch_shapes=[
            pltpu.VMEM((MOE_TILE, PACKED), I32),
            pltpu.VMEM((MOE_TILE, PACKED), I32),
            pltpu.VMEM((MOE_TILE, PACKED), I32),
            pltpu.VMEM((MOE_TILE, PACKED), I32),
            pltpu.SemaphoreType.DMA((2,)),
            pltpu.SemaphoreType.DMA((2,)),
        ],
    )
    return pl.pallas_call(
        _moe_kernel,
        grid_spec=grid_spec,
        out_shape=jax.ShapeDtypeStruct((nslot, PACKED), I32),
        compiler_params=_ARB,
        name="moe_experts",
    )(*plan, xs, w_gate, w_up, w_down)


def _dispatch_plan(table, counts, tm):
    e_ids = table[0:2].astype(I32)
    ranks = table[4:6].astype(I32)
    counts = counts[0, :N_EXPERTS].astype(I32)
    padded = ((counts + tm - 1) // tm) * tm
    ends = jnp.cumsum(padded)
    starts = ends - padded
    experts = jnp.arange(N_EXPERTS, dtype=I32)
    slot = jnp.sum(jnp.where(e_ids[..., None] == experts, starts, 0), axis=-1) + ranks
    plan = (starts // tm, ends // tm, counts, ends[-1:] // tm)
    return slot, tuple(p.astype(I32) for p in plan)


def _final_kernel(x1_ref, y0_ref, y1_ref, route_ref, g_ref, *rest):
    out_ref = rest[-1]
    route = route_ref[...]
    w1 = route[:, 2:3]
    w2 = route[:, 3:4]
    x_lo, x_hi = _unpack_rows(x1_ref[...])
    a_lo, a_hi = _unpack_rows(y0_ref[...])
    b_lo, b_hi = _unpack_rows(y1_ref[...])
    x2 = jnp.concatenate([x_lo + (w1 * a_lo + w2 * b_lo), x_hi + (w1 * a_hi + w2 * b_hi)], axis=1)
    out_ref[...] = _rms(x2, g_ref[...])


def _final(x1, ycat, route, norm_g, tm, chunk, prev):
    t = x1.shape[0]
    nblk = t // tm // FINAL_CHUNKS
    off = chunk * nblk
    in_specs = [
        pl.BlockSpec((tm, PACKED), lambda i: (i + off, 0)),
        pl.BlockSpec((tm, PACKED), lambda i: (i, 0)),
        pl.BlockSpec((tm, PACKED), lambda i: (i + nblk, 0)),
        pl.BlockSpec((tm, LANES), lambda i: (i + off, 0)),
        _const_spec((1, D_MODEL)),
    ]
    args = [x1, ycat, ycat, route, norm_g]
    aliases = {}
    if prev is not None:
        in_specs.append(pl.BlockSpec(memory_space=pl.ANY))
        args.append(prev)
        aliases = {len(args) - 1: 0}
    return pl.pallas_call(
        _final_kernel,
        grid=(nblk,),
        in_specs=in_specs,
        out_specs=pl.BlockSpec((tm, D_MODEL), lambda i: (i + off, 0)),
        out_shape=jax.ShapeDtypeStruct((t, D_MODEL), F32),
        input_output_aliases=aliases,
        compiler_params=_ARB,
        name="final_norm",
    )(*args)


def kernel(x, norm_mix, w_in, conv_w, conv_b, w_conv_out, ssm_a_re, ssm_a_im, ssm_log_dt,
           ssm_b_re, ssm_b_im, ssm_c_re, ssm_c_im, ssm_d, w_glu_a, w_glu_b, gate_bias, w_o,
           norm_ffn, w_route_group, b_route_group, w_route_expert, b_route_expert,
           w_gate, w_up, w_down, norm_final):
    bsz, length, d = x.shape
    assert d == D_MODEL and norm_mix.shape[0] == 1
    t = bsz * length
    tm = TOKEN_TILE
    assert bsz == 1 and t % (FINAL_TILE * FINAL_CHUNKS) == 0
    xt = x.reshape(t, d)
    row = lambda a: a.reshape(1, -1).astype(F32)

    s5_weights = _s5_weights(
        ssm_a_re[0], ssm_a_im[0], ssm_log_dt[0], ssm_b_re[0], ssm_b_im[0],
        ssm_c_re[0], ssm_c_im[0])
    ma, gb, yg, wg_bf, wu_bf, wd_bf = _mixer_front(
        xt, row(norm_mix[0]), w_in[0].astype(BF16), conv_w[0].reshape(3, CONV_WIDTH),
        row(conv_b[0]), w_conv_out[0], gate_bias[0], s5_weights, row(ssm_d[0]),
        w_gate[0], w_up[0], w_down[0], FRONT_TILE)

    pad = LANES - N_EXPERTS - N_EXPERT_GROUPS
    w_route = jnp.concatenate(
        [w_route_expert[0], w_route_group[0], jnp.zeros((d, pad), F32)], axis=1)
    b_route = jnp.concatenate(
        [b_route_expert[0], b_route_group[0], jnp.zeros((pad,), F32)]).reshape(1, LANES)
    x1, h2p, route, table, counts = _mixer_back(
        yg, ma, gb, xt, w_glu_a[0], w_glu_b[0], w_o[0], row(norm_ffn[0]), w_route, b_route, tm)

    slot, plan = _dispatch_plan(table, counts, MOE_TILE)
    nslot = 2 * t + N_EXPERTS * MOE_TILE
    xs = _sc_dispatch(h2p, slot[0], slot[1], nslot)
    ys = _moe_experts(plan, xs, wg_bf, wu_bf, wd_bf)
    out = None
    tc = t // FINAL_CHUNKS
    for k in range(FINAL_CHUNKS):
        ycat = _sc_gather(ys, slot[:, k * tc:(k + 1) * tc].reshape(-1))
        out = _final(x1, ycat, route, row(norm_final), FINAL_TILE, k, out)
    return out.reshape(bsz, length, d)
```

```python
import functools

import jax
import jax.numpy as jnp
from jax import lax
from jax.experimental import pallas as pl
from jax.experimental.pallas import tpu as pltpu
from jax.experimental.pallas import tpu_sc as plsc

F32 = jnp.float32
BF16 = jnp.bfloat16
I32 = jnp.int32

D_MODEL = 1024
CONV_WIDTH = 1024
SSM_WIDTH = 512
SSM_GROUP = 16
SSM_GROUPS = 32
SSM_STATE = 64
N_EXPERT_GROUPS = 4
EXPERTS_PER_GROUP = 8
N_EXPERTS = 32
EXPERT_FF = 512
EPS = 1e-6

LANES = 128
MXU_DIM = 256
TOKEN_TILE = 512
FINAL_TILE = 1024
W_IN_SLAB = 128
FRONT_TILE = 256
MOE_TILE = 512
SCAN_PITCH = 36
STATE_ROWS = 2 * SSM_GROUPS * SSM_STATE // LANES
HALF_ROWS = STATE_ROWS // 2
ROUTE_GROUP_LANE = N_EXPERTS
ROUTE_ROWS = 8
FINAL_CHUNKS = 4
PACKED = D_MODEL // 2
HI_MASK = -65536
SC_CORES = 2
SC_SUBCORES = 16
SC_CHUNK = 128
VMEM_LIMIT = 56 * 1024 * 1024

_ARB = pltpu.CompilerParams(dimension_semantics=("arbitrary",), vmem_limit_bytes=VMEM_LIMIT)


def _const_spec(shape):
    nd = len(shape)
    return pl.BlockSpec(shape, lambda i, *_: (0,) * nd, pipeline_mode=pl.Buffered(1))


def _row_spec(tm, width):
    return pl.BlockSpec((tm, width), lambda i, *_: (i, 0))


def _rms(x, g):
    ms = jnp.mean(x * x, axis=-1, keepdims=True)
    return x * lax.rsqrt(ms + EPS) * g


def _pack_rows(v):
    bits = lax.bitcast_convert_type(v.astype(BF16).astype(F32), I32)
    lo = lax.shift_right_logical(bits[:, :PACKED], 16)
    hi = bits[:, PACKED:] & HI_MASK
    return lo | hi


def _unpack_rows(w):
    lo = lax.bitcast_convert_type(lax.shift_left(w, 16), F32)
    hi = lax.bitcast_convert_type(w & HI_MASK, F32)
    return lo, hi


def _front_kernel(x_ref, g_ref, win_hbm, cw_ref, cb_ref, wco32_ref, gbias_ref,
                  bt_ref, ct_ref, are_ref, aim_ref, d_ref, wg_ref, wu_ref, wd_ref,
                  ma_ref, gb_ref, y_ref, wgb_ref, wub_ref, wdb_ref,
                  carry_ref, u_ref, r_ref, state_ref, bre_ref, bim_ref, cw2_ref, wco_ref,
                  win_ref, stage_ref, stage_sem):
    tm = x_ref.shape[0]
    wgb_ref[...] = wg_ref[...].astype(BF16)
    wub_ref[...] = wu_ref[...].astype(BF16)
    wdb_ref[...] = wd_ref[...].astype(BF16)
    c0, c1, c2 = CONV_WIDTH, 2 * CONV_WIDTH, 3 * CONV_WIDTH
    c3 = c2 + SSM_WIDTH

    @pl.when(pl.program_id(0) == 0)
    def _():
        carry_ref[...] = jnp.zeros_like(carry_ref)
        u_ref[...] = jnp.zeros_like(u_ref)
        state_ref[...] = jnp.zeros_like(state_ref)
        _s5_expand_weights(bt_ref, ct_ref, bre_ref, bim_ref, cw2_ref)
        wco_ref[...] = wco32_ref[...].astype(BF16)
        slab = stage_ref.shape[1]
        copies = [pltpu.make_async_copy(win_hbm.at[pl.ds(j * slab, slab)], stage_ref.at[j % 2],
                                        stage_sem.at[j % 2]) for j in range(win_ref.shape[0] // slab)]
        copies[0].start()
        for j, copy in enumerate(copies):
            if j + 1 < len(copies):
                copies[j + 1].start()
            copy.wait()
            win_ref[j * slab:(j + 1) * slab, :] = stage_ref[j % 2].astype(BF16)

    h = _rms(x_ref[...], g_ref[...]).astype(BF16)
    u_prev = u_ref[...]
    _s5_scan_tile(u_prev, bre_ref, bim_ref, are_ref, aim_ref, r_ref, state_ref)

    def proj(lo, hi):
        return jnp.dot(h, win_ref[:, lo:hi], preferred_element_type=F32)

    v = proj(c1, c2) * proj(0, c0)
    row = lax.broadcasted_iota(jnp.int32, v.shape, 0)
    prev1 = carry_ref[7:8, :]
    prev2 = carry_ref[6:7, :]
    v1 = jnp.where(row == 0, prev1, pltpu.roll(v, 1, 0))
    v2 = jnp.where(row == 0, prev2, jnp.where(row == 1, prev1, pltpu.roll(v, 2, 0)))
    carry_ref[...] = v[tm - 8:, :]
    y = cw_ref[0:1, :] * v2 + cw_ref[1:2, :] * v1 + cw_ref[2:3, :] * v + cb_ref[...]
    z = (proj(c0, c1) * y).astype(BF16)
    ya = jnp.dot(z, wco_ref[...], preferred_element_type=F32)
    ga = jax.nn.sigmoid(proj(c3, c3 + D_MODEL) + gbias_ref[0:1, :])
    ma_ref[...] = (ga * ya).astype(BF16)
    gb = jax.nn.sigmoid(proj(c3 + D_MODEL, c3 + 2 * D_MODEL) + gbias_ref[1:2, :])
    gb_ref[...] = gb.astype(BF16)
    u_new = proj(c2, c3).astype(BF16)
    _s5_readout(u_prev, cw2_ref, d_ref, y_ref, r_ref)
    u_ref[...] = u_new


def _mixer_front(x, norm_g, w_in, conv_w, conv_b, w_conv_out, gate_bias, s5_weights, d_skip,
                 w_gate, w_up, w_down, tm):
    bt, ct, a_re, a_im = s5_weights
    halves = SSM_GROUPS * SSM_GROUP // MXU_DIM
    t = x.shape[0]
    in_cols = w_in.shape[1]
    n = t // tm
    last = lambda i: (jnp.minimum(i, n - 1), 0)
    lag = lambda i: (jnp.maximum(i - 1, 0), 0)
    split = max(k for k in (1, 2, 4, 8) if N_EXPERTS * k <= n + 1)
    piece = lambda i: (jnp.minimum(i, N_EXPERTS * split - 1) // split,
                       jnp.minimum(i, N_EXPERTS * split - 1) % split, 0)
    w_specs = [pl.BlockSpec((1, D_MODEL // split, EXPERT_FF), piece),
               pl.BlockSpec((1, D_MODEL // split, EXPERT_FF), piece),
               pl.BlockSpec((1, EXPERT_FF // split, D_MODEL), piece)]
    return pl.pallas_call(
        _front_kernel,
        grid=(n + 1,),
        in_specs=[
            pl.BlockSpec((tm, D_MODEL), last),
            _const_spec((1, D_MODEL)),
            pl.BlockSpec(memory_space=pl.ANY),
            _const_spec((3, CONV_WIDTH)),
            _const_spec((1, CONV_WIDTH)),
            _const_spec((CONV_WIDTH, D_MODEL)),
            _const_spec((2, D_MODEL)),
            _const_spec(bt.shape),
            _const_spec(ct.shape),
            _const_spec(a_re.shape),
            _const_spec(a_im.shape),
            _const_spec((1, SSM_WIDTH)),
        ] + w_specs,
        out_specs=[_row_spec(tm, D_MODEL), _row_spec(tm, D_MODEL),
                   pl.BlockSpec((tm, SSM_WIDTH), lag)] + w_specs,
        out_shape=[
            jax.ShapeDtypeStruct((t + tm, D_MODEL), BF16),
            jax.ShapeDtypeStruct((t + tm, D_MODEL), BF16),
            jax.ShapeDtypeStruct((t, SSM_WIDTH), BF16),
            jax.ShapeDtypeStruct(w_gate.shape, BF16),
            jax.ShapeDtypeStruct(w_up.shape, BF16),
            jax.ShapeDtypeStruct(w_down.shape, BF16),
        ],
        scratch_shapes=[
            pltpu.VMEM((8, CONV_WIDTH), F32),
            pltpu.VMEM((tm, SSM_WIDTH), BF16),
            pltpu.VMEM((tm * SCAN_PITCH, LANES), F32),
            pltpu.VMEM((STATE_ROWS, LANES), F32),
            pltpu.VMEM((halves, MXU_DIM, SSM_GROUPS * SSM_STATE // halves), BF16),
            pltpu.VMEM((halves, MXU_DIM, SSM_GROUPS * SSM_STATE // halves), BF16),
            pltpu.VMEM((halves, 2 * SSM_GROUPS * SSM_STATE // halves, MXU_DIM), BF16),
            pltpu.VMEM((CONV_WIDTH, D_MODEL), BF16),
            pltpu.VMEM((D_MODEL, in_cols), BF16),
            pltpu.VMEM((2, W_IN_SLAB, in_cols), F32),
            pltpu.SemaphoreType.DMA((2,)),
        ],
        compiler_params=_ARB,
        name="mixer_front",
    )(x, norm_g, w_in, conv_w, conv_b, w_conv_out, gate_bias, bt, ct, a_re, a_im, d_skip,
      w_gate, w_up, w_down)


def _s5_scan_tile(u, bre_ref, bim_ref, are_ref, aim_ref, r_ref, state_ref):
    tm = u.shape[0]
    tiles_per_half = MXU_DIM * 4 // LANES

    for k in range(2):
        uk = u[:, k * MXU_DIM:(k + 1) * MXU_DIM]
        re = jnp.dot(uk, bre_ref[k], preferred_element_type=F32)
        im = jnp.dot(uk, bim_ref[k], preferred_element_type=F32)
        for jj in range(tiles_per_half):
            j = k * tiles_per_half + jj
            sl = slice(jj * LANES, (jj + 1) * LANES)
            r_ref[pl.ds(j, tm, stride=SCAN_PITCH), :] = re[:, sl]
            r_ref[pl.ds(HALF_ROWS + j, tm, stride=SCAN_PITCH), :] = im[:, sl]

    a_re = are_ref[...]
    a_im = aim_ref[...]

    s_re = state_ref[0:HALF_ROWS, :]
    s_im = state_ref[HALF_ROWS:, :]
    for t in range(tm):
        base = t * SCAN_PITCH
        b_re = r_ref[pl.ds(base, HALF_ROWS), :]
        b_im = r_ref[pl.ds(base + HALF_ROWS, HALF_ROWS), :]
        s_re, s_im = (a_re * s_re - a_im * s_im + b_re,
                      a_re * s_im + a_im * s_re + b_im)
        r_ref[pl.ds(base, HALF_ROWS), :] = s_re
        r_ref[pl.ds(base + HALF_ROWS, HALF_ROWS), :] = s_im
    state_ref[0:HALF_ROWS, :] = s_re
    state_ref[HALF_ROWS:, :] = s_im


def _s5_readout(u, cw_ref, d_ref, y_ref, r_ref):
    tm = u.shape[0]
    tiles_per_half = MXU_DIM * 4 // LANES
    ys = []
    for k in range(2):
        cols = []
        for half in range(2):
            for jj in range(tiles_per_half):
                j = half * HALF_ROWS + k * tiles_per_half + jj
                cols.append(r_ref[pl.ds(j, tm, stride=SCAN_PITCH), :])
        s = jnp.concatenate(cols, axis=1).astype(BF16)
        ys.append(jnp.dot(s, cw_ref[k], preferred_element_type=F32))
    y = jnp.concatenate(ys, axis=1) + d_ref[...] * u.astype(F32)
    y_ref[...] = jax.nn.gelu(y).astype(BF16)


def _s5_expand_weights(bt_ref, ct_ref, bre_ref, bim_ref, cw_ref):
    n, h = SSM_STATE, SSM_GROUP
    kb, nb = bre_ref.shape[1], bre_ref.shape[2]
    kc, nc = cw_ref.shape[1] // 2, cw_ref.shape[2]
    iota = lambda shape, d: lax.broadcasted_iota(jnp.int32, shape, d)
    tile_b = (iota((n, nb), 1) % n == iota((n, nb), 0)).astype(BF16)
    keep_b = iota((kb, nb), 0) // h == iota((kb, nb), 1) // n
    tile_c = (iota((h, nc), 1) % h == iota((h, nc), 0)).astype(BF16)
    keep_c = iota((kc, nc), 0) // n == iota((kc, nc), 1) // h
    for k in range(bre_ref.shape[0]):
        for part, dst in enumerate((bre_ref, bim_ref)):
            spread = jnp.dot(bt_ref[part, k].astype(BF16), tile_b, preferred_element_type=F32)
            dst[k] = jnp.where(keep_b, spread, 0.0).astype(BF16)
        for part in range(2):
            spread = jnp.dot(ct_ref[part, k].astype(BF16), tile_c, preferred_element_type=F32)
            cw_ref[k, part * kc:(part + 1) * kc, :] = jnp.where(keep_c, spread, 0.0).astype(BF16)


def _s5_weights(a_re, a_im, log_dt, b_re, b_im, c_re, c_im):
    g, n, h = SSM_GROUPS, SSM_STATE, SSM_GROUP
    halves = g * h // MXU_DIM
    lam = lax.complex(a_re.astype(F32), a_im.astype(F32))
    dt = jnp.exp(log_dt.astype(F32))[:, None]
    a_bar = jnp.exp(lam * dt)
    b_bar = ((a_bar - 1.0) / lam)[..., None] * lax.complex(b_re.astype(F32), b_im.astype(F32))
    bt = jnp.stack([jnp.real(b_bar), jnp.imag(b_bar)])
    bt = bt.transpose(0, 1, 3, 2).reshape(2, halves, g * h // halves, n)
    ct = jnp.stack([c_re.astype(F32), -c_im.astype(F32)])
    ct = ct.transpose(0, 1, 3, 2).reshape(2, halves, g * n // halves, h)
    tile = (HALF_ROWS, LANES)
    return bt, ct, jnp.real(a_bar).reshape(tile), jnp.imag(a_bar).reshape(tile)


def _back_kernel(yg_ref, ma_ref, gb_ref, x_ref, wa32_ref, wb32_ref, wo32_ref, g2_ref, wr32_ref,
                 br_ref, x1_ref, h2p_ref, route_ref, table_ref, counts_ref,
                 h2_ref, wa_ref, wb_ref, wo_ref, wr_ref):
    tm = x_ref.shape[0]
    step = pl.program_id(0)

    @pl.when(step == 0)
    def _():
        counts_ref[...] = jnp.zeros_like(counts_ref)
        h2_ref[...] = jnp.zeros_like(h2_ref)
        wa_ref[...] = wa32_ref[...].astype(BF16)
        wb_ref[...] = wb32_ref[...].astype(BF16)
        wo_ref[...] = wo32_ref[...].astype(BF16)
        wr32 = wr32_ref[...]
        wr_hi = wr32.astype(BF16)
        wr_ref[:, :LANES] = wr_hi
        wr_ref[:, LANES:] = (wr32 - wr_hi.astype(F32)).astype(BF16)

    h2_prev = h2_ref[...]
    h2_hi = h2_prev.astype(BF16)
    h2_lo = (h2_prev - h2_hi.astype(F32)).astype(BF16)
    both = jnp.dot(h2_hi, wr_ref[...], preferred_element_type=F32)
    logits = (both[:, :LANES]
              + (both[:, LANES:]
                 + jnp.dot(h2_lo, wr_ref[:, :LANES], preferred_element_type=F32))) + br_ref[...]
    lane = lax.broadcasted_iota(jnp.int32, logits.shape, 1)
    lane_f = lane.astype(F32)
    neg = jnp.float32(-jnp.inf)
    big = jnp.float32(LANES)
    is_grp = (lane >= ROUTE_GROUP_LANE) & (lane < ROUTE_GROUP_LANE + N_EXPERT_GROUPS)
    gl = jnp.where(is_grp, logits, neg)
    gmax = jnp.max(gl, axis=1, keepdims=True)
    gidx = jnp.min(jnp.where(gl == gmax, lane_f - ROUTE_GROUP_LANE, big), axis=1, keepdims=True)
    pg_top = 1.0 / jnp.sum(jnp.exp(gl - gmax), axis=1, keepdims=True)
    lane_grp = (lane // EXPERTS_PER_GROUP).astype(F32)
    el = jnp.where((lane < N_EXPERTS) & (lane_grp == gidx), logits, neg)
    l1 = jnp.max(el, axis=1, keepdims=True)
    i1 = jnp.min(jnp.where(el == l1, lane_f, big), axis=1, keepdims=True)
    el2 = jnp.where(lane_f == i1, neg, el)
    l2 = jnp.max(el2, axis=1, keepdims=True)
    i2 = jnp.min(jnp.where(el2 == l2, lane_f, big), axis=1, keepdims=True)
    r = jnp.exp(l2 - l1)
    w1 = pg_top / (1.0 + r)
    w2 = pg_top * r / (1.0 + r)
    oh1 = (lane_f == i1).astype(F32)
    oh2 = (lane_f == i2).astype(F32)
    picked = oh1 + oh2

    yg = yg_ref[...]
    yb = (jnp.dot(yg, wa_ref[...], preferred_element_type=F32)
          * jax.nn.sigmoid(jnp.dot(yg, wb_ref[...], preferred_element_type=F32)))
    mixed = (ma_ref[...].astype(F32) + gb_ref[...].astype(F32) * yb).astype(BF16)
    x1 = x_ref[...] + jnp.dot(mixed, wo_ref[...], preferred_element_type=F32)
    x1_ref[...] = _pack_rows(x1)
    h2 = _rms(x1, g2_ref[...])
    h2p_ref[...] = _pack_rows(h2)
    h2_ref[...] = h2

    rr = lax.broadcasted_iota(jnp.int32, (tm, tm), 0)
    cc = lax.broadcasted_iota(jnp.int32, (tm, tm), 1)
    before = (cc < rr).astype(BF16)
    prior = jnp.dot(before, picked.astype(BF16), preferred_element_type=F32) + counts_ref[...]
    rank1 = jnp.sum(oh1 * prior, axis=1, keepdims=True)
    rank2 = jnp.sum(oh2 * prior, axis=1, keepdims=True)
    counts_ref[...] += jnp.where(step > 0, jnp.sum(picked, axis=0, keepdims=True), 0.0)

    route = jnp.where(
        lane == 0, i1, jnp.where(lane == 1, i2, jnp.where(lane == 2, w1, jnp.where(
            lane == 3, w2, jnp.where(lane == 4, rank1, jnp.where(lane == 5, rank2, 0.0))))))
    route_ref[...] = route
    table_ref[...] = route.T[0:ROUTE_ROWS, :]


def _mixer_back(yg, ma, gb, x, w_glu_a, w_glu_b, w_o, norm_g, w_route, b_route, tm):
    t = x.shape[0]
    n = t // tm
    cur = lambda i: (jnp.minimum(i, n - 1), 0)
    lag = lambda i: (jnp.maximum(i - 1, 0), 0)
    return pl.pallas_call(
        _back_kernel,
        grid=(n + 1,),
        in_specs=[
            pl.BlockSpec((tm, SSM_WIDTH), cur),
            pl.BlockSpec((tm, D_MODEL), cur),
            pl.BlockSpec((tm, D_MODEL), cur),
            pl.BlockSpec((tm, D_MODEL), cur),
            _const_spec((SSM_WIDTH, D_MODEL)),
            _const_spec((SSM_WIDTH, D_MODEL)),
            _const_spec((D_MODEL, D_MODEL)),
            _const_spec((1, D_MODEL)),
            _const_spec((D_MODEL, LANES)),
            _const_spec((1, LANES)),
        ],
        out_specs=[pl.BlockSpec((tm, PACKED), cur), pl.BlockSpec((tm, PACKED), cur),
                   pl.BlockSpec((tm, LANES), lag),
                   pl.BlockSpec((ROUTE_ROWS, tm), lambda i: (0, jnp.maximum(i - 1, 0))),
                   pl.BlockSpec((1, LANES), lambda i: (0, 0))],
        out_shape=[
            jax.ShapeDtypeStruct((t, PACKED), I32),
            jax.ShapeDtypeStruct((t, PACKED), I32),
            jax.ShapeDtypeStruct((t, LANES), F32),
            jax.ShapeDtypeStruct((ROUTE_ROWS, t), F32),
            jax.ShapeDtypeStruct((1, LANES), F32),
        ],
        scratch_shapes=[
            pltpu.VMEM((tm, D_MODEL), F32),
            pltpu.VMEM((SSM_WIDTH, D_MODEL), BF16),
            pltpu.VMEM((SSM_WIDTH, D_MODEL), BF16),
            pltpu.VMEM((D_MODEL, D_MODEL), BF16),
            pltpu.VMEM((D_MODEL, 2 * LANES), BF16),
        ],
        compiler_params=_ARB,
        name="mixer_back",
    )(yg, ma, gb, x, w_glu_a, w_glu_b, w_o, norm_g, w_route, b_route)


_SC_MESH = dict(core_axis_name="c", subcore_axis_name="s")


def _sc_worker():
    return lax.axis_index("s") * SC_CORES + lax.axis_index("c")


def _sc_dispatch(rows, slot0, slot1, nslot):
    t, width = rows.shape
    workers = SC_CORES * SC_SUBCORES
    per_w = t // workers
    assert per_w % SC_CHUNK == 0
    nchunk = per_w // SC_CHUNK

    @functools.partial(
        pl.kernel,
        out_type=jax.ShapeDtypeStruct((nslot, width), rows.dtype),
        mesh=plsc.VectorSubcoreMesh(**_SC_MESH),
        scratch_types=[
            pltpu.VMEM((nchunk, SC_CHUNK), I32),
            pltpu.VMEM((nchunk, SC_CHUNK), I32),
            pltpu.VMEM((SC_CHUNK, width), rows.dtype),
            pltpu.SemaphoreType.DMA,
            pltpu.SemaphoreType.DMA,
        ],
        name="moe_dispatch",
    )
    def k(rows_hbm, s0_hbm, s1_hbm, out_hbm, i0_v, i1_v, rows_v, sem0, sem1):
        w = _sc_worker()
        pltpu.sync_copy(s0_hbm.at[w], i0_v)
        pltpu.sync_copy(s1_hbm.at[w], i1_v)

        @pl.loop(0, nchunk)
        def _(c):
            pltpu.sync_copy(rows_hbm.at[pl.ds(w * per_w + c * SC_CHUNK, SC_CHUNK)], rows_v)
            first = pltpu.async_copy(rows_v, out_hbm.at[i0_v.at[c]], sem0)
            second = pltpu.async_copy(rows_v, out_hbm.at[i1_v.at[c]], sem1)
            first.wait()
            second.wait()

    shape = (workers, nchunk, SC_CHUNK)
    return k(rows, slot0.reshape(shape), slot1.reshape(shape))


def _sc_gather(table, idx):
    n = idx.shape[0]
    width = table.shape[1]
    per_w = n // (SC_CORES * SC_SUBCORES)
    assert per_w % SC_CHUNK == 0

    @functools.partial(
        pl.kernel,
        out_type=jax.ShapeDtypeStruct((n, width), table.dtype),
        mesh=plsc.VectorSubcoreMesh(**_SC_MESH),
        scratch_types=[
            pltpu.VMEM((1, SC_CHUNK), I32),
            pltpu.VMEM((SC_CHUNK, width), table.dtype),
        ],
        name="moe_combine_gather",
    )
    def k(table_hbm, idx_hbm, out_hbm, idx_v, rows_v):
        base = _sc_worker() * per_w

        @pl.loop(0, per_w // SC_CHUNK)
        def _(c):
            off = base + c * SC_CHUNK
            pltpu.sync_copy(idx_hbm.at[:, pl.ds(off, SC_CHUNK)], idx_v)
            pltpu.sync_copy(table_hbm.at[idx_v.at[0]], rows_v)
            pltpu.sync_copy(rows_v, out_hbm.at[pl.ds(off, SC_CHUNK)])

    return k(table, idx.reshape(1, n))


def _moe_kernel(first_ref, last_ref, count_ref, total_ref,
                xs_hbm, wg_ref, wu_ref, wd_ref, out_hbm,
                xbuf0, xbuf1, obuf0, obuf1, xsem, osem):
    e = pl.program_id(0)
    ch = MOE_TILE
    half = ch // 2
    first, last, count, total = first_ref[e], last_ref[e], count_ref[e], total_ref[0]
    xbufs = (xbuf0, xbuf1)
    obufs = (obuf0, obuf1)

    def x_copy(c, s):
        rows = pl.ds(pl.multiple_of(c * ch, ch), ch)
        return pltpu.make_async_copy(xs_hbm.at[rows], xbufs[s], xsem.at[s])

    def o_copy(c, s):
        rows = pl.ds(pl.multiple_of(c * ch, ch), ch)
        return pltpu.make_async_copy(obufs[s], out_hbm.at[rows], osem.at[s])

    @pl.when(e == 0)
    def _():
        x_copy(0, 0).start()

    def experts_rows(s, rows, valid):
        row = lax.broadcasted_iota(jnp.int32, (rows, PACKED), 0)
        words = jnp.where(row < valid, xbufs[s][0:rows, :], 0)
        lo, hi = _unpack_rows(words)
        xs = jnp.concatenate([lo.astype(BF16), hi.astype(BF16)], axis=1)
        gate = jnp.dot(xs, wg_ref[0], preferred_element_type=F32)
        up = jnp.dot(xs, wu_ref[0], preferred_element_type=F32)
        act = (gate * jax.nn.sigmoid(gate) * up).astype(BF16)
        obufs[s][0:rows, :] = _pack_rows(jnp.dot(act, wd_ref[0], preferred_element_type=F32))

    def chunk_in_slot(c, s):
        x_copy(c, s).wait()

        @pl.when(c + 1 < total)
        def _():
            x_copy(c + 1, 1 - s).start()

        @pl.when(c >= 2)
        def _():
            o_copy(c - 2, s).wait()

        valid = count - (c - first) * ch

        sizes = (ch // 4, half, ch)
        for lo, rows in zip((0,) + sizes[:-1], sizes):
            @pl.when((valid > lo) & ((valid <= rows) | (rows == ch)))
            def _():
                experts_rows(s, rows, valid)
                if rows < ch:
                    obufs[s][rows:, :] = jnp.zeros((ch - rows, PACKED), I32)

        o_copy(c, s).start()

    def chunk(c, carry):
        for s in range(2):
            @pl.when(c % 2 == s)
            def _():
                chunk_in_slot(c, s)
        return carry

    lax.fori_loop(first, last, chunk, 0)

    @pl.when(e == pl.num_programs(0) - 1)
    def _():
        for s in range(2):
            @pl.when(((total - 1) % 2 == s) | (total >= 2))
            def _():
                o_copy(0, s).wait()


def _moe_experts(plan, xs, w_gate, w_up, w_down):
    nslot = xs.shape[0]
    any_spec = pl.BlockSpec(memory_space=pl.ANY)
    grid_spec = pltpu.PrefetchScalarGridSpec(
        num_scalar_prefetch=len(plan),
        grid=(N_EXPERTS,),
        in_specs=[
            any_spec,
            pl.BlockSpec((1, D_MODEL, EXPERT_FF), lambda e, *_: (e, 0, 0)),
            pl.BlockSpec((1, D_MODEL, EXPERT_FF), lambda e, *_: (e, 0, 0)),
            pl.BlockSpec((1, EXPERT_FF, D_MODEL), lambda e, *_: (e, 0, 0)),
        ],
        out_specs=any_spec,
        scratch_shapes=[
            pltpu.VMEM((MOE_TILE, PACKED), I32),
            pltpu.VMEM((MOE_TILE, PACKED), I32),
            pltpu.VMEM((MOE_TILE, PACKED), I32),
            pltpu.VMEM((MOE_TILE, PACKED), I32),
            pltpu.SemaphoreType.DMA((2,)),
            pltpu.SemaphoreType.DMA((2,)),
        ],
    )
    return pl.pallas_call(
        _moe_kernel,
        grid_spec=grid_spec,
        out_shape=jax.ShapeDtypeStruct((nslot, PACKED), I32),
        compiler_params=_ARB,
        name="moe_experts",
    )(*plan, xs, w_gate, w_up, w_down)


def _dispatch_plan(table, counts, tm):
    e_ids = table[0:2].astype(I32)
    ranks = table[4:6].astype(I32)
    counts = counts[0, :N_EXPERTS].astype(I32)
    padded = ((counts + tm - 1) // tm) * tm
    ends = jnp.cumsum(padded)
    starts = ends - padded
    experts = jnp.arange(N_EXPERTS, dtype=I32)
    slot = jnp.sum(jnp.where(e_ids[..., None] == experts, starts, 0), axis=-1) + ranks
    plan = (starts // tm, ends // tm, counts, ends[-1:] // tm)
    return slot, tuple(p.astype(I32) for p in plan)


def _final_kernel(x1_ref, y0_ref, y1_ref, route_ref, g_ref, *rest):
    out_ref = rest[-1]
    route = route_ref[...]
    w1 = route[:, 2:3]
    w2 = route[:, 3:4]
    x_lo, x_hi = _unpack_rows(x1_ref[...])
    a_lo, a_hi = _unpack_rows(y0_ref[...])
    b_lo, b_hi = _unpack_rows(y1_ref[...])
    x2 = jnp.concatenate([x_lo + (w1 * a_lo + w2 * b_lo), x_hi + (w1 * a_hi + w2 * b_hi)], axis=1)
    out_ref[...] = _rms(x2, g_ref[...])


def _final(x1, ycat, route, norm_g, tm, chunk, prev):
    t = x1.shape[0]
    nblk = t // tm // FINAL_CHUNKS
    off = chunk * nblk
    in_specs = [
        pl.BlockSpec((tm, PACKED), lambda i: (i + off, 0)),
        pl.BlockSpec((tm, PACKED), lambda i: (i, 0)),
        pl.BlockSpec((tm, PACKED), lambda i: (i + nblk, 0)),
        pl.BlockSpec((tm, LANES), lambda i: (i + off, 0)),
        _const_spec((1, D_MODEL)),
    ]
    args = [x1, ycat, ycat, route, norm_g]
    aliases = {}
    if prev is not None:
        in_specs.append(pl.BlockSpec(memory_space=pl.ANY))
        args.append(prev)
        aliases = {len(args) - 1: 0}
    return pl.pallas_call(
        _final_kernel,
        grid=(nblk,),
        in_specs=in_specs,
        out_specs=pl.BlockSpec((tm, D_MODEL), lambda i: (i + off, 0)),
        out_shape=jax.ShapeDtypeStruct((t, D_MODEL), F32),
        input_output_aliases=aliases,
        compiler_params=_ARB,
        name="final_norm",
    )(*args)


def kernel(x, norm_mix, w_in, conv_w, conv_b, w_conv_out, ssm_a_re, ssm_a_im, ssm_log_dt,
           ssm_b_re, ssm_b_im, ssm_c_re, ssm_c_im, ssm_d, w_glu_a, w_glu_b, gate_bias, w_o,
           norm_ffn, w_route_group, b_route_group, w_route_expert, b_route_expert,
           w_gate, w_up, w_down, norm_final):
    bsz, length, d = x.shape
    assert d == D_MODEL and norm_mix.shape[0] == 1
    t = bsz * length
    tm = TOKEN_TILE
    assert bsz == 1 and t % (FINAL_TILE * FINAL_CHUNKS) == 0
    xt = x.reshape(t, d)
    row = lambda a: a.reshape(1, -1).astype(F32)

    s5_weights = _s5_weights(
        ssm_a_re[0], ssm_a_im[0], ssm_log_dt[0], ssm_b_re[0], ssm_b_im[0],
        ssm_c_re[0], ssm_c_im[0])
    ma, gb, yg, wg_bf, wu_bf, wd_bf = _mixer_front(
        xt, row(norm_mix[0]), w_in[0], conv_w[0].reshape(3, CONV_WIDTH),
        row(conv_b[0]), w_conv_out[0], gate_bias[0], s5_weights, row(ssm_d[0]),
        w_gate[0], w_up[0], w_down[0], FRONT_TILE)

    pad = LANES - N_EXPERTS - N_EXPERT_GROUPS
    w_route = jnp.concatenate(
        [w_route_expert[0], w_route_group[0], jnp.zeros((d, pad), F32)], axis=1)
    b_route = jnp.concatenate(
        [b_route_expert[0], b_route_group[0], jnp.zeros((pad,), F32)]).reshape(1, LANES)
    x1, h2p, route, table, counts = _mixer_back(
        yg, ma, gb, xt, w_glu_a[0], w_glu_b[0], w_o[0], row(norm_ffn[0]), w_route, b_route, tm)

    slot, plan = _dispatch_plan(table, counts, MOE_TILE)
    nslot = 2 * t + N_EXPERTS * MOE_TILE
    xs = _sc_dispatch(h2p, slot[0], slot[1], nslot)
    ys = _moe_experts(plan, xs, wg_bf, wu_bf, wd_bf)
    out = None
    tc = t // FINAL_CHUNKS
    for k in range(FINAL_CHUNKS):
        ycat = _sc_gather(ys, slot[:, k * tc:(k + 1) * tc].reshape(-1))
        out = _final(x1, ycat, route, row(norm_final), FINAL_TILE, k, out)
    return out.reshape(bsz, length, d)
```

```python
import functools

import jax
import jax.numpy as jnp
from jax import lax
from jax.experimental import pallas as pl
from jax.experimental.pallas import tpu as pltpu
from jax.experimental.pallas import tpu_sc as plsc

F32 = jnp.float32
BF16 = jnp.bfloat16
I32 = jnp.int32

D_MODEL = 1024
CONV_WIDTH = 1024
SSM_WIDTH = 512
SSM_GROUP = 16
SSM_GROUPS = 32
SSM_STATE = 64
N_EXPERT_GROUPS = 4
EXPERTS_PER_GROUP = 8
N_EXPERTS = 32
EXPERT_FF = 512
EPS = 1e-6

LANES = 128
MXU_DIM = 256
TOKEN_TILE = 512
FINAL_TILE = 1024
W_IN_SLAB = 128
FRONT_TILE = 256
MOE_TILE = 512
SCAN_PITCH = 36
STATE_ROWS = 2 * SSM_GROUPS * SSM_STATE // LANES
HALF_ROWS = STATE_ROWS // 2
ROUTE_GROUP_LANE = N_EXPERTS
ROUTE_ROWS = 8
FINAL_CHUNKS = 4
PACKED = D_MODEL // 2
HI_MASK = -65536
SC_CORES = 2
SC_SUBCORES = 16
SC_CHUNK = 128
VMEM_LIMIT = 56 * 1024 * 1024

_ARB = pltpu.CompilerParams(dimension_semantics=("arbitrary",), vmem_limit_bytes=VMEM_LIMIT)


def _const_spec(shape):
    nd = len(shape)
    return pl.BlockSpec(shape, lambda i, *_: (0,) * nd, pipeline_mode=pl.Buffered(1))


def _row_spec(tm, width):
    return pl.BlockSpec((tm, width), lambda i, *_: (i, 0))


def _rms(x, g):
    ms = jnp.mean(x * x, axis=-1, keepdims=True)
    return x * lax.rsqrt(ms + EPS) * g


def _pack_rows(v):
    bits = lax.bitcast_convert_type(v.astype(BF16).astype(F32), I32)
    lo = lax.shift_right_logical(bits[:, :PACKED], 16)
    hi = bits[:, PACKED:] & HI_MASK
    return lo | hi


def _unpack_rows(w):
    lo = lax.bitcast_convert_type(lax.shift_left(w, 16), F32)
    hi = lax.bitcast_convert_type(w & HI_MASK, F32)
    return lo, hi


def _front_kernel(x_ref, g_ref, win_hbm, cw_ref, cb_ref, wco32_ref, gbias_ref,
                  bt_ref, ct_ref, are_ref, aim_ref, d_ref, wg_ref, wu_ref, wd_ref,
                  ma_ref, gb_ref, y_ref, wgb_ref, wub_ref, wdb_ref,
                  carry_ref, u_ref, r_ref, state_ref, bre_ref, bim_ref, cw2_ref, wco_ref,
                  win_ref, stage_ref, stage_sem):
    tm = x_ref.shape[0]
    wgb_ref[...] = wg_ref[...].astype(BF16)
    wub_ref[...] = wu_ref[...].astype(BF16)
    wdb_ref[...] = wd_ref[...].astype(BF16)
    c0, c1, c2 = CONV_WIDTH, 2 * CONV_WIDTH, 3 * CONV_WIDTH
    c3 = c2 + SSM_WIDTH

    @pl.when(pl.program_id(0) == 0)
    def _():
        carry_ref[...] = jnp.zeros_like(carry_ref)
        u_ref[...] = jnp.zeros_like(u_ref)
        state_ref[...] = jnp.zeros_like(state_ref)
        _s5_expand_weights(bt_ref, ct_ref, bre_ref, bim_ref, cw2_ref)
        wco_ref[...] = wco32_ref[...].astype(BF16)
        slab = stage_ref.shape[1]
        copies = [pltpu.make_async_copy(win_hbm.at[pl.ds(j * slab, slab)], stage_ref.at[j % 2],
                                        stage_sem.at[j % 2]) for j in range(win_ref.shape[0] // slab)]
        copies[0].start()
        for j, copy in enumerate(copies):
            if j + 1 < len(copies):
                copies[j + 1].start()
            copy.wait()
            win_ref[j * slab:(j + 1) * slab, :] = stage_ref[j % 2].astype(BF16)

    h = _rms(x_ref[...], g_ref[...]).astype(BF16)
    u_prev = u_ref[...]
    _s5_scan_tile(u_prev, bre_ref, bim_ref, are_ref, aim_ref, r_ref, state_ref)

    def proj(lo, hi):
        return jnp.dot(h, win_ref[:, lo:hi], preferred_element_type=F32)

    v = proj(c1, c2) * proj(0, c0)
    row = lax.broadcasted_iota(jnp.int32, v.shape, 0)
    prev1 = carry_ref[7:8, :]
    prev2 = carry_ref[6:7, :]
    v1 = jnp.where(row == 0, prev1, pltpu.roll(v, 1, 0))
    v2 = jnp.where(row == 0, prev2, jnp.where(row == 1, prev1, pltpu.roll(v, 2, 0)))
    carry_ref[...] = v[tm - 8:, :]
    y = cw_ref[0:1, :] * v2 + cw_ref[1:2, :] * v1 + cw_ref[2:3, :] * v + cb_ref[...]
    z = (proj(c0, c1) * y).astype(BF16)
    ya = jnp.dot(z, wco_ref[...], preferred_element_type=F32)
    ga = jax.nn.sigmoid(proj(c3, c3 + D_MODEL) + gbias_ref[0:1, :])
    ma_ref[...] = (ga * ya).astype(BF16)
    gb = jax.nn.sigmoid(proj(c3 + D_MODEL, c3 + 2 * D_MODEL) + gbias_ref[1:2, :])
    gb_ref[...] = gb.astype(BF16)
    u_new = proj(c2, c3).astype(BF16)
    _s5_readout(u_prev, cw2_ref, d_ref, y_ref, r_ref)
    u_ref[...] = u_new


def _mixer_front(x, norm_g, w_in, conv_w, conv_b, w_conv_out, gate_bias, s5_weights, d_skip,
                 w_gate, w_up, w_down, tm):
    bt, ct, a_re, a_im = s5_weights
    halves = SSM_GROUPS * SSM_GROUP // MXU_DIM
    t = x.shape[0]
    in_cols = w_in.shape[1]
    n = t // tm
    last = lambda i: (jnp.minimum(i, n - 1), 0)
    lag = lambda i: (jnp.maximum(i - 1, 0), 0)
    split = max(k for k in (1, 2, 4, 8) if N_EXPERTS * k <= n + 1)
    piece = lambda i: (jnp.minimum(i, N_EXPERTS * split - 1) // split,
                       jnp.minimum(i, N_EXPERTS * split - 1) % split, 0)
    w_specs = [pl.BlockSpec((1, D_MODEL // split, EXPERT_FF), piece),
               pl.BlockSpec((1, D_MODEL // split, EXPERT_FF), piece),
               pl.BlockSpec((1, EXPERT_FF // split, D_MODEL), piece)]
    return pl.pallas_call(
        _front_kernel,
        grid=(n + 1,),
        in_specs=[
            pl.BlockSpec((tm, D_MODEL), last),
            _const_spec((1, D_MODEL)),
            pl.BlockSpec(memory_space=pl.ANY),
            _const_spec((3, CONV_WIDTH)),
            _const_spec((1, CONV_WIDTH)),
            _const_spec((CONV_WIDTH, D_MODEL)),
            _const_spec((2, D_MODEL)),
            _const_spec(bt.shape),
            _const_spec(ct.shape),
            _const_spec(a_re.shape),
            _const_spec(a_im.shape),
            _const_spec((1, SSM_WIDTH)),
        ] + w_specs,
        out_specs=[_row_spec(tm, D_MODEL), _row_spec(tm, D_MODEL),
                   pl.BlockSpec((tm, SSM_WIDTH), lag)] + w_specs,
        out_shape=[
            jax.ShapeDtypeStruct((t + tm, D_MODEL), BF16),
            jax.ShapeDtypeStruct((t + tm, D_MODEL), BF16),
            jax.ShapeDtypeStruct((t, SSM_WIDTH), BF16),
            jax.ShapeDtypeStruct(w_gate.shape, BF16),
            jax.ShapeDtypeStruct(w_up.shape, BF16),
            jax.ShapeDtypeStruct(w_down.shape, BF16),
        ],
        scratch_shapes=[
            pltpu.VMEM((8, CONV_WIDTH), F32),
            pltpu.VMEM((tm, SSM_WIDTH), BF16),
            pltpu.VMEM((tm * SCAN_PITCH, LANES), F32),
            pltpu.VMEM((STATE_ROWS, LANES), F32),
            pltpu.VMEM((halves, MXU_DIM, SSM_GROUPS * SSM_STATE // halves), BF16),
            pltpu.VMEM((halves, MXU_DIM, SSM_GROUPS * SSM_STATE // halves), BF16),
            pltpu.VMEM((halves, 2 * SSM_GROUPS * SSM_STATE // halves, MXU_DIM), BF16),
            pltpu.VMEM((CONV_WIDTH, D_MODEL), BF16),
            pltpu.VMEM((D_MODEL, in_cols), BF16),
            pltpu.VMEM((2, W_IN_SLAB, in_cols), F32),
            pltpu.SemaphoreType.DMA((2,)),
        ],
        compiler_params=_ARB,
        name="mixer_front",
    )(x, norm_g, w_in, conv_w, conv_b, w_conv_out, gate_bias, bt, ct, a_re, a_im, d_skip,
      w_gate, w_up, w_down)


def _s5_scan_tile(u, bre_ref, bim_ref, are_ref, aim_ref, r_ref, state_ref):
    tm = u.shape[0]
    tiles_per_half = MXU_DIM * 4 // LANES

    for k in range(2):
        uk = u[:, k * MXU_DIM:(k + 1) * MXU_DIM]
        re = jnp.dot(uk, bre_ref[k], preferred_element_type=F32)
        im = jnp.dot(uk, bim_ref[k], preferred_element_type=F32)
        for jj in range(tiles_per_half):
            j = k * tiles_per_half + jj
            sl = slice(jj * LANES, (jj + 1) * LANES)
            r_ref[pl.ds(j, tm, stride=SCAN_PITCH), :] = re[:, sl]
            r_ref[pl.ds(HALF_ROWS + j, tm, stride=SCAN_PITCH), :] = im[:, sl]

    a_re = are_ref[...]
    a_im = aim_ref[...]

    s_re = state_ref[0:HALF_ROWS, :]
    s_im = state_ref[HALF_ROWS:, :]
    for t in range(tm):
        base = t * SCAN_PITCH
        b_re = r_ref[pl.ds(base, HALF_ROWS), :]
        b_im = r_ref[pl.ds(base + HALF_ROWS, HALF_ROWS), :]
        s_re, s_im = (a_re * s_re - a_im * s_im + b_re,
                      a_re * s_im + a_im * s_re + b_im)
        r_ref[pl.ds(base, HALF_ROWS), :] = s_re
        r_ref[pl.ds(base + HALF_ROWS, HALF_ROWS), :] = s_im
    state_ref[0:HALF_ROWS, :] = s_re
    state_ref[HALF_ROWS:, :] = s_im


def _s5_readout(u, cw_ref, d_ref, y_ref, r_ref):
    tm = u.shape[0]
    tiles_per_half = MXU_DIM * 4 // LANES
    ys = []
    for k in range(2):
        cols = []
        for half in range(2):
            for jj in range(tiles_per_half):
                j = half * HALF_ROWS + k * tiles_per_half + jj
                cols.append(r_ref[pl.ds(j, tm, stride=SCAN_PITCH), :])
        s = jnp.concatenate(cols, axis=1).astype(BF16)
        ys.append(jnp.dot(s, cw_ref[k], preferred_element_type=F32))
    y = jnp.concatenate(ys, axis=1) + d_ref[...] * u.astype(F32)
    y_ref[...] = jax.nn.gelu(y).astype(BF16)


def _s5_expand_weights(bt_ref, ct_ref, bre_ref, bim_ref, cw_ref):
    n, h = SSM_STATE, SSM_GROUP
    kb, nb = bre_ref.shape[1], bre_ref.shape[2]
    kc, nc = cw_ref.shape[1] // 2, cw_ref.shape[2]
    iota = lambda shape, d: lax.broadcasted_iota(jnp.int32, shape, d)
    tile_b = (iota((n, nb), 1) % n == iota((n, nb), 0)).astype(BF16)
    keep_b = iota((kb, nb), 0) // h == iota((kb, nb), 1) // n
    tile_c = (iota((h, nc), 1) % h == iota((h, nc), 0)).astype(BF16)
    keep_c = iota((kc, nc), 0) // n == iota((kc, nc), 1) // h
    for k in range(bre_ref.shape[0]):
        for part, dst in enumerate((bre_ref, bim_ref)):
            spread = jnp.dot(bt_ref[part, k].astype(BF16), tile_b, preferred_element_type=F32)
            dst[k] = jnp.where(keep_b, spread, 0.0).astype(BF16)
        for part in range(2):
            spread = jnp.dot(ct_ref[part, k].astype(BF16), tile_c, preferred_element_type=F32)
            cw_ref[k, part * kc:(part + 1) * kc, :] = jnp.where(keep_c, spread, 0.0).astype(BF16)


def _s5_weights(a_re, a_im, log_dt, b_re, b_im, c_re, c_im):
    g, n, h = SSM_GROUPS, SSM_STATE, SSM_GROUP
    halves = g * h // MXU_DIM
    lam = lax.complex(a_re.astype(F32), a_im.astype(F32))
    dt = jnp.exp(log_dt.astype(F32))[:, None]
    a_bar = jnp.exp(lam * dt)
    b_bar = ((a_bar - 1.0) / lam)[..., None] * lax.complex(b_re.astype(F32), b_im.astype(F32))
    bt = jnp.stack([jnp.real(b_bar), jnp.imag(b_bar)])
    bt = bt.transpose(0, 1, 3, 2).reshape(2, halves, g * h // halves, n)
    ct = jnp.stack([c_re.astype(F32), -c_im.astype(F32)])
    ct = ct.transpose(0, 1, 3, 2).reshape(2, halves, g * n // halves, h)
    tile = (HALF_ROWS, LANES)
    return bt, ct, jnp.real(a_bar).reshape(tile), jnp.imag(a_bar).reshape(tile)


def _back_kernel(yg_ref, ma_ref, gb_ref, x_ref, wa32_ref, wb32_ref, wo32_ref, g2_ref, wr32_ref,
                 br_ref, x1_ref, h2p_ref, route_ref, table_ref, counts_ref,
                 h2_ref, wa_ref, wb_ref, wo_ref, wr_ref):
    tm = x_ref.shape[0]
    step = pl.program_id(0)

    @pl.when(step == 0)
    def _():
        counts_ref[...] = jnp.zeros_like(counts_ref)
        h2_ref[...] = jnp.zeros_like(h2_ref)
        wa_ref[...] = wa32_ref[...].astype(BF16)
        wb_ref[...] = wb32_ref[...].astype(BF16)
        wo_ref[...] = wo32_ref[...].astype(BF16)
        wr32 = wr32_ref[...]
        wr_hi = wr32.astype(BF16)
        wr_ref[:, :LANES] = wr_hi
        wr_ref[:, LANES:] = (wr32 - wr_hi.astype(F32)).astype(BF16)

    h2_prev = h2_ref[...]
    h2_hi = h2_prev.astype(BF16)
    h2_lo = (h2_prev - h2_hi.astype(F32)).astype(BF16)
    both = jnp.dot(h2_hi, wr_ref[...], preferred_element_type=F32)
    logits = (both[:, :LANES]
              + (both[:, LANES:]
                 + jnp.dot(h2_lo, wr_ref[:, :LANES], preferred_element_type=F32))) + br_ref[...]
    lane = lax.broadcasted_iota(jnp.int32, logits.shape, 1)
    lane_f = lane.astype(F32)
    neg = jnp.float32(-jnp.inf)
    big = jnp.float32(LANES)
    is_grp = (lane >= ROUTE_GROUP_LANE) & (lane < ROUTE_GROUP_LANE + N_EXPERT_GROUPS)
    gl = jnp.where(is_grp, logits, neg)
    gmax = jnp.max(gl, axis=1, keepdims=True)
    gidx = jnp.min(jnp.where(gl == gmax, lane_f - ROUTE_GROUP_LANE, big), axis=1, keepdims=True)
    pg_top = 1.0 / jnp.sum(jnp.exp(gl - gmax), axis=1, keepdims=True)
    lane_grp = (lane // EXPERTS_PER_GROUP).astype(F32)
    el = jnp.where((lane < N_EXPERTS) & (lane_grp == gidx), logits, neg)
    l1 = jnp.max(el, axis=1, keepdims=True)
    i1 = jnp.min(jnp.where(el == l1, lane_f, big), axis=1, keepdims=True)
    el2 = jnp.where(lane_f == i1, neg, el)
    l2 = jnp.max(el2, axis=1, keepdims=True)
    i2 = jnp.min(jnp.where(el2 == l2, lane_f, big), axis=1, keepdims=True)
    r = jnp.exp(l2 - l1)
    w1 = pg_top / (1.0 + r)
    w2 = pg_top * r / (1.0 + r)
    oh1 = (lane_f == i1).astype(F32)
    oh2 = (lane_f == i2).astype(F32)
    picked = oh1 + oh2

    yg = yg_ref[...]
    yb = (jnp.dot(yg, wa_ref[...], preferred_element_type=F32)
          * jax.nn.sigmoid(jnp.dot(yg, wb_ref[...], preferred_element_type=F32)))
    mixed = (ma_ref[...].astype(F32) + gb_ref[...].astype(F32) * yb).astype(BF16)
    x1 = x_ref[...] + jnp.dot(mixed, wo_ref[...], preferred_element_type=F32)
    x1_ref[...] = _pack_rows(x1)
    h2 = _rms(x1, g2_ref[...])
    h2p_ref[...] = _pack_rows(h2)
    h2_ref[...] = h2

    rr = lax.broadcasted_iota(jnp.int32, (tm, tm), 0)
    cc = lax.broadcasted_iota(jnp.int32, (tm, tm), 1)
    before = (cc < rr).astype(BF16)
    prior = jnp.dot(before, picked.astype(BF16), preferred_element_type=F32) + counts_ref[...]
    rank1 = jnp.sum(oh1 * prior, axis=1, keepdims=True)
    rank2 = jnp.sum(oh2 * prior, axis=1, keepdims=True)
    counts_ref[...] += jnp.where(step > 0, jnp.sum(picked, axis=0, keepdims=True), 0.0)

    route = jnp.where(
        lane == 0, i1, jnp.where(lane == 1, i2, jnp.where(lane == 2, w1, jnp.where(
            lane == 3, w2, jnp.where(lane == 4, rank1, jnp.where(lane == 5, rank2, 0.0))))))
    route_ref[...] = route
    table_ref[...] = route.T[0:ROUTE_ROWS, :]


def _mixer_back(yg, ma, gb, x, w_glu_a, w_glu_b, w_o, norm_g, w_route, b_route, tm):
    t = x.shape[0]
    n = t // tm
    cur = lambda i: (jnp.minimum(i, n - 1), 0)
    lag = lambda i: (jnp.maximum(i - 1, 0), 0)
    return pl.pallas_call(
        _back_kernel,
        grid=(n + 1,),
        in_specs=[
            pl.BlockSpec((tm, SSM_WIDTH), cur),
            pl.BlockSpec((tm, D_MODEL), cur),
            pl.BlockSpec((tm, D_MODEL), cur),
            pl.BlockSpec((tm, D_MODEL), cur),
            _const_spec((SSM_WIDTH, D_MODEL)),
            _const_spec((SSM_WIDTH, D_MODEL)),
            _const_spec((D_MODEL, D_MODEL)),
            _const_spec((1, D_MODEL)),
            _const_spec((D_MODEL, LANES)),
            _const_spec((1, LANES)),
        ],
        out_specs=[pl.BlockSpec((tm, PACKED), cur), pl.BlockSpec((tm, PACKED), cur),
                   pl.BlockSpec((tm, LANES), lag),
                   pl.BlockSpec((ROUTE_ROWS, tm), lambda i: (0, jnp.maximum(i - 1, 0))),
                   pl.BlockSpec((1, LANES), lambda i: (0, 0))],
        out_shape=[
            jax.ShapeDtypeStruct((t, PACKED), I32),
            jax.ShapeDtypeStruct((t, PACKED), I32),
            jax.ShapeDtypeStruct((t, LANES), F32),
            jax.ShapeDtypeStruct((ROUTE_ROWS, t), F32),
            jax.ShapeDtypeStruct((1, LANES), F32),
        ],
        scratch_shapes=[
            pltpu.VMEM((tm, D_MODEL), F32),
            pltpu.VMEM((SSM_WIDTH, D_MODEL), BF16),
            pltpu.VMEM((SSM_WIDTH, D_MODEL), BF16),
            pltpu.VMEM((D_MODEL, D_MODEL), BF16),
            pltpu.VMEM((D_MODEL, 2 * LANES), BF16),
        ],
        compiler_params=_ARB,
        name="mixer_back",
    )(yg, ma, gb, x, w_glu_a, w_glu_b, w_o, norm_g, w_route, b_route)


_SC_MESH = dict(core_axis_name="c", subcore_axis_name="s")


def _sc_worker():
    return lax.axis_index("s") * SC_CORES + lax.axis_index("c")


def _sc_dispatch(rows, slot0, slot1, nslot):
    t, width = rows.shape
    workers = SC_CORES * SC_SUBCORES
    per_w = t // workers
    assert per_w % SC_CHUNK == 0
    nchunk = per_w // SC_CHUNK

    @functools.partial(
        pl.kernel,
        out_type=jax.ShapeDtypeStruct((nslot, width), rows.dtype),
        mesh=plsc.VectorSubcoreMesh(**_SC_MESH),
        scratch_types=[
            pltpu.VMEM((nchunk, SC_CHUNK), I32),
            pltpu.VMEM((nchunk, SC_CHUNK), I32),
            pltpu.VMEM((SC_CHUNK, width), rows.dtype),
            pltpu.SemaphoreType.DMA,
            pltpu.SemaphoreType.DMA,
        ],
        name="moe_dispatch",
    )
    def k(rows_hbm, s0_hbm, s1_hbm, out_hbm, i0_v, i1_v, rows_v, sem0, sem1):
        w = _sc_worker()
        pltpu.sync_copy(s0_hbm.at[w], i0_v)
        pltpu.sync_copy(s1_hbm.at[w], i1_v)

        @pl.loop(0, nchunk)
        def _(c):
            pltpu.sync_copy(rows_hbm.at[pl.ds(w * per_w + c * SC_CHUNK, SC_CHUNK)], rows_v)
            first = pltpu.async_copy(rows_v, out_hbm.at[i0_v.at[c]], sem0)
            second = pltpu.async_copy(rows_v, out_hbm.at[i1_v.at[c]], sem1)
            first.wait()
            second.wait()

    shape = (workers, nchunk, SC_CHUNK)
    return k(rows, slot0.reshape(shape), slot1.reshape(shape))


def _sc_gather(table, idx):
    n = idx.shape[0]
    width = table.shape[1]
    workers = SC_CORES * SC_SUBCORES
    per_w = n // workers
    assert per_w % SC_CHUNK == 0
    nrow = per_w // SC_CHUNK
    part = SC_CHUNK // 2
    nparts = per_w // part

    @functools.partial(
        pl.kernel,
        out_type=jax.ShapeDtypeStruct((n, width), table.dtype),
        mesh=plsc.VectorSubcoreMesh(**_SC_MESH),
        scratch_types=[
            pltpu.VMEM((nrow, SC_CHUNK), I32),
            pltpu.VMEM((2, part, width), table.dtype),
            pltpu.SemaphoreType.DMA,
            pltpu.SemaphoreType.DMA,
            pltpu.SemaphoreType.DMA,
            pltpu.SemaphoreType.DMA,
        ],
        name="moe_combine_gather",
    )
    def k(table_hbm, idx_hbm, out_hbm, idx_v, rows_v, g0, g1, w0, w1):
        w = _sc_worker()
        pltpu.sync_copy(idx_hbm.at[w], idx_v)
        gsem, wsem = (g0, g1), (w0, w1)

        def gather(j):
            ids = idx_v.at[j // 2, pl.ds((j % 2) * part, part)]
            return pltpu.async_copy(table_hbm.at[ids], rows_v.at[j % 2], gsem[j % 2])

        def write(j):
            rows = pl.ds(w * per_w + j * part, part)
            return pltpu.async_copy(rows_v.at[j % 2], out_hbm.at[rows], wsem[j % 2])

        gathers = {j: gather(j) for j in range(min(2, nparts))}
        writes = {}
        for j in range(nparts):
            gathers[j].wait()
            writes[j] = write(j)
            if j + 2 < nparts:
                writes[j].wait()
                gathers[j + 2] = gather(j + 2)
        for j in range(max(nparts - 2, 0), nparts):
            writes[j].wait()

    return k(table, idx.reshape(workers, nrow, SC_CHUNK))


def _moe_kernel(first_ref, last_ref, count_ref, total_ref,
                xs_hbm, wg_ref, wu_ref, wd_ref, out_hbm,
                xbuf0, xbuf1, obuf0, obuf1, xsem, osem):
    e = pl.program_id(0)
    ch = MOE_TILE
    half = ch // 2
    first, last, count, total = first_ref[e], last_ref[e], count_ref[e], total_ref[0]
    xbufs = (xbuf0, xbuf1)
    obufs = (obuf0, obuf1)

    def x_copy(c, s):
        rows = pl.ds(pl.multiple_of(c * ch, ch), ch)
        return pltpu.make_async_copy(xs_hbm.at[rows], xbufs[s], xsem.at[s])

    def o_copy(c, s):
        rows = pl.ds(pl.multiple_of(c * ch, ch), ch)
        return pltpu.make_async_copy(obufs[s], out_hbm.at[rows], osem.at[s])

    @pl.when(e == 0)
    def _():
        x_copy(0, 0).start()

    def experts_rows(s, rows, valid):
        row = lax.broadcasted_iota(jnp.int32, (rows, PACKED), 0)
        words = jnp.where(row < valid, xbufs[s][0:rows, :], 0)
        lo, hi = _unpack_rows(words)
        xs = jnp.concatenate([lo.astype(BF16), hi.astype(BF16)], axis=1)
        gate = jnp.dot(xs, wg_ref[0], preferred_element_type=F32)
        up = jnp.dot(xs, wu_ref[0], preferred_element_type=F32)
        act = (gate * jax.nn.sigmoid(gate) * up).astype(BF16)
        obufs[s][0:rows, :] = _pack_rows(jnp.dot(act, wd_ref[0], preferred_element_type=F32))

    def chunk_in_slot(c, s):
        x_copy(c, s).wait()

        @pl.when(c + 1 < total)
        def _():
            x_copy(c + 1, 1 - s).start()

        @pl.when(c >= 2)
        def _():
            o_copy(c - 2, s).wait()

        valid = count - (c - first) * ch

        sizes = (ch // 4, half, ch)
        for lo, rows in zip((0,) + sizes[:-1], sizes):
            @pl.when((valid > lo) & ((valid <= rows) | (rows == ch)))
            def _():
                experts_rows(s, rows, valid)
                if rows < ch:
                    obufs[s][rows:, :] = jnp.zeros((ch - rows, PACKED), I32)

        o_copy(c, s).start()

    def chunk(c, carry):
        for s in range(2):
            @pl.when(c % 2 == s)
            def _():
                chunk_in_slot(c, s)
        return carry

    lax.fori_loop(first, last, chunk, 0)

    @pl.when(e == pl.num_programs(0) - 1)
    def _():
        for s in range(2):
            @pl.when(((total - 1) % 2 == s) | (total >= 2))
            def _():
                o_copy(0, s).wait()


def _moe_experts(plan, xs, w_gate, w_up, w_down):
    nslot = xs.shape[0]
    any_spec = pl.BlockSpec(memory_space=pl.ANY)
    grid_spec = pltpu.PrefetchScalarGridSpec(
        num_scalar_prefetch=len(plan),
        grid=(N_EXPERTS,),
        in_specs=[
            any_spec,
            pl.BlockSpec((1, D_MODEL, EXPERT_FF), lambda e, *_: (e, 0, 0)),
            pl.BlockSpec((1, D_MODEL, EXPERT_FF), lambda e, *_: (e, 0, 0)),
            pl.BlockSpec((1, EXPERT_FF, D_MODEL), lambda e, *_: (e, 0, 0)),
        ],
        out_specs=any_spec,
        scratch_shapes=[
            pltpu.VMEM((MOE_TILE, PACKED), I32),
            pltpu.VMEM((MOE_TILE, PACKED), I32),
            pltpu.VMEM((MOE_TILE, PACKED), I32),
            pltpu.VMEM((MOE_TILE, PACKED), I32),
            pltpu.SemaphoreType.DMA((2,)),
            pltpu.SemaphoreType.DMA((2,)),
        ],
    )
    return pl.pallas_call(
        _moe_kernel,
        grid_spec=grid_spec,
        out_shape=jax.ShapeDtypeStruct((nslot, PACKED), I32),
        compiler_params=_ARB,
        name="moe_experts",
    )(*plan, xs, w_gate, w_up, w_down)


def _dispatch_plan(table, counts, tm):
    e_ids = table[0:2].astype(I32)
    ranks = table[4:6].astype(I32)
    counts = counts[0, :N_EXPERTS].astype(I32)
    padded = ((counts + tm - 1) // tm) * tm
    ends = jnp.cumsum(padded)
    starts = ends - padded
    experts = jnp.arange(N_EXPERTS, dtype=I32)
    slot = jnp.sum(jnp.where(e_ids[..., None] == experts, starts, 0), axis=-1) + ranks
    plan = (starts // tm, ends // tm, counts, ends[-1:] // tm)
    return slot, tuple(p.astype(I32) for p in plan)


def _final_kernel(x1_ref, y0_ref, y1_ref, route_ref, g_ref, *rest):
    out_ref = rest[-1]
    route = route_ref[...]
    w1 = route[:, 2:3]
    w2 = route[:, 3:4]
    x_lo, x_hi = _unpack_rows(x1_ref[...])
    a_lo, a_hi = _unpack_rows(y0_ref[...])
    b_lo, b_hi = _unpack_rows(y1_ref[...])
    x2 = jnp.concatenate([x_lo + (w1 * a_lo + w2 * b_lo), x_hi + (w1 * a_hi + w2 * b_hi)], axis=1)
    out_ref[...] = _rms(x2, g_ref[...])


def _final(x1, ycat, route, norm_g, tm, chunk, prev):
    t = x1.shape[0]
    nblk = t // tm // FINAL_CHUNKS
    off = chunk * nblk
    in_specs = [
        pl.BlockSpec((tm, PACKED), lambda i: (i + off, 0)),
        pl.BlockSpec((tm, PACKED), lambda i: (i, 0)),
        pl.BlockSpec((tm, PACKED), lambda i: (i + nblk, 0)),
        pl.BlockSpec((tm, LANES), lambda i: (i + off, 0)),
        _const_spec((1, D_MODEL)),
    ]
    args = [x1, ycat, ycat, route, norm_g]
    aliases = {}
    if prev is not None:
        in_specs.append(pl.BlockSpec(memory_space=pl.ANY))
        args.append(prev)
        aliases = {len(args) - 1: 0}
    return pl.pallas_call(
        _final_kernel,
        grid=(nblk,),
        in_specs=in_specs,
        out_specs=pl.BlockSpec((tm, D_MODEL), lambda i: (i + off, 0)),
        out_shape=jax.ShapeDtypeStruct((t, D_MODEL), F32),
        input_output_aliases=aliases,
        compiler_params=_ARB,
        name="final_norm",
    )(*args)


def kernel(x, norm_mix, w_in, conv_w, conv_b, w_conv_out, ssm_a_re, ssm_a_im, ssm_log_dt,
           ssm_b_re, ssm_b_im, ssm_c_re, ssm_c_im, ssm_d, w_glu_a, w_glu_b, gate_bias, w_o,
           norm_ffn, w_route_group, b_route_group, w_route_expert, b_route_expert,
           w_gate, w_up, w_down, norm_final):
    bsz, length, d = x.shape
    assert d == D_MODEL and norm_mix.shape[0] == 1
    t = bsz * length
    tm = TOKEN_TILE
    assert bsz == 1 and t % (FINAL_TILE * FINAL_CHUNKS) == 0
    xt = x.reshape(t, d)
    row = lambda a: a.reshape(1, -1).astype(F32)

    s5_weights = _s5_weights(
        ssm_a_re[0], ssm_a_im[0], ssm_log_dt[0], ssm_b_re[0], ssm_b_im[0],
        ssm_c_re[0], ssm_c_im[0])
    ma, gb, yg, wg_bf, wu_bf, wd_bf = _mixer_front(
        xt, row(norm_mix[0]), w_in[0], conv_w[0].reshape(3, CONV_WIDTH),
        row(conv_b[0]), w_conv_out[0], gate_bias[0], s5_weights, row(ssm_d[0]),
        w_gate[0], w_up[0], w_down[0], FRONT_TILE)

    pad = LANES - N_EXPERTS - N_EXPERT_GROUPS
    w_route = jnp.concatenate(
        [w_route_expert[0], w_route_group[0], jnp.zeros((d, pad), F32)], axis=1)
    b_route = jnp.concatenate(
        [b_route_expert[0], b_route_group[0], jnp.zeros((pad,), F32)]).reshape(1, LANES)
    x1, h2p, route, table, counts = _mixer_back(
        yg, ma, gb, xt, w_glu_a[0], w_glu_b[0], w_o[0], row(norm_ffn[0]), w_route, b_route, tm)

    slot, plan = _dispatch_plan(table, counts, MOE_TILE)
    nslot = 2 * t + N_EXPERTS * MOE_TILE
    xs = _sc_dispatch(h2p, slot[0], slot[1], nslot)
    ys = _moe_experts(plan, xs, wg_bf, wu_bf, wd_bf)
    out = None
    tc = t // FINAL_CHUNKS
    for k in range(FINAL_CHUNKS):
        ycat = _sc_gather(ys, slot[:, k * tc:(k + 1) * tc].reshape(-1))
        out = _final(x1, ycat, route, row(norm_final), FINAL_TILE, k, out)
    return out.reshape(bsz, length, d)
```
